```python
import math
import jax, jax.numpy as jnp
from jax import lax
import numpy as np

D_MODEL = 1024
BATCH = 8
SEQ = 2048
DEPTH = 4

CHUNK = 64
Q_BLOCK = 128
N_EVEN = (DEPTH + 1) // 2
N_ODD = DEPTH // 2

CONV_WIDTH = D_MODEL // 2
CONV_KERNEL = 31
RWKV_WIDTH = D_MODEL // 2
RWKV_HEAD = 64
RWKV_HEADS = RWKV_WIDTH // RWKV_HEAD
DECAY_LORA = 64
AAA_LORA = 64
GATE_LORA = 160
RWKV_IN = 3 * RWKV_WIDTH + DECAY_LORA + AAA_LORA + GATE_LORA
EVEN_IN = 2 * CONV_WIDTH + RWKV_IN
DIFF_HEADS = 8
DIFF_HEAD_DIM = 64
DIFF_V_DIM = 2 * DIFF_HEAD_DIM
DIFF_WIDTH = DIFF_HEADS * DIFF_V_DIM
ROPE_DIM = DIFF_HEAD_DIM // 4
ROPE_THETA = 500000.0
PEER_HEADS = 8
PEER_KEYS = 128
PEER_EXPERTS = PEER_KEYS * PEER_KEYS
PEER_QDIM = 256
PEER_HALF = PEER_QDIM // 2
PEER_TOPK = 16
PEER_TOKEN_BLOCK = 128

NORM_EPS = 1e-6
LN_EPS = 1e-5
RWKV_GN_EPS = 64e-5

kernel_name = "hybrid_conv_rwkv7_diffattn_peer_trunk"


def rms_norm(x, g):
    xf = x.astype(jnp.float32)
    y = xf * lax.rsqrt(jnp.mean(xf * xf, axis=-1, keepdims=True) + NORM_EPS)
    return (y * g.astype(jnp.float32)).astype(x.dtype)


def layer_norm(x, g, b, eps):
    xf = x.astype(jnp.float32)
    mu = jnp.mean(xf, axis=-1, keepdims=True)
    var = jnp.mean(jnp.square(xf - mu), axis=-1, keepdims=True)
    y = (xf - mu) * lax.rsqrt(var + eps)
    return y.astype(x.dtype)[..., :] * g + b if g is not None else y.astype(x.dtype)


def modulate(x, g, shift, scale):
    return rms_norm(x, g) * (1 + scale[:, None, :]) + shift[:, None, :]


def token_shift(z):
    return jnp.pad(z, ((0, 0), (1, 0), (0, 0)))[:, :-1]


def conformer_conv(z, conv_w, conv_b, ln_g, ln_b):
    a, gate = jnp.split(z, 2, axis=-1)
    u = a * jax.nn.sigmoid(gate)
    u = lax.conv_general_dilated(
        u, conv_w[:, None, :].astype(u.dtype), window_strides=(1,),
        padding=[(CONV_KERNEL - 1, 0)], dimension_numbers=('NWC', 'WIO', 'NWC'),
        feature_group_count=CONV_WIDTH) + conv_b
    u = layer_norm(u, ln_g, ln_b, LN_EPS)
    return jax.nn.silu(u)


def wkv7_scan(r, w, k, v, a, b):
    Bn, S, H, N = r.shape
    tm = lambda t: jnp.moveaxis(t.astype(jnp.float32), 1, 0)

    def step(state, inp):
        r_t, w_t, k_t, v_t, a_t, b_t = inp
        sa = jnp.einsum('bhvk,bhk->bhv', state, a_t)
        state = (state * w_t[:, :, None, :] + sa[..., None] * b_t[:, :, None, :]
                 + v_t[..., None] * k_t[:, :, None, :])
        y = jnp.einsum('bhvk,bhk->bhv', state, r_t)
        return state, y

    s0 = jnp.zeros((Bn, H, N, N), jnp.float32)
    _, y = lax.scan(step, s0, (tm(r), tm(w), tm(k), tm(v), tm(a), tm(b)))
    return jnp.moveaxis(y, 0, 1)


def rwkv7_time_mix(z, mu, w0, w2, a0, a2, g2, k_k, k_a, r_k, ln_g, ln_b):
    Bn, S, _ = z.shape
    z = z + (token_shift(z) - z) * mu
    RW = RWKV_WIDTH
    r, k, v, wd, ad, gd = jnp.split(
        z, [RW, 2 * RW, 3 * RW, 3 * RW + DECAY_LORA, 3 * RW + DECAY_LORA + AAA_LORA], axis=-1)
    w = -jax.nn.softplus(-(w0 + jnp.tanh(wd) @ w2)) - 0.5
    decay = jnp.exp(-jnp.exp(w.astype(jnp.float32)))
    a = jax.nn.sigmoid(a0 + ad @ a2)
    g = jax.nn.sigmoid(gd) @ g2
    hs = lambda t: t.reshape(Bn, S, RWKV_HEADS, RWKV_HEAD)
    kkf = hs(k * k_k).astype(jnp.float32)
    kk = kkf / jnp.maximum(jnp.sqrt(jnp.sum(kkf * kkf, axis=-1, keepdims=True)), 1e-12)
    k = k * (1 + (a - 1) * k_a)
    r_h, k_h, v_h, a_h = hs(r), hs(k), hs(v), hs(a)
    y = wkv7_scan(r_h, hs(decay), k_h, v_h, -kk, kk * a_h.astype(jnp.float32))
    mu_y = jnp.mean(y, axis=-1, keepdims=True)
    var_y = jnp.mean(jnp.square(y - mu_y), axis=-1, keepdims=True)
    y = ((y - mu_y) * lax.rsqrt(var_y + RWKV_GN_EPS)).reshape(Bn, S, RW).astype(z.dtype)
    y = y * ln_g + ln_b
    bonus = jnp.sum(r_h * k_h * r_k, axis=-1, keepdims=True) * v_h
    return (y + bonus.reshape(Bn, S, RW)) * g


def rope_partial(x, pos):
    half = ROPE_DIM // 2
    inv = 1.0 / (ROPE_THETA ** (jnp.arange(0, ROPE_DIM, 2, dtype=jnp.float32) / ROPE_DIM))
    ang = pos.astype(jnp.float32)[..., None] * inv
    cos, sin = jnp.cos(ang)[:, :, None, :], jnp.sin(ang)[:, :, None, :]
    xr = x[..., :ROPE_DIM].astype(jnp.float32)
    x1, x2 = xr[..., :half], xr[..., half:]
    rot = jnp.concatenate([x1 * cos - x2 * sin, x1 * sin + x2 * cos], axis=-1)
    return jnp.concatenate([rot.astype(x.dtype), x[..., ROPE_DIM:]], axis=-1)


def diff_attention(h, pos, w_qkv, lq1, lk1, lq2, lk2, subln_g, lambda_init):
    Bn, S, _ = h.shape
    H, Dh = DIFF_HEADS, DIFF_HEAD_DIM
    q, k, v = jnp.split(h @ w_qkv, 3, axis=-1)
    q = rope_partial(q.reshape(Bn, S, 2 * H, Dh), pos).reshape(Bn, S, H, 2, Dh) * (Dh ** -0.5)
    k = rope_partial(k.reshape(Bn, S, 2 * H, Dh), pos).reshape(Bn, S, H, 2, Dh)
    v = v.reshape(Bn, S, H, DIFF_V_DIM)
    f32 = lambda t: t.astype(jnp.float32)
    lam = (jnp.exp(jnp.sum(f32(lq1) * f32(lk1))) - jnp.exp(jnp.sum(f32(lq2) * f32(lk2)))
           + lambda_init)
    outs = []
    for i in range(S // Q_BLOCK):
        q0, kend = i * Q_BLOCK, (i + 1) * Q_BLOCK
        s = jnp.einsum('bqhmd,bkhmd->bhmqk', q[:, q0:kend], k[:, :kend]).astype(jnp.float32)
        mask = (jnp.arange(kend) // CHUNK)[None, :] <= (jnp.arange(q0, kend) // CHUNK)[:, None]
        p = jax.nn.softmax(jnp.where(mask, s, -jnp.inf), axis=-1)
        p = p[:, :, 0] - lam * p[:, :, 1]
        outs.append(jnp.einsum('bhqk,bkhe->bqhe', p.astype(v.dtype), v[:, :kend]))
    o = jnp.concatenate(outs, axis=1)
    o = rms_norm(o, subln_g) * (1 - lambda_init)
    return o.reshape(Bn, S, DIFF_WIDTH)


def peer(h, w_q, subkeys, u_tab, v_tab):
    Bn, S, D = h.shape
    blocks = h.reshape(-1, PEER_TOKEN_BLOCK, D)

    def block(hb):
        q = (hb @ w_q).reshape(PEER_TOKEN_BLOCK, PEER_HEADS, 2, PEER_HALF)
        s = jnp.einsum('thpc,pnc->thpn', q, subkeys)
        s1, i1 = lax.top_k(s[:, :, 0], PEER_TOPK)
        s2, i2 = lax.top_k(s[:, :, 1], PEER_TOPK)
        cand = (s1[..., :, None] + s2[..., None, :]).reshape(
            PEER_TOKEN_BLOCK, PEER_HEADS, PEER_TOPK * PEER_TOPK)
        top_s, top_i = lax.top_k(cand, PEER_TOPK)
        e = (jnp.take_along_axis(i1, top_i // PEER_TOPK, axis=-1) * PEER_KEYS
             + jnp.take_along_axis(i2, top_i % PEER_TOPK, axis=-1))
        gate = jax.nn.softmax(top_s.astype(jnp.float32), axis=-1).astype(hb.dtype)
        act = jax.nn.gelu(jnp.einsum('thkd,td->thk', u_tab[e], hb), approximate=False) * gate
        return jnp.einsum('thk,thkd->td', act, v_tab[e])

    return lax.map(block, blocks).reshape(Bn, S, D)


def setup_inputs(seed: int = 0) -> dict:
    key = jax.random.key(seed)
    ks = iter(list(jax.random.split(key, 48)))
    D = D_MODEL
    nrm = lambda shape, scale: jax.random.normal(next(ks), shape, jnp.float32) * scale
    uni = lambda shape: jax.random.uniform(next(ks), shape, jnp.float32)
    gain = lambda shape: 1.0 + nrm(shape, 0.02)
    x = nrm((BATCH, SEQ, D), 1.0)
    c = nrm((BATCH, D), 1.0)
    start = jax.random.randint(next(ks), (BATCH, 1), 0, 4096, dtype=jnp.int32)
    positions = start + jnp.arange(SEQ, dtype=jnp.int32)[None, :]
    return {
        "x": x, "c": c, "positions": positions,
        "ada_w": nrm((DEPTH, D, 6 * D), 0.5 * D ** -0.5),
        "ada_b": nrm((DEPTH, 6 * D), 0.02),
        "norm_mix_g": gain((DEPTH, D)),
        "norm_ffn_g": gain((DEPTH, D)),
        "hyb_w_in": nrm((N_EVEN, D, EVEN_IN), D ** -0.5),
        "conv_w": nrm((N_EVEN, CONV_KERNEL, CONV_WIDTH), CONV_KERNEL ** -0.5),
        "conv_b": nrm((N_EVEN, CONV_WIDTH), 0.02),
        "conv_ln_g": gain((N_EVEN, CONV_WIDTH)),
        "conv_ln_b": nrm((N_EVEN, CONV_WIDTH), 0.02),
        "rwkv_mu": uni((N_EVEN, RWKV_IN)),
        "rwkv_w0": -6.0 + 5.0 * uni((N_EVEN, RWKV_WIDTH)),
        "rwkv_w2": nrm((N_EVEN, DECAY_LORA, RWKV_WIDTH), 0.5 * DECAY_LORA ** -0.5),
        "rwkv_a0": nrm((N_EVEN, RWKV_WIDTH), 0.1),
        "rwkv_a2": nrm((N_EVEN, AAA_LORA, RWKV_WIDTH), AAA_LORA ** -0.5),
        "rwkv_g2": nrm((N_EVEN, GATE_LORA, RWKV_WIDTH), GATE_LORA ** -0.5),
        "rwkv_k_k": 0.85 + nrm((N_EVEN, RWKV_WIDTH), 0.02),
        "rwkv_k_a": gain((N_EVEN, RWKV_WIDTH)),
        "rwkv_r_k": nrm((N_EVEN, RWKV_HEADS, RWKV_HEAD), 0.1),
        "rwkv_ln_g": gain((N_EVEN, RWKV_WIDTH)),
        "rwkv_ln_b": nrm((N_EVEN, RWKV_WIDTH), 0.02),
        "hyb_w_out": nrm((N_EVEN, CONV_WIDTH + RWKV_WIDTH, D), (CONV_WIDTH + RWKV_WIDTH) ** -0.5),
        "diff_w_qkv": nrm((N_ODD, D, 3 * DIFF_WIDTH), D ** -0.5),
        "diff_lq1": nrm((N_ODD, DIFF_HEAD_DIM), 0.1),
        "diff_lk1": nrm((N_ODD, DIFF_HEAD_DIM), 0.1),
        "diff_lq2": nrm((N_ODD, DIFF_HEAD_DIM), 0.1),
        "diff_lk2": nrm((N_ODD, DIFF_HEAD_DIM), 0.1),
        "diff_subln_g": gain((N_ODD, DIFF_V_DIM)),
        "diff_w_out": nrm((N_ODD, DIFF_WIDTH, D), DIFF_WIDTH ** -0.5),
        "peer_w_q": nrm((DEPTH, D, PEER_HEADS * PEER_QDIM), D ** -0.5),
        "peer_subkeys": nrm((DEPTH, 2, PEER_KEYS, PEER_HALF), PEER_HALF ** -0.5),
        "peer_u": nrm((DEPTH, PEER_EXPERTS, D), D ** -0.5),
        "peer_v": nrm((DEPTH, PEER_EXPERTS, D), 0.5),
        "final_g": gain((D,)),
    }


def reference(x, c, positions, ada_w, ada_b, norm_mix_g, norm_ffn_g, hyb_w_in, conv_w, conv_b,
              conv_ln_g, conv_ln_b, rwkv_mu, rwkv_w0, rwkv_w2, rwkv_a0, rwkv_a2, rwkv_g2,
              rwkv_k_k, rwkv_k_a, rwkv_r_k, rwkv_ln_g, rwkv_ln_b, hyb_w_out, diff_w_qkv,
              diff_lq1, diff_lk1, diff_lq2, diff_lk2, diff_subln_g, diff_w_out, peer_w_q,
              peer_subkeys, peer_u, peer_v, final_g):
    cond = jax.nn.silu(c)
    for l in range(DEPTH):
        mod = cond @ ada_w[l] + ada_b[l]
        sh1, sc1, g1, sh2, sc2, g2 = jnp.split(mod, 6, axis=-1)
        h = modulate(x, norm_mix_g[l], sh1, sc1)
        if l % 2 == 0:
            e = l // 2
            z = h @ hyb_w_in[e]
            ya = conformer_conv(z[..., :2 * CONV_WIDTH], conv_w[e], conv_b[e],
                                conv_ln_g[e], conv_ln_b[e])
            yb = rwkv7_time_mix(z[..., 2 * CONV_WIDTH:], rwkv_mu[e], rwkv_w0[e], rwkv_w2[e],
                                rwkv_a0[e], rwkv_a2[e], rwkv_g2[e], rwkv_k_k[e], rwkv_k_a[e],
                                rwkv_r_k[e], rwkv_ln_g[e], rwkv_ln_b[e])
            y = jnp.concatenate([ya, yb], axis=-1) @ hyb_w_out[e]
        else:
            o = l // 2
            lambda_init = 0.8 - 0.6 * math.exp(-0.3 * l)
            y = diff_attention(h, positions, diff_w_qkv[o], diff_lq1[o], diff_lk1[o],
                               diff_lq2[o], diff_lk2[o], diff_subln_g[o], lambda_init) @ diff_w_out[o]
        x = x + g1[:, None, :] * y
        h = modulate(x, norm_ffn_g[l], sh2, sc2)
        x = x + g2[:, None, :] * peer(h, peer_w_q[l], peer_subkeys[l], peer_u[l], peer_v[l])
    return rms_norm(x, final_g)
```

```python
import functools
import math

import jax
import jax.numpy as jnp
from jax import lax
from jax.experimental import pallas as pl
from jax.experimental.pallas import tpu as pltpu

F32 = jnp.float32
BF16 = jnp.bfloat16
HIGHEST = lax.Precision.HIGHEST

CONV_WIDTH = 512
CONV_KERNEL = 31
RWKV_WIDTH = 512
RWKV_HEAD = 64
RWKV_HEADS = 8
DECAY_LORA = 64
AAA_LORA = 64
GATE_LORA = 160
DIFF_HEADS = 8
DIFF_HEAD_DIM = 64
DIFF_V_DIM = 128
ROPE_DIM = 16
ROPE_THETA = 500000.0
ATTN_CHUNK = 64
PEER_HEADS = 8
PEER_KEYS = 128
PEER_HALF = 128
PEER_TOPK = 16
NORM_EPS = 1e-6
LN_EPS = 1e-5
RWKV_GN_EPS = 64e-5

LANES = 128
SCAN_CHUNK = 64
CONV_HALO = 32
VMEM_LIMIT = 48 * 1024 * 1024


def _cparams(semantics):
    return pltpu.CompilerParams(dimension_semantics=semantics, vmem_limit_bytes=VMEM_LIMIT)


def _nt_dot(a, b, precision=None):
    return lax.dot_general(a, b, (((1,), (1,)), ((), ())), precision=precision,
                           preferred_element_type=F32)


def _dot(a, b, precision=None):
    return jnp.dot(a, b, precision=precision, preferred_element_type=F32)


def _sigmoid(x):
    return 1.0 / (1.0 + jnp.exp(-x))


def _ada_kernel(c_ref, w_ref, b_ref, o_ref):
    c = c_ref[...]
    cond = c * _sigmoid(c)
    o_ref[0] = _dot(cond, w_ref[0], HIGHEST) + b_ref[0]


def _ada_call(c, ada_w, ada_b):
    depth, d, n = ada_w.shape
    b = c.shape[0]
    tn = 1536
    return pl.pallas_call(
        _ada_kernel,
        grid=(depth, n // tn),
        in_specs=[
            pl.BlockSpec((b, d), lambda l, j: (0, 0)),
            pl.BlockSpec((1, d, tn), lambda l, j: (l, 0, j)),
            pl.BlockSpec((1, 1, tn), lambda l, j: (l, 0, j)),
        ],
        out_specs=pl.BlockSpec((1, b, tn), lambda l, j: (l, 0, j)),
        out_shape=jax.ShapeDtypeStruct((depth, b, n), F32),
        compiler_params=_cparams(("arbitrary", "arbitrary")),
        name="ada_mod",
    )(c, ada_w, ada_b.reshape(depth, 1, n))


def _modulate(x, g, shift, scale):
    ms = jnp.mean(x * x, axis=-1, keepdims=True)
    y = x * lax.rsqrt(ms + NORM_EPS)
    return (y * g) * (1.0 + scale) + shift


def _norm_matmul_kernel(*refs, n_w, exact, emit_h):
    x_ref, g_ref, sh_ref, sc_ref = refs[:4]
    w_refs = refs[4:4 + n_w]
    o_refs = refs[4 + n_w:]
    h = _modulate(x_ref[...], g_ref[...], sh_ref[0], sc_ref[0])
    hb = h.astype(BF16)
    for w_ref, o_ref in zip(w_refs, o_refs[:n_w]):
        if exact:
            o_ref[...] = _dot(h, w_ref[...], HIGHEST).astype(o_ref.dtype)
        else:
            o_ref[...] = _dot(hb, w_ref[...]).astype(o_ref.dtype)
    if emit_h:
        o_refs[n_w][...] = hb


def _norm_matmul_call(x, g, shift, scale, weights, out_dtypes, seq, *, exact=False, emit_h=False,
                      tm=512, name="norm_matmul"):
    t, d = x.shape
    in_specs = [
        pl.BlockSpec((tm, d), lambda i: (i, 0)),
        pl.BlockSpec((1, d), lambda i: (0, 0)),
        pl.BlockSpec((1, 1, d), lambda i: ((i * tm) // seq, 0, 0)),
        pl.BlockSpec((1, 1, d), lambda i: ((i * tm) // seq, 0, 0)),
    ]
    out_specs, out_shape = [], []
    for w, dt in zip(weights, out_dtypes):
        n = w.shape[1]
        in_specs.append(pl.BlockSpec((d, n), lambda i: (0, 0)))
        out_specs.append(pl.BlockSpec((tm, n), lambda i: (i, 0)))
        out_shape.append(jax.ShapeDtypeStruct((t, n), dt))
    if emit_h:
        out_specs.append(pl.BlockSpec((tm, d), lambda i: (i, 0)))
        out_shape.append(jax.ShapeDtypeStruct((t, d), BF16))
    return pl.pallas_call(
        functools.partial(_norm_matmul_kernel, n_w=len(weights), exact=exact, emit_h=emit_h),
        grid=(t // tm,),
        in_specs=in_specs,
        out_specs=out_specs,
        out_shape=out_shape,
        compiler_params=_cparams(("arbitrary",)),
        name=name,
    )(x, g, shift, scale, *weights)


def _out_proj_kernel(*refs, n_y):
    x_ref, gate_ref = refs[:2]
    y_refs = refs[2:2 + n_y]
    w_refs = refs[2 + n_y:2 + 2 * n_y]
    o_ref = refs[2 + 2 * n_y]
    acc = _dot(y_refs[0][...], w_refs[0][...])
    for y_ref, w_ref in zip(y_refs[1:], w_refs[1:]):
        acc = acc + _dot(y_ref[...], w_ref[...])
    o_ref[...] = x_ref[...] + gate_ref[0] * acc


def _out_proj_call(x, gate, ys, ws, seq, *, tm=512):
    t, d = x.shape
    in_specs = [
        pl.BlockSpec((tm, d), lambda i: (i, 0)),
        pl.BlockSpec((1, 1, d), lambda i: ((i * tm) // seq, 0, 0)),
    ]
    for y in ys:
        in_specs.append(pl.BlockSpec((tm, y.shape[1]), lambda i: (i, 0)))
    for w in ws:
        in_specs.append(pl.BlockSpec(w.shape, lambda i: (0, 0)))
    return pl.pallas_call(
        functools.partial(_out_proj_kernel, n_y=len(ys)),
        grid=(t // tm,),
        in_specs=in_specs,
        out_specs=pl.BlockSpec((tm, d), lambda i: (i, 0)),
        out_shape=jax.ShapeDtypeStruct((t, d), F32),
        compiler_params=_cparams(("arbitrary",)),
        name="out_proj",
    )(x, gate, *ys, *ws)


def _conv_kernel(z_ref, w_ref, b_ref, g_ref, beta_ref, o_ref, ext_ref, *, ts):
    width = CONV_WIDTH

    @pl.when(pl.program_id(1) == 0)
    def _():
        ext_ref[0:CONV_HALO, :] = jnp.zeros((CONV_HALO, width), F32)

    z = z_ref[...]
    u = z[:, :width] * _sigmoid(z[:, width:])
    ext_ref[CONV_HALO:CONV_HALO + ts, :] = u
    base = CONV_HALO - (CONV_KERNEL - 1)
    rows = 64
    for cb in range(width // LANES):
        cs = slice(cb * LANES, (cb + 1) * LANES)
        for rb in range(ts // rows):
            acc = jnp.zeros((rows, LANES), F32)
            for j in range(CONV_KERNEL):
                start = rb * rows + base + j
                acc = acc + w_ref[j:j + 1, cs] * ext_ref[start:start + rows, cs]
            o_ref[rb * rows:(rb + 1) * rows, cs] = acc
    conv = o_ref[...] + b_ref[...]
    mu = jnp.mean(conv, axis=-1, keepdims=True)
    dlt = conv - mu
    var = jnp.mean(dlt * dlt, axis=-1, keepdims=True)
    y = dlt * lax.rsqrt(var + LN_EPS) * g_ref[...] + beta_ref[...]
    o_ref[...] = y * _sigmoid(y)
    ext_ref[0:CONV_HALO, :] = ext_ref[ts:ts + CONV_HALO, :]


def _conv_call(z, conv_w, conv_b, ln_g, ln_b, batch, seq, *, ts=256):
    t = z.shape[0]
    width = CONV_WIDTH
    nts = seq // ts
    vec = lambda a: a.reshape(1, width)
    return pl.pallas_call(
        functools.partial(_conv_kernel, ts=ts),
        grid=(batch, nts),
        in_specs=[
            pl.BlockSpec((ts, 2 * width), lambda b, i: (b * nts + i, 0)),
            pl.BlockSpec((CONV_KERNEL, width), lambda b, i: (0, 0)),
            pl.BlockSpec((1, width), lambda b, i: (0, 0)),
            pl.BlockSpec((1, width), lambda b, i: (0, 0)),
            pl.BlockSpec((1, width), lambda b, i: (0, 0)),
        ],
        out_specs=pl.BlockSpec((ts, width), lambda b, i: (b * nts + i, 0)),
        out_shape=jax.ShapeDtypeStruct((t, width), F32),
        scratch_shapes=[pltpu.VMEM((ts + CONV_HALO, width), F32)],
        compiler_params=_cparams(("arbitrary", "arbitrary")),
        name="conformer_conv",
    )(z, conv_w, vec(conv_b), vec(ln_g), vec(ln_b))


def _softplus(x):
    return jnp.maximum(x, 0.0) + jnp.log(1.0 + jnp.exp(-jnp.abs(x)))


def _unit_lower_inverse(a, row, col):
    n = a.shape[0]
    eye = (row == col).astype(F32)
    blk = lambda m: (row // m) == (col // m)
    bdot = lambda p, q: _dot(p.astype(BF16), q.astype(BF16))
    n1 = jnp.where(blk(8), a, 0.0)
    t = eye + n1
    n2 = bdot(n1, n1)
    t = t + bdot(t, n2)
    n4 = bdot(n2, n2)
    t = t + bdot(t, n4)
    m = 8
    while m < n:
        e = jnp.where(blk(2 * m) & jnp.logical_not(blk(m)), a, 0.0)
        t = t + bdot(t, bdot(e, t))
        m *= 2
    return t


def _rwkv_kernel(zr_ref, zl_ref, mur_ref, mul_ref, w0_ref, a0_ref, kk_ref, ka_ref, rk_ref,
                 lng_ref, lnb_ref, w2_ref, a2_ref, g2_ref, bd_ref, o_ref,
                 extr_ref, extl_ref, state_ref, y_ref):
    c = SCAN_CHUNK
    rw = RWKV_WIDTH
    n = RWKV_HEAD

    @pl.when(pl.program_id(1) == 0)
    def _():
        extr_ref[0:8, :] = jnp.zeros((8, 3 * rw), F32)
        extl_ref[0:8, :] = jnp.zeros((8, rw), F32)
        state_ref[...] = jnp.zeros_like(state_ref)

    zr = zr_ref[...]
    zl = zl_ref[...]
    extr_ref[8:8 + c, :] = zr
    extl_ref[8:8 + c, :] = zl
    zr = zr + (extr_ref[7:7 + c, :] - zr) * mur_ref[...]
    zl = zl + (extl_ref[7:7 + c, :] - zl) * mul_ref[...]
    extr_ref[0:8, :] = extr_ref[c:c + 8, :]
    extl_ref[0:8, :] = extl_ref[c:c + 8, :]

    r = zr[:, 0:rw]
    k = zr[:, rw:2 * rw]
    v = zr[:, 2 * rw:3 * rw]
    wd = zl[:, 0:LANES]
    ad = zl[:, LANES:2 * LANES]
    gd = zl[:, 2 * LANES:4 * LANES]
    bd = bd_ref[...]

    w_raw = -_softplus(-(w0_ref[...] + _dot(jnp.tanh(wd).astype(BF16), w2_ref[...]))) - 0.5
    lw = -jnp.exp(w_raw)
    alpha = _sigmoid(a0_ref[...] + _dot(ad.astype(BF16), a2_ref[...]))
    gate = _dot(_sigmoid(gd).astype(BF16), g2_ref[...])
    kkf = k * kk_ref[...]
    ss = _dot(kkf * kkf, bd, HIGHEST)
    kk = kkf / jnp.maximum(jnp.sqrt(ss), 1e-12)
    kp = k * (1.0 + (alpha - 1.0) * ka_ref[...])
    bonus = _dot(r * kp * rk_ref[...], bd, HIGHEST) * v
    av = -kk
    bv = kk * alpha

    row = lax.broadcasted_iota(jnp.int32, (c, c), 0)
    col = lax.broadcasted_iota(jnp.int32, (c, c), 1)
    incl = (col <= row)
    strict = (col < row)
    cum = _dot(incl.astype(F32), lw, HIGHEST)
    tot = cum[c - 1:c, :]
    g_in = jnp.exp(cum)
    g_inp = jnp.exp(cum - lw)
    g_out = jnp.exp(-cum)
    g_end = jnp.exp(tot - cum)
    g_tot = jnp.exp(tot)
    a_in = av * g_inp
    r_in = r * g_in
    b_out = bv * g_out
    k_out = kp * g_out
    b_end = bv * g_end
    k_end = kp * g_end

    for h in range(RWKV_HEADS):
        hs = slice(h * n, (h + 1) * n)
        p = jnp.concatenate([a_in[:, hs], r_in[:, hs]], axis=0).astype(BF16)
        q = jnp.concatenate([b_out[:, hs], k_out[:, hs]], axis=0).astype(BF16)
        vh = v[:, hs].astype(BF16)
        m = _nt_dot(p, q)
        a_ab = jnp.where(strict, m[:c, :c], 0.0)
        a_ak = jnp.where(strict, m[:c, c:], 0.0)
        a_rb = jnp.where(incl, m[c:, :c], 0.0)
        a_rk = jnp.where(incl, m[c:, c:], 0.0)
        st = state_ref[h]
        ph = _nt_dot(p, st.astype(BF16))
        tinv = _unit_lower_inverse(a_ab, row, col)
        rhs = ph[:c] + _dot(a_ak.astype(BF16), vh)
        u = _dot(tinv.astype(BF16), rhs.astype(BF16))
        ub = u.astype(BF16)
        yh = ph[c:] + _dot(a_rb.astype(BF16), ub) + _dot(a_rk.astype(BF16), vh)
        y_ref[:, hs] = yh
        uv_t = jnp.concatenate([u.T, v[:, hs].T], axis=1).astype(BF16)
        bk = jnp.concatenate([b_end[:, hs], k_end[:, hs]], axis=0).astype(BF16)
        state_ref[h] = st * g_tot[:, hs] + _dot(uv_t, bk)

    y = y_ref[...]
    inv_n = 1.0 / n
    mu_y = _dot(y, bd, HIGHEST) * inv_n
    dy = y - mu_y
    var_y = _dot(dy * dy, bd, HIGHEST) * inv_n
    yn = dy * lax.rsqrt(var_y + RWKV_GN_EPS)
    o_ref[...] = (yn * lng_ref[...] + lnb_ref[...] + bonus) * gate


def _rwkv_call(zr, zl, p, batch, seq):
    t = zr.shape[0]
    c = SCAN_CHUNK
    rw = RWKV_WIDTH
    nc = seq // c
    const = lambda shape: pl.BlockSpec(shape, lambda b, i: (0,) * len(shape))
    vec = const((1, rw))
    return pl.pallas_call(
        _rwkv_kernel,
        grid=(batch, nc),
        in_specs=[
            pl.BlockSpec((c, 3 * rw), lambda b, i: (b * nc + i, 0)),
            pl.BlockSpec((c, rw), lambda b, i: (b * nc + i, 0)),
            const((1, 3 * rw)), vec,
            vec, vec, vec, vec, vec, vec, vec,
            const((LANES, rw)), const((LANES, rw)), const((2 * LANES, rw)),
            const((rw, rw)),
        ],
        out_specs=pl.BlockSpec((c, rw), lambda b, i: (b * nc + i, 0)),
        out_shape=jax.ShapeDtypeStruct((t, rw), F32),
        scratch_shapes=[
            pltpu.VMEM((c + 8, 3 * rw), F32),
            pltpu.VMEM((c + 8, rw), F32),
            pltpu.VMEM((RWKV_HEADS, RWKV_HEAD, RWKV_HEAD), F32),
            pltpu.VMEM((c, rw), F32),
        ],
        compiler_params=_cparams(("arbitrary", "arbitrary")),
        name="rwkv7_mix",
    )(zr, zl, p["mu_r"], p["mu_l"], p["w0"], p["a0"], p["k_k"], p["k_a"], p["r_k"],
      p["ln_g"], p["ln_b"], p["w2"], p["a2"], p["g2"], p["bd"])


def _qkv_rope_kernel(x_ref, g_ref, sh_ref, sc_ref, pos_ref, inv_ref, sgn_ref, wq_ref, wk_ref, wv_ref,
                     q_ref, k_ref, v_ref):
    h = _modulate(x_ref[...], g_ref[...], sh_ref[0], sc_ref[0]).astype(BF16)
    ang = pos_ref[...].astype(F32) * inv_ref[...]
    cosf = jnp.cos(ang)
    sinf = jnp.sin(ang) * sgn_ref[...]
    lane = lax.broadcasted_iota(jnp.int32, ang.shape, 1)
    low = (lane % DIFF_HEAD_DIM) < (ROPE_DIM // 2)
    half = ROPE_DIM // 2

    def rope(w_ref, o_ref, scale):
        z = _dot(h, w_ref[...])
        for cb in range(z.shape[1] // LANES):
            zc = z[:, cb * LANES:(cb + 1) * LANES]
            partner = jnp.where(low, pltpu.roll(zc, LANES - half, axis=1), pltpu.roll(zc, half, axis=1))
            o_ref[:, cb * LANES:(cb + 1) * LANES] = ((zc * cosf + partner * sinf) * scale).astype(o_ref.dtype)

    rope(wq_ref, q_ref, DIFF_HEAD_DIM ** -0.5)
    rope(wk_ref, k_ref, 1.0)
    v_ref[...] = _dot(h, wv_ref[...]).astype(v_ref.dtype)


def _qkv_rope_call(x, g, shift, scale, pos, wq, wk, wv, seq, *, tm=512):
    t, d = x.shape
    half = ROPE_DIM // 2
    inv = 1.0 / (ROPE_THETA ** (jnp.arange(0, ROPE_DIM, 2, dtype=F32) / ROPE_DIM))
    dl = jnp.arange(LANES) % DIFF_HEAD_DIM
    inv_pat = jnp.where(dl < ROPE_DIM, inv[dl % half], 0.0).reshape(1, LANES).astype(F32)
    sgn_pat = jnp.where(dl < half, -1.0, 1.0).reshape(1, LANES).astype(F32)
    row = lambda n: pl.BlockSpec((tm, n), lambda i: (i, 0))
    full = lambda a: pl.BlockSpec(a.shape, lambda i: (0, 0))
    mod = pl.BlockSpec((1, 1, d), lambda i: ((i * tm) // seq, 0, 0))
    n = wq.shape[1]
    return pl.pallas_call(
        _qkv_rope_kernel,
        grid=(t // tm,),
        in_specs=[row(d), full(g), mod, mod, row(1), full(inv_pat), full(sgn_pat),
                  full(wq), full(wk), full(wv)],
        out_specs=[row(n), row(n), row(n)],
        out_shape=[jax.ShapeDtypeStruct((t, n), BF16)] * 3,
        compiler_params=_cparams(("arbitrary",)),
        name="qkv_rope",
    )(x, g, shift, scale, pos, inv_pat, sgn_pat, wq, wk, wv)


def _attn_kernel(q_ref, k_ref, v_ref, lq1_ref, lk1_ref, lq2_ref, lk2_ref, sg_ref, o_ref,
                 m_ref, l_ref, acc_ref, *, tq, lambda_init):
    i = pl.program_id(2)
    q = q_ref[...]
    lane = lax.broadcasted_iota(jnp.int32, q.shape, 1)
    zero = jnp.zeros_like(q)
    qm = (jnp.where(lane < DIFF_HEAD_DIM, q, zero), jnp.where(lane >= DIFF_HEAD_DIM, q, zero))

    m_ref[...] = jnp.full_like(m_ref, -jnp.inf)
    l_ref[...] = jnp.zeros_like(l_ref)
    acc_ref[...] = jnp.zeros_like(acc_ref)

    def step(j, masked):
        start = pl.multiple_of(j * tq, tq)
        kt = k_ref[pl.ds(start, tq), :]
        vt = v_ref[pl.ds(start, tq), :]
        for mp in range(2):
            s = _nt_dot(qm[mp], kt)
            if masked:
                rq = lax.broadcasted_iota(jnp.int32, s.shape, 0) // ATTN_CHUNK
                ck = lax.broadcasted_iota(jnp.int32, s.shape, 1) // ATTN_CHUNK
                s = jnp.where(ck <= rq, s, -jnp.inf)
            m_old = m_ref[mp]
            m_new = jnp.maximum(m_old, jnp.max(s, axis=-1, keepdims=True))
            alpha = jnp.exp(m_old - m_new)
            p = jnp.exp(s - m_new)
            l_ref[mp] = alpha * l_ref[mp] + jnp.sum(p, axis=-1, keepdims=True)
            acc_ref[mp] = alpha * acc_ref[mp] + _dot(p.astype(BF16), vt)
            m_ref[mp] = m_new

    def body(j, carry):
        step(j, False)
        return carry

    lax.fori_loop(0, i, body, 0)
    step(i, True)

    f32sum = lambda a, b: jnp.sum(a[...] * b[...], axis=-1, keepdims=True)
    lam = jnp.exp(f32sum(lq1_ref, lk1_ref)) - jnp.exp(f32sum(lq2_ref, lk2_ref)) + lambda_init
    o = acc_ref[0] / l_ref[0] - lam * (acc_ref[1] / l_ref[1])
    ms = jnp.mean(o * o, axis=-1, keepdims=True)
    o = o * lax.rsqrt(ms + NORM_EPS) * sg_ref[...] * (1.0 - lambda_init)
    o_ref[...] = o.astype(o_ref.dtype)


def _attn_call(q, k, v, lq1, lk1, lq2, lk2, subln_g, lambda_init, batch, seq, *, tq=256):
    t = q.shape[0]
    nq = seq // tq
    vec = lambda a: a.reshape(1, -1)
    small = lambda a: pl.BlockSpec(a.shape, lambda b, h, i: (0, 0))
    kv = pl.BlockSpec((seq, DIFF_V_DIM), lambda b, h, i: (b, h))
    args = [vec(lq1), vec(lk1), vec(lq2), vec(lk2), vec(subln_g)]
    return pl.pallas_call(
        functools.partial(_attn_kernel, tq=tq, lambda_init=lambda_init),
        grid=(batch, DIFF_HEADS, nq),
        in_specs=[pl.BlockSpec((tq, DIFF_V_DIM), lambda b, h, i: (b * nq + i, h)), kv, kv]
                 + [small(a) for a in args],
        out_specs=pl.BlockSpec((tq, DIFF_V_DIM), lambda b, h, i: (b * nq + i, h)),
        out_shape=jax.ShapeDtypeStruct((t, DIFF_HEADS * DIFF_V_DIM), BF16),
        scratch_shapes=[
            pltpu.VMEM((2, tq, 1), F32),
            pltpu.VMEM((2, tq, 1), F32),
            pltpu.VMEM((2, tq, DIFF_V_DIM), F32),
        ],
        compiler_params=_cparams(("arbitrary", "arbitrary", "arbitrary")),
        name="diff_attn",
    )(q, k, v, *args)


def _peer_route_kernel(q_ref, sk_ref, s1_ref, s2_ref, f_ref, e2_ref, tau_ref, a_ref, b_ref, cand_ref,
                       *, groups):
    neg = -jnp.inf
    k = PEER_TOPK

    def top_rows(w, dst_ref):
        for r in range(k):
            mx = jnp.max(w, axis=0, keepdims=True)
            dst_ref[r:r + 1, :] = mx
            w = jnp.where(w == mx, neg, w)

    for gidx in range(groups):
        rows = slice(gidx * LANES, (gidx + 1) * LANES)
        s1 = _nt_dot(sk_ref[0], q_ref[rows, 0:PEER_HALF], HIGHEST)
        s2 = _nt_dot(sk_ref[1], q_ref[rows, PEER_HALF:2 * PEER_HALF], HIGHEST)
        top_rows(s1, a_ref)
        top_rows(s2, b_ref)
        bvals = b_ref[...]
        for x in range(k):
            cand_ref[x * k:(x + 1) * k, :] = a_ref[x:x + 1, :] + bvals
        w = cand_ref[...]
        zsum = jnp.zeros((1, LANES), F32)
        c0 = None
        mx = None
        for r in range(k):
            mx = jnp.max(w, axis=0, keepdims=True)
            if r == 0:
                c0 = mx
            zsum = zsum + jnp.exp(mx - c0)
            w = jnp.where(w == mx, neg, w)
        s1_ref[0, gidx] = s1
        s2_ref[0, gidx] = s2
        f_ref[0, gidx] = jnp.exp(s1 - a_ref[0:1, :]) / zsum
        e2_ref[0, gidx] = jnp.exp(s2 - b_ref[0:1, :])
        tau_ref[0, gidx] = jnp.broadcast_to(mx, (8, LANES))


def _peer_route_call(q, subkeys, *, tt=256):
    t = q.shape[0]
    groups = tt // LANES
    ng = t // LANES
    big = jax.ShapeDtypeStruct((PEER_HEADS, ng, PEER_KEYS, LANES), F32)
    big_spec = pl.BlockSpec((1, groups, PEER_KEYS, LANES), lambda i, h: (h, i, 0, 0))
    return pl.pallas_call(
        functools.partial(_peer_route_kernel, groups=groups),
        grid=(t // tt, PEER_HEADS),
        in_specs=[
            pl.BlockSpec((tt, 2 * PEER_HALF), lambda i, h: (i, h)),
            pl.BlockSpec(subkeys.shape, lambda i, h: (0, 0, 0)),
        ],
        out_specs=[big_spec, big_spec, big_spec, big_spec,
                   pl.BlockSpec((1, groups, 8, LANES), lambda i, h: (h, i, 0, 0))],
        out_shape=[big, big, big, big, jax.ShapeDtypeStruct((PEER_HEADS, ng, 8, LANES), F32)],
        scratch_shapes=[
            pltpu.VMEM((PEER_TOPK, LANES), F32),
            pltpu.VMEM((PEER_TOPK, LANES), F32),
            pltpu.VMEM((PEER_TOPK * PEER_TOPK, LANES), F32),
        ],
        compiler_params=_cparams(("arbitrary", "arbitrary")),
        name="peer_route",
    )(q, subkeys)


def _gelu(x):
    return 0.5 * x * (1.0 + lax.erf(x * (1.0 / math.sqrt(2.0))))


def _peer_expert_kernel(x_ref, gate_ref, hb_ref, u_ref, vt_ref, s1_ref, f_ref, s2_ref, e2_ref, tau_ref,
                        o_ref, pre_ref, wa_ref, acc_ref, *, groups, rows_per_tile):
    e = pl.program_id(1)

    @pl.when(e == 0)
    def _():
        acc_ref[...] = jnp.zeros_like(acc_ref)

    pre_ref[...] = _nt_dot(u_ref[...], hb_ref[...])

    def row_body(il, carry):
        base = pl.multiple_of(il * PEER_KEYS, PEER_KEYS)
        for gidx in range(groups):
            cols = slice(gidx * LANES, (gidx + 1) * LANES)
            w = jnp.zeros((PEER_KEYS, LANES), F32)
            for h in range(PEER_HEADS):
                s1row = s1_ref[h, gidx, pl.ds(il, 1), :]
                frow = f_ref[h, gidx, pl.ds(il, 1), :]
                tau = tau_ref[h, gidx, 0:1, :]
                hit = (s2_ref[h, gidx] + s1row) >= tau
                w = w + jnp.where(hit, e2_ref[h, gidx], 0.0) * frow
            pre = pre_ref[pl.ds(base, PEER_KEYS), cols]
            wa_ref[pl.ds(base, PEER_KEYS), cols] = (w * _gelu(pre)).astype(BF16)
        return carry

    lax.fori_loop(0, rows_per_tile, row_body, 0)
    acc_ref[...] += _dot(vt_ref[...], wa_ref[...])

    @pl.when(e == pl.num_programs(1) - 1)
    def _():
        o_ref[...] = x_ref[...] + gate_ref[0] * acc_ref[...].T


def _peer_expert_call(x, gate, hb, u_b, vt_b, s1, f, s2, e2, tau, seq, *, tt=256, te=1024):
    t, d = x.shape
    ne = u_b.shape[0]
    groups = tt // LANES
    rows_per_tile = te // PEER_KEYS
    sel = pl.BlockSpec((PEER_HEADS, groups, rows_per_tile, LANES), lambda i, e: (0, i, e, 0))
    full = pl.BlockSpec((PEER_HEADS, groups, PEER_KEYS, LANES), lambda i, e: (0, i, 0, 0))
    return pl.pallas_call(
        functools.partial(_peer_expert_kernel, groups=groups, rows_per_tile=rows_per_tile),
        grid=(t // tt, ne // te),
        in_specs=[
            pl.BlockSpec((tt, d), lambda i, e: (i, 0)),
            pl.BlockSpec((1, 1, d), lambda i, e: ((i * tt) // seq, 0, 0)),
            pl.BlockSpec((tt, d), lambda i, e: (i, 0)),
            pl.BlockSpec((te, d), lambda i, e: (e, 0)),
            pl.BlockSpec((d, te), lambda i, e: (0, e)),
            sel, sel, full, full,
            pl.BlockSpec((PEER_HEADS, groups, 8, LANES), lambda i, e: (0, i, 0, 0)),
        ],
        out_specs=pl.BlockSpec((tt, d), lambda i, e: (i, 0)),
        out_shape=jax.ShapeDtypeStruct((t, d), F32),
        scratch_shapes=[
            pltpu.VMEM((te, tt), F32),
            pltpu.VMEM((te, tt), BF16),
            pltpu.VMEM((d, tt), F32),
        ],
        compiler_params=_cparams(("arbitrary", "arbitrary")),
        name="peer_experts",
    )(x, gate, hb, u_b, vt_b, s1, f, s2, e2, tau)


def _final_norm_kernel(x_ref, g_ref, o_ref):
    x = x_ref[...]
    ms = jnp.mean(x * x, axis=-1, keepdims=True)
    o_ref[...] = x * lax.rsqrt(ms + NORM_EPS) * g_ref[...]


def _final_norm_call(x, g, *, tm=512):
    t, d = x.shape
    return pl.pallas_call(
        _final_norm_kernel,
        grid=(t // tm,),
        in_specs=[pl.BlockSpec((tm, d), lambda i: (i, 0)), pl.BlockSpec((1, d), lambda i: (0, 0))],
        out_specs=pl.BlockSpec((tm, d), lambda i: (i, 0)),
        out_shape=jax.ShapeDtypeStruct((t, d), F32),
        compiler_params=_cparams(("arbitrary",)),
        name="final_norm",
    )(x, g.reshape(1, d))


def _pad_rows(a, rows):
    return jnp.pad(a, ((0, rows - a.shape[0]), (0, 0)))


def _pad_cols(a, cols):
    return jnp.pad(a, ((0, 0), (0, cols - a.shape[1])))


def _hybrid_mixer(x, g, shift, scale, batch, seq, w_in, conv_w, conv_b, conv_ln_g, conv_ln_b, mu, w0, w2,
                  a0, a2, g2, k_k, k_a, r_k, ln_g, ln_b):
    rw = RWKV_WIDTH
    c0 = 2 * CONV_WIDTH
    lora0 = c0 + 3 * rw
    l1, l2 = lora0 + DECAY_LORA, lora0 + DECAY_LORA + AAA_LORA
    w_conv = w_in[:, :c0].astype(BF16)
    w_rkv = w_in[:, c0:lora0].astype(BF16)
    w_lora = jnp.concatenate([
        _pad_cols(w_in[:, lora0:l1], LANES), _pad_cols(w_in[:, l1:l2], LANES),
        _pad_cols(w_in[:, l2:], 2 * LANES)], axis=1).astype(BF16)
    zc, zr, zl = _norm_matmul_call(x, g, shift, scale, [w_conv, w_rkv, w_lora], [F32, F32, F32], seq,
                                   name="hybrid_in_proj")
    ya = _conv_call(zc, conv_w, conv_b, conv_ln_g, conv_ln_b, batch, seq)
    mu_l = jnp.concatenate([
        _pad_cols(mu[None, lora0 - c0:l1 - c0], LANES), _pad_cols(mu[None, l1 - c0:l2 - c0], LANES),
        _pad_cols(mu[None, l2 - c0:], 2 * LANES)], axis=1)
    vec = lambda a: a.reshape(1, rw)
    head_id = jnp.arange(rw) // RWKV_HEAD
    params = dict(
        mu_r=mu[None, :3 * rw], mu_l=mu_l, w0=vec(w0), a0=vec(a0), k_k=vec(k_k), k_a=vec(k_a),
        r_k=vec(r_k), ln_g=vec(ln_g), ln_b=vec(ln_b),
        w2=_pad_rows(w2, LANES).astype(BF16), a2=_pad_rows(a2, LANES).astype(BF16),
        g2=_pad_rows(g2, 2 * LANES).astype(BF16),
        bd=(head_id[:, None] == head_id[None, :]).astype(F32))
    yb = _rwkv_call(zr, zl, params, batch, seq)
    return ya, yb


def kernel(x, c, positions, ada_w, ada_b, norm_mix_g, norm_ffn_g, hyb_w_in, conv_w, conv_b, conv_ln_g, conv_ln_b, rwkv_mu, rwkv_w0, rwkv_w2, rwkv_a0, rwkv_a2, rwkv_g2, rwkv_k_k, rwkv_k_a, rwkv_r_k, rwkv_ln_g, rwkv_ln_b, hyb_w_out, diff_w_qkv, diff_lq1, diff_lk1, diff_lq2, diff_lk2, diff_subln_g, diff_w_out, peer_w_q, peer_subkeys, peer_u, peer_v, final_g):
    batch, seq, d = x.shape
    depth = ada_w.shape[0]
    t = batch * seq
    xt = x.reshape(t, d)
    pos = positions.reshape(t, 1)
    mod = _ada_call(c, ada_w, ada_b).reshape(depth, batch, 6, 1, d)

    for l in range(depth):
        sh1, sc1, g1, sh2, sc2, g2 = (mod[l, :, j] for j in range(6))
        gmix = norm_mix_g[l].reshape(1, d)
        if l % 2 == 0:
            e = l // 2
            ya, yb = _hybrid_mixer(
                xt, gmix, sh1, sc1, batch, seq, hyb_w_in[e], conv_w[e], conv_b[e], conv_ln_g[e],
                conv_ln_b[e], rwkv_mu[e], rwkv_w0[e], rwkv_w2[e], rwkv_a0[e], rwkv_a2[e], rwkv_g2[e],
                rwkv_k_k[e], rwkv_k_a[e], rwkv_r_k[e], rwkv_ln_g[e], rwkv_ln_b[e])
            w_out = hyb_w_out[e].astype(BF16)
            xt = _out_proj_call(xt, g1, [ya.astype(BF16), yb.astype(BF16)],
                                [w_out[:CONV_WIDTH], w_out[CONV_WIDTH:]], seq)
        else:
            o = l // 2
            lambda_init = 0.8 - 0.6 * math.exp(-0.3 * l)
            wqkv = diff_w_qkv[o].astype(BF16)
            n = wqkv.shape[1] // 3
            q, k, v = _qkv_rope_call(xt, gmix, sh1, sc1, pos, wqkv[:, :n], wqkv[:, n:2 * n], wqkv[:, 2 * n:],
                                     seq)
            y = _attn_call(q, k, v, diff_lq1[o], diff_lk1[o], diff_lq2[o], diff_lk2[o], diff_subln_g[o],
                           lambda_init, batch, seq)
            xt = _out_proj_call(xt, g1, [y], [diff_w_out[o].astype(BF16)], seq)

        q, hb = _norm_matmul_call(xt, norm_ffn_g[l].reshape(1, d), sh2, sc2, [peer_w_q[l]], [F32], seq,
                                  exact=True, emit_h=True, name="peer_query")
        s1, s2, f, e2, tau = _peer_route_call(q, peer_subkeys[l])
        xt = _peer_expert_call(xt, g2, hb, peer_u[l].astype(BF16), peer_v[l].T.astype(BF16),
                               s1, f, s2, e2, tau, seq)

    return _final_norm_call(xt, final_g).reshape(batch, seq, d)
```

```python
import functools
import math

import jax
import jax.numpy as jnp
from jax import lax
from jax.experimental import pallas as pl
from jax.experimental.pallas import tpu as pltpu

F32 = jnp.float32
BF16 = jnp.bfloat16
HIGHEST = lax.Precision.HIGHEST

CONV_WIDTH = 512
CONV_KERNEL = 31
RWKV_WIDTH = 512
RWKV_HEAD = 64
RWKV_HEADS = 8
DECAY_LORA = 64
AAA_LORA = 64
GATE_LORA = 160
DIFF_HEADS = 8
DIFF_HEAD_DIM = 64
DIFF_V_DIM = 128
ROPE_DIM = 16
ROPE_THETA = 500000.0
ATTN_CHUNK = 64
PEER_HEADS = 8
PEER_KEYS = 128
PEER_HALF = 128
PEER_TOPK = 16
NORM_EPS = 1e-6
LN_EPS = 1e-5
RWKV_GN_EPS = 64e-5

LANES = 128
SCAN_CHUNK = 64
CONV_HALO = 32
PEER_TOKEN_TILE = 256
VMEM_LIMIT = 48 * 1024 * 1024


def _cparams(semantics):
    return pltpu.CompilerParams(dimension_semantics=semantics, vmem_limit_bytes=VMEM_LIMIT)


def _nt_dot(a, b, precision=None):
    return lax.dot_general(a, b, (((1,), (1,)), ((), ())), precision=precision,
                           preferred_element_type=F32)


def _dot(a, b, precision=None):
    return jnp.dot(a, b, precision=precision, preferred_element_type=F32)


def _sigmoid(x):
    return 1.0 / (1.0 + jnp.exp(-x))


def _ada_kernel(c_ref, w_ref, b_ref, o_ref):
    c = c_ref[...]
    cond = c * _sigmoid(c)
    o_ref[0] = _dot(cond, w_ref[0], HIGHEST) + b_ref[0]


def _ada_call(c, ada_w, ada_b):
    depth, d, n = ada_w.shape
    b = c.shape[0]
    tn = 1536
    return pl.pallas_call(
        _ada_kernel,
        grid=(depth, n // tn),
        in_specs=[
            pl.BlockSpec((b, d), lambda l, j: (0, 0)),
            pl.BlockSpec((1, d, tn), lambda l, j: (l, 0, j)),
            pl.BlockSpec((1, 1, tn), lambda l, j: (l, 0, j)),
        ],
        out_specs=pl.BlockSpec((1, b, tn), lambda l, j: (l, 0, j)),
        out_shape=jax.ShapeDtypeStruct((depth, b, n), F32),
        compiler_params=_cparams(("arbitrary", "arbitrary")),
        name="ada_mod",
    )(c, ada_w, ada_b.reshape(depth, 1, n))


def _modulate(x, g, shift, scale):
    ms = jnp.mean(x * x, axis=-1, keepdims=True)
    y = x * lax.rsqrt(ms + NORM_EPS)
    return (y * g) * (1.0 + scale) + shift


def _norm_matmul_kernel(*refs, n_w, exact, emit_h):
    x_ref, g_ref, sh_ref, sc_ref = refs[:4]
    w_refs = refs[4:4 + n_w]
    o_refs = refs[4 + n_w:]
    h = _modulate(x_ref[...], g_ref[...], sh_ref[0], sc_ref[0])
    hb = h.astype(BF16)
    for w_ref, o_ref in zip(w_refs, o_refs[:n_w]):
        if exact:
            o_ref[...] = _dot(h, w_ref[...], HIGHEST).astype(o_ref.dtype)
        else:
            o_ref[...] = _dot(hb, w_ref[...]).astype(o_ref.dtype)
    if emit_h:
        o_refs[n_w][...] = hb


def _norm_matmul_call(x, g, shift, scale, weights, out_dtypes, seq, *, exact=False, emit_h=False,
                      tm=512, name="norm_matmul"):
    t, d = x.shape
    in_specs = [
        pl.BlockSpec((tm, d), lambda i: (i, 0)),
        pl.BlockSpec((1, d), lambda i: (0, 0)),
        pl.BlockSpec((1, 1, d), lambda i: ((i * tm) // seq, 0, 0)),
        pl.BlockSpec((1, 1, d), lambda i: ((i * tm) // seq, 0, 0)),
    ]
    out_specs, out_shape = [], []
    for w, dt in zip(weights, out_dtypes):
        n = w.shape[1]
        in_specs.append(pl.BlockSpec((d, n), lambda i: (0, 0)))
        out_specs.append(pl.BlockSpec((tm, n), lambda i: (i, 0)))
        out_shape.append(jax.ShapeDtypeStruct((t, n), dt))
    if emit_h:
        out_specs.append(pl.BlockSpec((tm, d), lambda i: (i, 0)))
        out_shape.append(jax.ShapeDtypeStruct((t, d), BF16))
    return pl.pallas_call(
        functools.partial(_norm_matmul_kernel, n_w=len(weights), exact=exact, emit_h=emit_h),
        grid=(t // tm,),
        in_specs=in_specs,
        out_specs=out_specs,
        out_shape=out_shape,
        compiler_params=_cparams(("arbitrary",)),
        name=name,
    )(x, g, shift, scale, *weights)


def _out_proj_kernel(*refs, n_y):
    x_ref, gate_ref = refs[:2]
    y_refs = refs[2:2 + n_y]
    w_refs = refs[2 + n_y:2 + 2 * n_y]
    o_ref = refs[2 + 2 * n_y]
    acc = _dot(y_refs[0][...], w_refs[0][...])
    for y_ref, w_ref in zip(y_refs[1:], w_refs[1:]):
        acc = acc + _dot(y_ref[...], w_ref[...])
    o_ref[...] = x_ref[...] + gate_ref[0] * acc


def _out_proj_call(x, gate, ys, ws, seq, *, tm=512):
    t, d = x.shape
    in_specs = [
        pl.BlockSpec((tm, d), lambda i: (i, 0)),
        pl.BlockSpec((1, 1, d), lambda i: ((i * tm) // seq, 0, 0)),
    ]
    for y in ys:
        in_specs.append(pl.BlockSpec((tm, y.shape[1]), lambda i: (i, 0)))
    for w in ws:
        in_specs.append(pl.BlockSpec(w.shape, lambda i: (0, 0)))
    return pl.pallas_call(
        functools.partial(_out_proj_kernel, n_y=len(ys)),
        grid=(t // tm,),
        in_specs=in_specs,
        out_specs=pl.BlockSpec((tm, d), lambda i: (i, 0)),
        out_shape=jax.ShapeDtypeStruct((t, d), F32),
        compiler_params=_cparams(("arbitrary",)),
        name="out_proj",
    )(x, gate, *ys, *ws)


def _conv_kernel(z_ref, w_ref, b_ref, g_ref, beta_ref, o_ref, ext_ref, *, ts):
    width = CONV_WIDTH

    @pl.when(pl.program_id(1) == 0)
    def _():
        ext_ref[0:CONV_HALO, :] = jnp.zeros((CONV_HALO, width), F32)

    z = z_ref[...]
    u = z[:, :width] * _sigmoid(z[:, width:])
    ext_ref[CONV_HALO:CONV_HALO + ts, :] = u
    base = CONV_HALO - (CONV_KERNEL - 1)
    rows = 64
    for cb in range(width // LANES):
        cs = slice(cb * LANES, (cb + 1) * LANES)
        for rb in range(ts // rows):
            acc = jnp.zeros((rows, LANES), F32)
            for j in range(CONV_KERNEL):
                start = rb * rows + base + j
                acc = acc + w_ref[j:j + 1, cs] * ext_ref[start:start + rows, cs]
            o_ref[rb * rows:(rb + 1) * rows, cs] = acc
    conv = o_ref[...] + b_ref[...]
    mu = jnp.mean(conv, axis=-1, keepdims=True)
    dlt = conv - mu
    var = jnp.mean(dlt * dlt, axis=-1, keepdims=True)
    y = dlt * lax.rsqrt(var + LN_EPS) * g_ref[...] + beta_ref[...]
    o_ref[...] = y * _sigmoid(y)
    ext_ref[0:CONV_HALO, :] = ext_ref[ts:ts + CONV_HALO, :]


def _conv_call(z, conv_w, conv_b, ln_g, ln_b, batch, seq, *, ts=256):
    t = z.shape[0]
    width = CONV_WIDTH
    nts = seq // ts
    vec = lambda a: a.reshape(1, width)
    return pl.pallas_call(
        functools.partial(_conv_kernel, ts=ts),
        grid=(batch, nts),
        in_specs=[
            pl.BlockSpec((ts, 2 * width), lambda b, i: (b * nts + i, 0)),
            pl.BlockSpec((CONV_KERNEL, width), lambda b, i: (0, 0)),
            pl.BlockSpec((1, width), lambda b, i: (0, 0)),
            pl.BlockSpec((1, width), lambda b, i: (0, 0)),
            pl.BlockSpec((1, width), lambda b, i: (0, 0)),
        ],
        out_specs=pl.BlockSpec((ts, width), lambda b, i: (b * nts + i, 0)),
        out_shape=jax.ShapeDtypeStruct((t, width), F32),
        scratch_shapes=[pltpu.VMEM((ts + CONV_HALO, width), F32)],
        compiler_params=_cparams(("arbitrary", "arbitrary")),
        name="conformer_conv",
    )(z, conv_w, vec(conv_b), vec(ln_g), vec(ln_b))


def _softplus(x):
    return jnp.maximum(x, 0.0) + jnp.log(1.0 + jnp.exp(-jnp.abs(x)))


def _unit_lower_inverse(a, row, col):
    n = a.shape[0]
    eye = (row == col).astype(F32)
    blk = lambda m: (row // m) == (col // m)
    bdot = lambda p, q: _dot(p.astype(BF16), q.astype(BF16))
    n1 = jnp.where(blk(8), a, 0.0)
    t = eye + n1
    n2 = bdot(n1, n1)
    t = t + bdot(t, n2)
    n4 = bdot(n2, n2)
    t = t + bdot(t, n4)
    m = 8
    while m < n:
        e = jnp.where(blk(2 * m) & jnp.logical_not(blk(m)), a, 0.0)
        t = t + bdot(t, bdot(e, t))
        m *= 2
    return t


def _rwkv_kernel(zr_ref, zl_ref, mur_ref, mul_ref, w0_ref, a0_ref, kk_ref, ka_ref, rk_ref,
                 lng_ref, lnb_ref, w2_ref, a2_ref, g2_ref, bd_ref, o_ref,
                 extr_ref, extl_ref, state_ref, y_ref):
    c = SCAN_CHUNK
    rw = RWKV_WIDTH
    n = RWKV_HEAD

    @pl.when(pl.program_id(1) == 0)
    def _():
        extr_ref[0:8, :] = jnp.zeros((8, 3 * rw), F32)
        extl_ref[0:8, :] = jnp.zeros((8, rw), F32)
        state_ref[...] = jnp.zeros_like(state_ref)

    zr = zr_ref[...]
    zl = zl_ref[...]
    extr_ref[8:8 + c, :] = zr
    extl_ref[8:8 + c, :] = zl
    zr = zr + (extr_ref[7:7 + c, :] - zr) * mur_ref[...]
    zl = zl + (extl_ref[7:7 + c, :] - zl) * mul_ref[...]
    extr_ref[0:8, :] = extr_ref[c:c + 8, :]
    extl_ref[0:8, :] = extl_ref[c:c + 8, :]

    r = zr[:, 0:rw]
    k = zr[:, rw:2 * rw]
    v = zr[:, 2 * rw:3 * rw]
    wd = zl[:, 0:LANES]
    ad = zl[:, LANES:2 * LANES]
    gd = zl[:, 2 * LANES:4 * LANES]
    bd = bd_ref[...]

    w_raw = -_softplus(-(w0_ref[...] + _dot(jnp.tanh(wd).astype(BF16), w2_ref[...]))) - 0.5
    lw = -jnp.exp(w_raw)
    alpha = _sigmoid(a0_ref[...] + _dot(ad.astype(BF16), a2_ref[...]))
    gate = _dot(_sigmoid(gd).astype(BF16), g2_ref[...])
    kkf = k * kk_ref[...]
    ss = _dot(kkf * kkf, bd, HIGHEST)
    kk = kkf / jnp.maximum(jnp.sqrt(ss), 1e-12)
    kp = k * (1.0 + (alpha - 1.0) * ka_ref[...])
    bonus = _dot(r * kp * rk_ref[...], bd, HIGHEST) * v
    av = -kk
    bv = kk * alpha

    row = lax.broadcasted_iota(jnp.int32, (c, c), 0)
    col = lax.broadcasted_iota(jnp.int32, (c, c), 1)
    incl = (col <= row)
    strict = (col < row)
    cum = _dot(incl.astype(F32), lw, HIGHEST)
    tot = cum[c - 1:c, :]
    g_in = jnp.exp(cum)
    g_inp = jnp.exp(cum - lw)
    g_out = jnp.exp(-cum)
    g_end = jnp.exp(tot - cum)
    g_tot = jnp.exp(tot)
    a_in = av * g_inp
    r_in = r * g_in
    b_out = bv * g_out
    k_out = kp * g_out
    b_end = bv * g_end
    k_end = kp * g_end

    for h in range(RWKV_HEADS):
        hs = slice(h * n, (h + 1) * n)
        p = jnp.concatenate([a_in[:, hs], r_in[:, hs]], axis=0).astype(BF16)
        q = jnp.concatenate([b_out[:, hs], k_out[:, hs]], axis=0).astype(BF16)
        vh = v[:, hs].astype(BF16)
        m = _nt_dot(p, q)
        a_ab = jnp.where(strict, m[:c, :c], 0.0)
        a_ak = jnp.where(strict, m[:c, c:], 0.0)
        a_rb = jnp.where(incl, m[c:, :c], 0.0)
        a_rk = jnp.where(incl, m[c:, c:], 0.0)
        st = state_ref[h]
        ph = _nt_dot(p, st.astype(BF16))
        tinv = _unit_lower_inverse(a_ab, row, col)
        rhs = ph[:c] + _dot(a_ak.astype(BF16), vh)
        u = _dot(tinv.astype(BF16), rhs.astype(BF16))
        ub = u.astype(BF16)
        yh = ph[c:] + _dot(a_rb.astype(BF16), ub) + _dot(a_rk.astype(BF16), vh)
        y_ref[:, hs] = yh
        uv_t = jnp.concatenate([u.T, v[:, hs].T], axis=1).astype(BF16)
        bk = jnp.concatenate([b_end[:, hs], k_end[:, hs]], axis=0).astype(BF16)
        state_ref[h] = st * g_tot[:, hs] + _dot(uv_t, bk)

    y = y_ref[...]
    inv_n = 1.0 / n
    mu_y = _dot(y, bd, HIGHEST) * inv_n
    dy = y - mu_y
    var_y = _dot(dy * dy, bd, HIGHEST) * inv_n
    yn = dy * lax.rsqrt(var_y + RWKV_GN_EPS)
    o_ref[...] = (yn * lng_ref[...] + lnb_ref[...] + bonus) * gate


def _rwkv_call(zr, zl, p, batch, seq):
    t = zr.shape[0]
    c = SCAN_CHUNK
    rw = RWKV_WIDTH
    nc = seq // c
    const = lambda shape: pl.BlockSpec(shape, lambda b, i: (0,) * len(shape))
    vec = const((1, rw))
    return pl.pallas_call(
        _rwkv_kernel,
        grid=(batch, nc),
        in_specs=[
            pl.BlockSpec((c, 3 * rw), lambda b, i: (b * nc + i, 0)),
            pl.BlockSpec((c, rw), lambda b, i: (b * nc + i, 0)),
            const((1, 3 * rw)), vec,
            vec, vec, vec, vec, vec, vec, vec,
            const((LANES, rw)), const((LANES, rw)), const((2 * LANES, rw)),
            const((rw, rw)),
        ],
        out_specs=pl.BlockSpec((c, rw), lambda b, i: (b * nc + i, 0)),
        out_shape=jax.ShapeDtypeStruct((t, rw), F32),
        scratch_shapes=[
            pltpu.VMEM((c + 8, 3 * rw), F32),
            pltpu.VMEM((c + 8, rw), F32),
            pltpu.VMEM((RWKV_HEADS, RWKV_HEAD, RWKV_HEAD), F32),
            pltpu.VMEM((c, rw), F32),
        ],
        compiler_params=_cparams(("arbitrary", "arbitrary")),
        name="rwkv7_mix",
    )(zr, zl, p["mu_r"], p["mu_l"], p["w0"], p["a0"], p["k_k"], p["k_a"], p["r_k"],
      p["ln_g"], p["ln_b"], p["w2"], p["a2"], p["g2"], p["bd"])


def _qkv_rope_kernel(x_ref, g_ref, sh_ref, sc_ref, pos_ref, inv_ref, sgn_ref, wq_ref, wk_ref, wv_ref,
                     q_ref, k_ref, v_ref):
    h = _modulate(x_ref[...], g_ref[...], sh_ref[0], sc_ref[0]).astype(BF16)
    ang = pos_ref[...].astype(F32) * inv_ref[...]
    cosf = jnp.cos(ang)
    sinf = jnp.sin(ang) * sgn_ref[...]
    lane = lax.broadcasted_iota(jnp.int32, ang.shape, 1)
    low = (lane % DIFF_HEAD_DIM) < (ROPE_DIM // 2)
    half = ROPE_DIM // 2

    def rope(w_ref, o_ref, scale):
        z = _dot(h, w_ref[...])
        for cb in range(z.shape[1] // LANES):
            zc = z[:, cb * LANES:(cb + 1) * LANES]
            partner = jnp.where(low, pltpu.roll(zc, LANES - half, axis=1), pltpu.roll(zc, half, axis=1))
            o_ref[:, cb * LANES:(cb + 1) * LANES] = ((zc * cosf + partner * sinf) * scale).astype(o_ref.dtype)

    rope(wq_ref, q_ref, DIFF_HEAD_DIM ** -0.5)
    rope(wk_ref, k_ref, 1.0)
    v_ref[...] = _dot(h, wv_ref[...]).astype(v_ref.dtype)


def _qkv_rope_call(x, g, shift, scale, pos, wq, wk, wv, seq, *, tm=512):
    t, d = x.shape
    half = ROPE_DIM // 2
    inv = 1.0 / (ROPE_THETA ** (jnp.arange(0, ROPE_DIM, 2, dtype=F32) / ROPE_DIM))
    dl = jnp.arange(LANES) % DIFF_HEAD_DIM
    inv_pat = jnp.where(dl < ROPE_DIM, inv[dl % half], 0.0).reshape(1, LANES).astype(F32)
    sgn_pat = jnp.where(dl < half, -1.0, 1.0).reshape(1, LANES).astype(F32)
    row = lambda n: pl.BlockSpec((tm, n), lambda i: (i, 0))
    full = lambda a: pl.BlockSpec(a.shape, lambda i: (0, 0))
    mod = pl.BlockSpec((1, 1, d), lambda i: ((i * tm) // seq, 0, 0))
    n = wq.shape[1]
    return pl.pallas_call(
        _qkv_rope_kernel,
        grid=(t // tm,),
        in_specs=[row(d), full(g), mod, mod, row(1), full(inv_pat), full(sgn_pat),
                  full(wq), full(wk), full(wv)],
        out_specs=[row(n), row(n), row(n)],
        out_shape=[jax.ShapeDtypeStruct((t, n), BF16)] * 3,
        compiler_params=_cparams(("arbitrary",)),
        name="qkv_rope",
    )(x, g, shift, scale, pos, inv_pat, sgn_pat, wq, wk, wv)


def _attn_kernel(q_ref, k_ref, v_ref, lq1_ref, lk1_ref, lq2_ref, lk2_ref, sg_ref, o_ref,
                 m_ref, l_ref, acc_ref, *, tq, lambda_init):
    i = pl.program_id(2)
    q = q_ref[...]
    lane = lax.broadcasted_iota(jnp.int32, q.shape, 1)
    zero = jnp.zeros_like(q)
    qm = (jnp.where(lane < DIFF_HEAD_DIM, q, zero), jnp.where(lane >= DIFF_HEAD_DIM, q, zero))

    m_ref[...] = jnp.full_like(m_ref, -jnp.inf)
    l_ref[...] = jnp.zeros_like(l_ref)
    acc_ref[...] = jnp.zeros_like(acc_ref)

    def step(j, masked):
        start = pl.multiple_of(j * tq, tq)
        kt = k_ref[pl.ds(start, tq), :]
        vt = v_ref[pl.ds(start, tq), :]
        for mp in range(2):
            s = _nt_dot(qm[mp], kt)
            if masked:
                rq = lax.broadcasted_iota(jnp.int32, s.shape, 0) // ATTN_CHUNK
                ck = lax.broadcasted_iota(jnp.int32, s.shape, 1) // ATTN_CHUNK
                s = jnp.where(ck <= rq, s, -jnp.inf)
            m_old = m_ref[mp]
            m_new = jnp.maximum(m_old, jnp.max(s, axis=-1, keepdims=True))
            alpha = jnp.exp(m_old - m_new)
            p = jnp.exp(s - m_new)
            l_ref[mp] = alpha * l_ref[mp] + jnp.sum(p, axis=-1, keepdims=True)
            acc_ref[mp] = alpha * acc_ref[mp] + _dot(p.astype(BF16), vt)
            m_ref[mp] = m_new

    def body(j, carry):
        step(j, False)
        return carry

    lax.fori_loop(0, i, body, 0)
    step(i, True)

    f32sum = lambda a, b: jnp.sum(a[...] * b[...], axis=-1, keepdims=True)
    lam = jnp.exp(f32sum(lq1_ref, lk1_ref)) - jnp.exp(f32sum(lq2_ref, lk2_ref)) + lambda_init
    o = acc_ref[0] / l_ref[0] - lam * (acc_ref[1] / l_ref[1])
    ms = jnp.mean(o * o, axis=-1, keepdims=True)
    o = o * lax.rsqrt(ms + NORM_EPS) * sg_ref[...] * (1.0 - lambda_init)
    o_ref[...] = o.astype(o_ref.dtype)


def _attn_call(q, k, v, lq1, lk1, lq2, lk2, subln_g, lambda_init, batch, seq, *, tq=256):
    t = q.shape[0]
    nq = seq // tq
    vec = lambda a: a.reshape(1, -1)
    small = lambda a: pl.BlockSpec(a.shape, lambda b, h, i: (0, 0))
    kv = pl.BlockSpec((seq, DIFF_V_DIM), lambda b, h, i: (b, h))
    args = [vec(lq1), vec(lk1), vec(lq2), vec(lk2), vec(subln_g)]
    return pl.pallas_call(
        functools.partial(_attn_kernel, tq=tq, lambda_init=lambda_init),
        grid=(batch, DIFF_HEADS, nq),
        in_specs=[pl.BlockSpec((tq, DIFF_V_DIM), lambda b, h, i: (b * nq + i, h)), kv, kv]
                 + [small(a) for a in args],
        out_specs=pl.BlockSpec((tq, DIFF_V_DIM), lambda b, h, i: (b * nq + i, h)),
        out_shape=jax.ShapeDtypeStruct((t, DIFF_HEADS * DIFF_V_DIM), BF16),
        scratch_shapes=[
            pltpu.VMEM((2, tq, 1), F32),
            pltpu.VMEM((2, tq, 1), F32),
            pltpu.VMEM((2, tq, DIFF_V_DIM), F32),
        ],
        compiler_params=_cparams(("arbitrary", "arbitrary", "arbitrary")),
        name="diff_attn",
    )(q, k, v, *args)


def _peer_route_kernel(q_ref, sk_ref, r2_ref, e2_ref, n1_ref, f_ref, a_ref, b_ref, cand_ref, *, tt):
    neg = -jnp.inf
    k = PEER_TOPK
    kf = float(k)
    key_id = lax.broadcasted_iota(jnp.int32, (PEER_KEYS, LANES), 0).astype(F32)
    cand_id = lax.broadcasted_iota(jnp.int32, (k * k, LANES), 0).astype(F32)

    def extract(w, ids, dst_ref, by_index, want_rank):
        rank = jnp.full(w.shape, kf, F32) if want_rank else None
        tops = []
        for r in range(k):
            mx = jnp.max(w, axis=0, keepdims=True)
            tops.append(mx)
            if dst_ref is not None:
                dst_ref[r:r + 1, :] = mx
            hit = w == mx
            if by_index:
                first = jnp.min(jnp.where(hit, ids, float(w.shape[0])), axis=0, keepdims=True)
                hit = ids == first
            if want_rank:
                rank = jnp.where(hit, float(r), rank)
            w = jnp.where(hit, neg, w)
        return rank, tops, w

    def candidates():
        bvals = b_ref[...]
        for x in range(k):
            cand_ref[x * k:(x + 1) * k, :] = a_ref[x:x + 1, :] + bvals
        return cand_ref[...]

    def twin_bf16(v):
        bits = lax.bitcast_convert_type(v.astype(BF16).astype(F32), jnp.uint32)
        return bits | (bits >> 16)

    def emit(cols, s1, s2, rank2, n1, a0, b0, c_tops):
        zsum = jnp.zeros((1, LANES), F32)
        for c in c_tops:
            zsum = zsum + jnp.exp(c - c_tops[0])
        r2_ref[0, 0, :, cols] = pltpu.bitcast(rank2.astype(BF16), jnp.uint32)
        e2_ref[0, 0, :, cols] = pltpu.bitcast(jnp.exp(s2 - b0).astype(BF16), jnp.uint32)
        n1_ref[0, 0, :, cols] = twin_bf16(n1)
        f_ref[0, 0, :, cols] = twin_bf16(jnp.exp(s1 - a0) / zsum)

    count = lambda m: jnp.sum(jnp.where(m, 1.0, 0.0), axis=0, keepdims=True)

    for gidx in range(tt // LANES):
        cols = slice(gidx * LANES, (gidx + 1) * LANES)
        s1 = _nt_dot(sk_ref[0], q_ref[cols, 0:PEER_HALF], HIGHEST)
        s2 = _nt_dot(sk_ref[1], q_ref[cols, PEER_HALF:2 * PEER_HALF], HIGHEST)

        _, a_tops, w1 = extract(s1, key_id, a_ref, False, False)
        rank2, b_tops, _ = extract(s2, key_id, b_ref, False, True)
        _, c_tops, _ = extract(candidates(), cand_id, None, False, False)
        tau = c_tops[k - 1]
        n1 = jnp.zeros_like(s1)
        for y in range(k):
            n1 = n1 + jnp.where((s1 + b_tops[y]) >= tau, 1.0, 0.0)
        emit(cols, s1, s2, rank2, n1, a_tops[0], b_tops[0], c_tops)
        tied = ((count(w1 == neg) != kf) | (count(rank2 < kf) != kf)
                | (jnp.sum(n1, axis=0, keepdims=True) != kf))

        @pl.when(jnp.max(jnp.where(tied, 1.0, 0.0)) > 0.0)
        def _():
            rank1, a_x, _ = extract(s1, key_id, a_ref, True, True)
            rank2x, b_x, _ = extract(s2, key_id, b_ref, True, True)
            rank3, c_x, _ = extract(candidates(), cand_id, None, True, True)
            win = rank3 < kf
            n1x = jnp.zeros_like(s1)
            for x in range(k):
                n_x = count(win[x * k:(x + 1) * k])
                n1x = n1x + jnp.where(rank1 == float(x), n_x, 0.0)
            emit(cols, s1, s2, rank2x, n1x, a_x[0], b_x[0], c_x)


def _peer_route_call(q, subkeys, *, tt):
    t = q.shape[0]
    nt = t // tt
    spec = pl.BlockSpec((1, 1, PEER_KEYS, tt), lambda i, h: (h, i, 0, 0))
    pair_spec = pl.BlockSpec((1, 1, PEER_KEYS // 2, tt), lambda i, h: (h, i, 0, 0))
    tab = lambda rows: jax.ShapeDtypeStruct((PEER_HEADS, nt, rows, tt), jnp.uint32)
    return pl.pallas_call(
        functools.partial(_peer_route_kernel, tt=tt),
        grid=(nt, PEER_HEADS),
        in_specs=[
            pl.BlockSpec((tt, 2 * PEER_HALF), lambda i, h: (i, h)),
            pl.BlockSpec(subkeys.shape, lambda i, h: (0, 0, 0)),
        ],
        out_specs=[pair_spec, pair_spec, spec, spec],
        out_shape=[tab(PEER_KEYS // 2), tab(PEER_KEYS // 2), tab(PEER_KEYS), tab(PEER_KEYS)],
        scratch_shapes=[
            pltpu.VMEM((PEER_TOPK, LANES), F32),
            pltpu.VMEM((PEER_TOPK, LANES), F32),
            pltpu.VMEM((PEER_TOPK * PEER_TOPK, LANES), F32),
        ],
        compiler_params=_cparams(("arbitrary", "arbitrary")),
        name="peer_route",
    )(q, subkeys)


def _gelu(x):
    return 0.5 * x * (1.0 + lax.erf(x * (1.0 / math.sqrt(2.0))))


def _peer_expert_kernel(x_ref, gate_ref, hb_ref, u_ref, vt_ref, n1_ref, f_ref, r2_ref, e2_ref,
                        o_ref, pre_a, pre_b, wa_ref, acc_ref, *, rows_per_tile):
    s = pl.program_id(1)
    te, tt = pre_a.shape
    d = acc_ref.shape[0]
    blk = 16

    @pl.when(s == 0)
    def _():
        acc_ref[...] = jnp.zeros_like(acc_ref)
        pre_b[...] = jnp.zeros_like(pre_b)

    def step(pre_new, pre_old):
        hb = hb_ref[...]
        for m in range(2):
            rows = slice(m * te // 2, (m + 1) * te // 2)
            pre_new[rows, :] = _nt_dot(u_ref[rows, :], hb)
        zero = jnp.zeros((blk, LANES), BF16)
        as_bf16 = lambda words: pltpu.bitcast(words, BF16)
        for il in range(rows_per_tile):
            for lt in range(tt // LANES):
                lanes = slice(lt * LANES, (lt + 1) * LANES)
                row = lambda ref, h: as_bf16(jnp.broadcast_to(ref[h, 0, il:il + 1, lanes], (blk // 2, LANES)))
                n1b = [row(n1_ref, h) for h in range(PEER_HEADS)]
                fb = [row(f_ref, h) for h in range(PEER_HEADS)]
                for jb in range(PEER_KEYS // blk):
                    words = slice(jb * blk // 2, (jb + 1) * blk // 2)
                    w = zero
                    for h in range(PEER_HEADS):
                        hit = as_bf16(r2_ref[h, 0, words, lanes]) < n1b[h]
                        w = w + jnp.where(hit, as_bf16(e2_ref[h, 0, words, lanes]), zero) * fb[h]
                    rows = slice(il * PEER_KEYS + jb * blk, il * PEER_KEYS + (jb + 1) * blk)
                    wa_ref[rows, lanes] = w * _gelu(pre_old[rows, lanes]).astype(BF16)
        wa = wa_ref[...]
        for m in range(2):
            rows = slice(m * d // 2, (m + 1) * d // 2)
            acc_ref[rows, :] += _dot(vt_ref[rows, :], wa)

    @pl.when(s % 2 == 0)
    def _():
        step(pre_a, pre_b)

    @pl.when(s % 2 == 1)
    def _():
        step(pre_b, pre_a)

    @pl.when(s == pl.num_programs(1) - 1)
    def _():
        o_ref[...] = x_ref[...] + gate_ref[0] * acc_ref[...].T


def _peer_expert_call(x, gate, hb, u_b, vt_b, n1, f, r2, e2, seq, *, tt, te=1024):
    t, d = x.shape
    ne = u_b.shape[0] // te
    rows_per_tile = te // PEER_KEYS
    prev = lambda s: jnp.maximum(s - 1, 0)
    sel = pl.BlockSpec((PEER_HEADS, 1, rows_per_tile, tt), lambda i, s: (0, i, prev(s), 0))
    full = pl.BlockSpec((PEER_HEADS, 1, PEER_KEYS // 2, tt), lambda i, s: (0, i, 0, 0))
    return pl.pallas_call(
        functools.partial(_peer_expert_kernel, rows_per_tile=rows_per_tile),
        grid=(t // tt, ne + 1),
        in_specs=[
            pl.BlockSpec((tt, d), lambda i, s: (i, 0)),
            pl.BlockSpec((1, 1, d), lambda i, s: ((i * tt) // seq, 0, 0)),
            pl.BlockSpec((tt, d), lambda i, s: (i, 0)),
            pl.BlockSpec((te, d), lambda i, s: (jnp.minimum(s, ne - 1), 0)),
            pl.BlockSpec((d, te), lambda i, s: (0, prev(s))),
            sel, sel, full, full,
        ],
        out_specs=pl.BlockSpec((tt, d), lambda i, s: (i, 0)),
        out_shape=jax.ShapeDtypeStruct((t, d), F32),
        scratch_shapes=[
            pltpu.VMEM((te, tt), F32),
            pltpu.VMEM((te, tt), F32),
            pltpu.VMEM((te, tt), BF16),
            pltpu.VMEM((d, tt), F32),
        ],
        compiler_params=_cparams(("arbitrary", "arbitrary")),
        name="peer_experts",
    )(x, gate, hb, u_b, vt_b, n1, f, r2, e2)


def _final_norm_kernel(x_ref, g_ref, o_ref):
    x = x_ref[...]
    ms = jnp.mean(x * x, axis=-1, keepdims=True)
    o_ref[...] = x * lax.rsqrt(ms + NORM_EPS) * g_ref[...]


def _final_norm_call(x, g, *, tm=512):
    t, d = x.shape
    return pl.pallas_call(
        _final_norm_kernel,
        grid=(t // tm,),
        in_specs=[pl.BlockSpec((tm, d), lambda i: (i, 0)), pl.BlockSpec((1, d), lambda i: (0, 0))],
        out_specs=pl.BlockSpec((tm, d), lambda i: (i, 0)),
        out_shape=jax.ShapeDtypeStruct((t, d), F32),
        compiler_params=_cparams(("arbitrary",)),
        name="final_norm",
    )(x, g.reshape(1, d))


def _pad_rows(a, rows):
    return jnp.pad(a, ((0, rows - a.shape[0]), (0, 0)))


def _pad_cols(a, cols):
    return jnp.pad(a, ((0, 0), (0, cols - a.shape[1])))


def _hybrid_mixer(x, g, shift, scale, batch, seq, w_in, conv_w, conv_b, conv_ln_g, conv_ln_b, mu, w0, w2,
                  a0, a2, g2, k_k, k_a, r_k, ln_g, ln_b):
    rw = RWKV_WIDTH
    c0 = 2 * CONV_WIDTH
    lora0 = c0 + 3 * rw
    l1, l2 = lora0 + DECAY_LORA, lora0 + DECAY_LORA + AAA_LORA
    w_conv = w_in[:, :c0].astype(BF16)
    w_rkv = w_in[:, c0:lora0].astype(BF16)
    w_lora = jnp.concatenate([
        _pad_cols(w_in[:, lora0:l1], LANES), _pad_cols(w_in[:, l1:l2], LANES),
        _pad_cols(w_in[:, l2:], 2 * LANES)], axis=1).astype(BF16)
    zc, zr, zl = _norm_matmul_call(x, g, shift, scale, [w_conv, w_rkv, w_lora], [F32, F32, F32], seq,
                                   name="hybrid_in_proj")
    ya = _conv_call(zc, conv_w, conv_b, conv_ln_g, conv_ln_b, batch, seq)
    mu_l = jnp.concatenate([
        _pad_cols(mu[None, lora0 - c0:l1 - c0], LANES), _pad_cols(mu[None, l1 - c0:l2 - c0], LANES),
        _pad_cols(mu[None, l2 - c0:], 2 * LANES)], axis=1)
    vec = lambda a: a.reshape(1, rw)
    head_id = jnp.arange(rw) // RWKV_HEAD
    params = dict(
        mu_r=mu[None, :3 * rw], mu_l=mu_l, w0=vec(w0), a0=vec(a0), k_k=vec(k_k), k_a=vec(k_a),
        r_k=vec(r_k), ln_g=vec(ln_g), ln_b=vec(ln_b),
        w2=_pad_rows(w2, LANES).astype(BF16), a2=_pad_rows(a2, LANES).astype(BF16),
        g2=_pad_rows(g2, 2 * LANES).astype(BF16),
        bd=(head_id[:, None] == head_id[None, :]).astype(F32))
    yb = _rwkv_call(zr, zl, params, batch, seq)
    return ya, yb


def kernel(x, c, positions, ada_w, ada_b, norm_mix_g, norm_ffn_g, hyb_w_in, conv_w, conv_b, conv_ln_g, conv_ln_b, rwkv_mu, rwkv_w0, rwkv_w2, rwkv_a0, rwkv_a2, rwkv_g2, rwkv_k_k, rwkv_k_a, rwkv_r_k, rwkv_ln_g, rwkv_ln_b, hyb_w_out, diff_w_qkv, diff_lq1, diff_lk1, diff_lq2, diff_lk2, diff_subln_g, diff_w_out, peer_w_q, peer_subkeys, peer_u, peer_v, final_g):
    batch, seq, d = x.shape
    depth = ada_w.shape[0]
    t = batch * seq
    xt = x.reshape(t, d)
    pos = positions.reshape(t, 1)
    mod = _ada_call(c, ada_w, ada_b).reshape(depth, batch, 6, 1, d)

    for l in range(depth):
        sh1, sc1, g1, sh2, sc2, g2 = (mod[l, :, j] for j in range(6))
        gmix = norm_mix_g[l].reshape(1, d)
        if l % 2 == 0:
            e = l // 2
            ya, yb = _hybrid_mixer(
                xt, gmix, sh1, sc1, batch, seq, hyb_w_in[e], conv_w[e], conv_b[e], conv_ln_g[e],
                conv_ln_b[e], rwkv_mu[e], rwkv_w0[e], rwkv_w2[e], rwkv_a0[e], rwkv_a2[e], rwkv_g2[e],
                rwkv_k_k[e], rwkv_k_a[e], rwkv_r_k[e], rwkv_ln_g[e], rwkv_ln_b[e])
            w_out = hyb_w_out[e].astype(BF16)
            xt = _out_proj_call(xt, g1, [ya.astype(BF16), yb.astype(BF16)],
                                [w_out[:CONV_WIDTH], w_out[CONV_WIDTH:]], seq)
        else:
            o = l // 2
            lambda_init = 0.8 - 0.6 * math.exp(-0.3 * l)
            wqkv = diff_w_qkv[o].astype(BF16)
            n = wqkv.shape[1] // 3
            q, k, v = _qkv_rope_call(xt, gmix, sh1, sc1, pos, wqkv[:, :n], wqkv[:, n:2 * n], wqkv[:, 2 * n:],
                                     seq)
            y = _attn_call(q, k, v, diff_lq1[o], diff_lk1[o], diff_lq2[o], diff_lk2[o], diff_subln_g[o],
                           lambda_init, batch, seq)
            xt = _out_proj_call(xt, g1, [y], [diff_w_out[o].astype(BF16)], seq)

        q, hb = _norm_matmul_call(xt, norm_ffn_g[l].reshape(1, d), sh2, sc2, [peer_w_q[l]], [F32], seq,
                                  exact=True, emit_h=True, name="peer_query")
        r2, e2, n1, f = _peer_route_call(q, peer_subkeys[l], tt=PEER_TOKEN_TILE)
        xt = _peer_expert_call(xt, g2, hb, peer_u[l].astype(BF16), peer_v[l].T.astype(BF16),
                               n1, f, r2, e2, seq, tt=PEER_TOKEN_TILE)

    return _final_norm_call(xt, final_g).reshape(batch, seq, d)
```

```python
import functools
import math

import jax
import jax.numpy as jnp
from jax import lax
from jax.experimental import pallas as pl
from jax.experimental.pallas import tpu as pltpu

F32 = jnp.float32
BF16 = jnp.bfloat16
HIGHEST = lax.Precision.HIGHEST

CONV_WIDTH = 512
CONV_KERNEL = 31
RWKV_WIDTH = 512
RWKV_HEAD = 64
RWKV_HEADS = 8
DECAY_LORA = 64
AAA_LORA = 64
GATE_LORA = 160
DIFF_HEADS = 8
DIFF_HEAD_DIM = 64
DIFF_V_DIM = 128
ROPE_DIM = 16
ROPE_THETA = 500000.0
ATTN_CHUNK = 64
PEER_HEADS = 8
PEER_KEYS = 128
PEER_HALF = 128
PEER_TOPK = 16
NORM_EPS = 1e-6
LN_EPS = 1e-5
RWKV_GN_EPS = 64e-5

LANES = 128
SCAN_CHUNK = 64
CONV_HALO = 32
PEER_TOKEN_TILE = 256
VMEM_LIMIT = 48 * 1024 * 1024


def _cparams(semantics, flags=None):
    return pltpu.CompilerParams(dimension_semantics=semantics, vmem_limit_bytes=VMEM_LIMIT, flags=flags)


def _nt_dot(a, b, precision=None):
    return lax.dot_general(a, b, (((1,), (1,)), ((), ())), precision=precision,
                           preferred_element_type=F32)


def _dot(a, b, precision=None):
    return jnp.dot(a, b, precision=precision, preferred_element_type=F32)


def _sigmoid(x):
    return 1.0 / (1.0 + jnp.exp(-x))


def _ada_kernel(c_ref, w_ref, b_ref, o_ref):
    c = c_ref[...]
    cond = c * _sigmoid(c)
    o_ref[0] = _dot(cond, w_ref[0], HIGHEST) + b_ref[0]


def _ada_call(c, ada_w, ada_b):
    depth, d, n = ada_w.shape
    b = c.shape[0]
    tn = 1536
    return pl.pallas_call(
        _ada_kernel,
        grid=(depth, n // tn),
        in_specs=[
            pl.BlockSpec((b, d), lambda l, j: (0, 0)),
            pl.BlockSpec((1, d, tn), lambda l, j: (l, 0, j)),
            pl.BlockSpec((1, 1, tn), lambda l, j: (l, 0, j)),
        ],
        out_specs=pl.BlockSpec((1, b, tn), lambda l, j: (l, 0, j)),
        out_shape=jax.ShapeDtypeStruct((depth, b, n), F32),
        compiler_params=_cparams(("arbitrary", "arbitrary")),
        name="ada_mod",
    )(c, ada_w, ada_b.reshape(depth, 1, n))


def _modulate(x, g, shift, scale):
    ms = jnp.mean(x * x, axis=-1, keepdims=True)
    y = x * lax.rsqrt(ms + NORM_EPS)
    return (y * g) * (1.0 + scale) + shift


def _split_bf16(a):
    hi = a.astype(BF16)
    return hi, (a - hi.astype(F32)).astype(BF16)


def _norm_matmul_kernel(*refs, n_w, split, emit_h):
    x_ref, g_ref, sh_ref, sc_ref = refs[:4]
    w_refs = refs[4:4 + n_w]
    o_refs = refs[4 + n_w:]
    h = _modulate(x_ref[...], g_ref[...], sh_ref[0], sc_ref[0])
    hb = h.astype(BF16)
    if split:
        h_lo = (h - hb.astype(F32)).astype(BF16)
        for k in range(n_w // 2):
            w_hi, w_lo = w_refs[2 * k][...], w_refs[2 * k + 1][...]
            o_refs[k][...] = (_dot(hb, w_hi) + _dot(h_lo, w_hi) + _dot(hb, w_lo)).astype(o_refs[k].dtype)
    else:
        for w_ref, o_ref in zip(w_refs, o_refs[:n_w]):
            o_ref[...] = _dot(hb, w_ref[...]).astype(o_ref.dtype)
    if emit_h:
        o_refs[-1][...] = h.T.astype(BF16)


def _norm_matmul_call(x, g, shift, scale, weights, out_dtypes, seq, *, split=False, emit_h=False,
                      tm=512, name="norm_matmul"):
    t, d = x.shape
    if split:
        weights = [part for w in weights for part in _split_bf16(w)]
    in_specs = [
        pl.BlockSpec((tm, d), lambda i: (i, 0)),
        pl.BlockSpec((1, d), lambda i: (0, 0)),
        pl.BlockSpec((1, 1, d), lambda i: ((i * tm) // seq, 0, 0)),
        pl.BlockSpec((1, 1, d), lambda i: ((i * tm) // seq, 0, 0)),
    ]
    out_specs, out_shape = [], []
    for w in weights:
        in_specs.append(pl.BlockSpec(w.shape, lambda i: (0, 0)))
    for w, dt in zip(weights[::2] if split else weights, out_dtypes):
        n = w.shape[1]
        out_specs.append(pl.BlockSpec((tm, n), lambda i: (i, 0)))
        out_shape.append(jax.ShapeDtypeStruct((t, n), dt))
    if emit_h:
        out_specs.append(pl.BlockSpec((d, tm), lambda i: (0, i)))
        out_shape.append(jax.ShapeDtypeStruct((d, t), BF16))
    return pl.pallas_call(
        functools.partial(_norm_matmul_kernel, n_w=len(weights), split=split, emit_h=emit_h),
        grid=(t // tm,),
        in_specs=in_specs,
        out_specs=out_specs,
        out_shape=out_shape,
        compiler_params=_cparams(("arbitrary",)),
        name=name,
    )(x, g, shift, scale, *weights)


def _out_proj_kernel(*refs, n_y):
    x_ref, gate_ref = refs[:2]
    y_refs = refs[2:2 + n_y]
    w_refs = refs[2 + n_y:2 + 2 * n_y]
    o_ref = refs[2 + 2 * n_y]
    acc = _dot(y_refs[0][...], w_refs[0][...])
    for y_ref, w_ref in zip(y_refs[1:], w_refs[1:]):
        acc = acc + _dot(y_ref[...], w_ref[...])
    o_ref[...] = x_ref[...] + gate_ref[0] * acc


def _out_proj_call(x, gate, ys, ws, seq, *, tm=512):
    t, d = x.shape
    in_specs = [
        pl.BlockSpec((tm, d), lambda i: (i, 0)),
        pl.BlockSpec((1, 1, d), lambda i: ((i * tm) // seq, 0, 0)),
    ]
    for y in ys:
        in_specs.append(pl.BlockSpec((tm, y.shape[1]), lambda i: (i, 0)))
    for w in ws:
        in_specs.append(pl.BlockSpec(w.shape, lambda i: (0, 0)))
    return pl.pallas_call(
        functools.partial(_out_proj_kernel, n_y=len(ys)),
        grid=(t // tm,),
        in_specs=in_specs,
        out_specs=pl.BlockSpec((tm, d), lambda i: (i, 0)),
        out_shape=jax.ShapeDtypeStruct((t, d), F32),
        compiler_params=_cparams(("arbitrary",)),
        name="out_proj",
    )(x, gate, *ys, *ws)


def _conv_kernel(z_ref, w_ref, b_ref, g_ref, beta_ref, o_ref, ext_ref, *, ts):
    width = CONV_WIDTH

    @pl.when(pl.program_id(1) == 0)
    def _():
        ext_ref[0:CONV_HALO, :] = jnp.zeros((CONV_HALO, width), F32)

    z = z_ref[...]
    u = z[:, :width] * _sigmoid(z[:, width:])
    ext_ref[CONV_HALO:CONV_HALO + ts, :] = u
    base = CONV_HALO - (CONV_KERNEL - 1)
    rows = 64
    for cb in range(width // LANES):
        cs = slice(cb * LANES, (cb + 1) * LANES)
        for rb in range(ts // rows):
            acc = jnp.zeros((rows, LANES), F32)
            for j in range(CONV_KERNEL):
                start = rb * rows + base + j
                acc = acc + w_ref[j:j + 1, cs] * ext_ref[start:start + rows, cs]
            o_ref[rb * rows:(rb + 1) * rows, cs] = acc
    conv = o_ref[...] + b_ref[...]
    mu = jnp.mean(conv, axis=-1, keepdims=True)
    dlt = conv - mu
    var = jnp.mean(dlt * dlt, axis=-1, keepdims=True)
    y = dlt * lax.rsqrt(var + LN_EPS) * g_ref[...] + beta_ref[...]
    o_ref[...] = y * _sigmoid(y)
    ext_ref[0:CONV_HALO, :] = ext_ref[ts:ts + CONV_HALO, :]


def _conv_call(z, conv_w, conv_b, ln_g, ln_b, batch, seq, *, ts=256):
    t = z.shape[0]
    width = CONV_WIDTH
    nts = seq // ts
    vec = lambda a: a.reshape(1, width)
    return pl.pallas_call(
        functools.partial(_conv_kernel, ts=ts),
        grid=(batch, nts),
        in_specs=[
            pl.BlockSpec((ts, 2 * width), lambda b, i: (b * nts + i, 0)),
            pl.BlockSpec((CONV_KERNEL, width), lambda b, i: (0, 0)),
            pl.BlockSpec((1, width), lambda b, i: (0, 0)),
            pl.BlockSpec((1, width), lambda b, i: (0, 0)),
            pl.BlockSpec((1, width), lambda b, i: (0, 0)),
        ],
        out_specs=pl.BlockSpec((ts, width), lambda b, i: (b * nts + i, 0)),
        out_shape=jax.ShapeDtypeStruct((t, width), F32),
        scratch_shapes=[pltpu.VMEM((ts + CONV_HALO, width), F32)],
        compiler_params=_cparams(("arbitrary", "arbitrary")),
        name="conformer_conv",
    )(z, conv_w, vec(conv_b), vec(ln_g), vec(ln_b))


def _softplus(x):
    return jnp.maximum(x, 0.0) + jnp.log(1.0 + jnp.exp(-jnp.abs(x)))


def _unit_lower_inverse(a, row, col):
    n = a.shape[0]
    eye = (row == col).astype(F32)
    blk = lambda m: (row // m) == (col // m)
    bdot = lambda p, q: _dot(p.astype(BF16), q.astype(BF16))
    n1 = jnp.where(blk(8), a, 0.0)
    t = eye + n1
    n2 = bdot(n1, n1)
    t = t + bdot(t, n2)
    n4 = bdot(n2, n2)
    t = t + bdot(t, n4)
    m = 8
    while m < n:
        e = jnp.where(blk(2 * m) & jnp.logical_not(blk(m)), a, 0.0)
        t = t + bdot(t, bdot(e, t))
        m *= 2
    return t


def _rwkv_kernel(zr_ref, zl_ref, mur_ref, mul_ref, w0_ref, a0_ref, kk_ref, ka_ref, rk_ref,
                 lng_ref, lnb_ref, w2_ref, a2_ref, g2_ref, bd_ref, o_ref,
                 extr_ref, extl_ref, state_ref, y_ref):
    c = SCAN_CHUNK
    rw = RWKV_WIDTH
    n = RWKV_HEAD

    @pl.when(pl.program_id(1) == 0)
    def _():
        extr_ref[0:8, :] = jnp.zeros((8, 3 * rw), F32)
        extl_ref[0:8, :] = jnp.zeros((8, rw), F32)
        state_ref[...] = jnp.zeros_like(state_ref)

    zr = zr_ref[...]
    zl = zl_ref[...]
    extr_ref[8:8 + c, :] = zr
    extl_ref[8:8 + c, :] = zl
    zr = zr + (extr_ref[7:7 + c, :] - zr) * mur_ref[...]
    zl = zl + (extl_ref[7:7 + c, :] - zl) * mul_ref[...]
    extr_ref[0:8, :] = extr_ref[c:c + 8, :]
    extl_ref[0:8, :] = extl_ref[c:c + 8, :]

    r = zr[:, 0:rw]
    k = zr[:, rw:2 * rw]
    v = zr[:, 2 * rw:3 * rw]
    wd = zl[:, 0:LANES]
    ad = zl[:, LANES:2 * LANES]
    gd = zl[:, 2 * LANES:4 * LANES]
    bd = bd_ref[...]

    w_raw = -_softplus(-(w0_ref[...] + _dot(jnp.tanh(wd).astype(BF16), w2_ref[...]))) - 0.5
    lw = -jnp.exp(w_raw)
    alpha = _sigmoid(a0_ref[...] + _dot(ad.astype(BF16), a2_ref[...]))
    gate = _dot(_sigmoid(gd).astype(BF16), g2_ref[...])
    kkf = k * kk_ref[...]
    ss = _dot(kkf * kkf, bd, HIGHEST)
    kk = kkf / jnp.maximum(jnp.sqrt(ss), 1e-12)
    kp = k * (1.0 + (alpha - 1.0) * ka_ref[...])
    bonus = _dot(r * kp * rk_ref[...], bd, HIGHEST) * v
    av = -kk
    bv = kk * alpha

    row = lax.broadcasted_iota(jnp.int32, (c, c), 0)
    col = lax.broadcasted_iota(jnp.int32, (c, c), 1)
    incl = (col <= row)
    strict = (col < row)
    cum = _dot(incl.astype(F32), lw, HIGHEST)
    tot = cum[c - 1:c, :]
    g_in = jnp.exp(cum)
    g_inp = jnp.exp(cum - lw)
    g_out = jnp.exp(-cum)
    g_end = jnp.exp(tot - cum)
    g_tot = jnp.exp(tot)
    a_in = av * g_inp
    r_in = r * g_in
    b_out = bv * g_out
    k_out = kp * g_out
    b_end = bv * g_end
    k_end = kp * g_end

    for h in range(RWKV_HEADS):
        hs = slice(h * n, (h + 1) * n)
        p = jnp.concatenate([a_in[:, hs], r_in[:, hs]], axis=0).astype(BF16)
        q = jnp.concatenate([b_out[:, hs], k_out[:, hs]], axis=0).astype(BF16)
        vh = v[:, hs].astype(BF16)
        m = _nt_dot(p, q)
        a_ab = jnp.where(strict, m[:c, :c], 0.0)
        a_ak = jnp.where(strict, m[:c, c:], 0.0)
        a_rb = jnp.where(incl, m[c:, :c], 0.0)
        a_rk = jnp.where(incl, m[c:, c:], 0.0)
        st = state_ref[h]
        ph = _nt_dot(p, st.astype(BF16))
        tinv = _unit_lower_inverse(a_ab, row, col)
        rhs = ph[:c] + _dot(a_ak.astype(BF16), vh)
        u = _dot(tinv.astype(BF16), rhs.astype(BF16))
        ub = u.astype(BF16)
        yh = ph[c:] + _dot(a_rb.astype(BF16), ub) + _dot(a_rk.astype(BF16), vh)
        y_ref[:, hs] = yh
        uv_t = jnp.concatenate([u.T, v[:, hs].T], axis=1).astype(BF16)
        bk = jnp.concatenate([b_end[:, hs], k_end[:, hs]], axis=0).astype(BF16)
        state_ref[h] = st * g_tot[:, hs] + _dot(uv_t, bk)

    y = y_ref[...]
    inv_n = 1.0 / n
    mu_y = _dot(y, bd, HIGHEST) * inv_n
    dy = y - mu_y
    var_y = _dot(dy * dy, bd, HIGHEST) * inv_n
    yn = dy * lax.rsqrt(var_y + RWKV_GN_EPS)
    o_ref[...] = (yn * lng_ref[...] + lnb_ref[...] + bonus) * gate


def _rwkv_call(zr, zl, p, batch, seq):
    t = zr.shape[0]
    c = SCAN_CHUNK
    rw = RWKV_WIDTH
    nc = seq // c
    const = lambda shape: pl.BlockSpec(shape, lambda b, i: (0,) * len(shape))
    vec = const((1, rw))
    return pl.pallas_call(
        _rwkv_kernel,
        grid=(batch, nc),
        in_specs=[
            pl.BlockSpec((c, 3 * rw), lambda b, i: (b * nc + i, 0)),
            pl.BlockSpec((c, rw), lambda b, i: (b * nc + i, 0)),
            const((1, 3 * rw)), vec,
            vec, vec, vec, vec, vec, vec, vec,
            const((LANES, rw)), const((LANES, rw)), const((2 * LANES, rw)),
            const((rw, rw)),
        ],
        out_specs=pl.BlockSpec((c, rw), lambda b, i: (b * nc + i, 0)),
        out_shape=jax.ShapeDtypeStruct((t, rw), F32),
        scratch_shapes=[
            pltpu.VMEM((c + 8, 3 * rw), F32),
            pltpu.VMEM((c + 8, rw), F32),
            pltpu.VMEM((RWKV_HEADS, RWKV_HEAD, RWKV_HEAD), F32),
            pltpu.VMEM((c, rw), F32),
        ],
        compiler_params=_cparams(("arbitrary", "arbitrary")),
        name="rwkv7_mix",
    )(zr, zl, p["mu_r"], p["mu_l"], p["w0"], p["a0"], p["k_k"], p["k_a"], p["r_k"],
      p["ln_g"], p["ln_b"], p["w2"], p["a2"], p["g2"], p["bd"])


def _qkv_rope_kernel(x_ref, g_ref, sh_ref, sc_ref, pos_ref, inv_ref, sgn_ref, wq_ref, wk_ref, wv_ref,
                     q_ref, k_ref, v_ref):
    h = _modulate(x_ref[...], g_ref[...], sh_ref[0], sc_ref[0]).astype(BF16)
    ang = pos_ref[...].astype(F32) * inv_ref[...]
    cosf = jnp.cos(ang)
    sinf = jnp.sin(ang) * sgn_ref[...]
    lane = lax.broadcasted_iota(jnp.int32, ang.shape, 1)
    low = (lane % DIFF_HEAD_DIM) < (ROPE_DIM // 2)
    half = ROPE_DIM // 2

    def rope(w_ref, o_ref, scale):
        z = _dot(h, w_ref[...])
        for cb in range(z.shape[1] // LANES):
            zc = z[:, cb * LANES:(cb + 1) * LANES]
            partner = jnp.where(low, pltpu.roll(zc, LANES - half, axis=1), pltpu.roll(zc, half, axis=1))
            o_ref[:, cb * LANES:(cb + 1) * LANES] = ((zc * cosf + partner * sinf) * scale).astype(o_ref.dtype)

    rope(wq_ref, q_ref, DIFF_HEAD_DIM ** -0.5)
    rope(wk_ref, k_ref, 1.0)
    v_ref[...] = _dot(h, wv_ref[...]).astype(v_ref.dtype)


def _qkv_rope_call(x, g, shift, scale, pos, wq, wk, wv, seq, *, tm=512):
    t, d = x.shape
    half = ROPE_DIM // 2
    inv = 1.0 / (ROPE_THETA ** (jnp.arange(0, ROPE_DIM, 2, dtype=F32) / ROPE_DIM))
    dl = jnp.arange(LANES) % DIFF_HEAD_DIM
    inv_pat = jnp.where(dl < ROPE_DIM, inv[dl % half], 0.0).reshape(1, LANES).astype(F32)
    sgn_pat = jnp.where(dl < half, -1.0, 1.0).reshape(1, LANES).astype(F32)
    row = lambda n: pl.BlockSpec((tm, n), lambda i: (i, 0))
    full = lambda a: pl.BlockSpec(a.shape, lambda i: (0, 0))
    mod = pl.BlockSpec((1, 1, d), lambda i: ((i * tm) // seq, 0, 0))
    n = wq.shape[1]
    return pl.pallas_call(
        _qkv_rope_kernel,
        grid=(t // tm,),
        in_specs=[row(d), full(g), mod, mod, row(1), full(inv_pat), full(sgn_pat),
                  full(wq), full(wk), full(wv)],
        out_specs=[row(n), row(n), row(n)],
        out_shape=[jax.ShapeDtypeStruct((t, n), BF16)] * 3,
        compiler_params=_cparams(("arbitrary",)),
        name="qkv_rope",
    )(x, g, shift, scale, pos, inv_pat, sgn_pat, wq, wk, wv)


def _attn_kernel(q_ref, k_ref, v_ref, lq1_ref, lk1_ref, lq2_ref, lk2_ref, sg_ref, o_ref,
                 *, tq, seq, lambda_init):
    i = pl.program_id(2)
    kb = 2 * tq
    q = q_ref[...]
    lane = lax.broadcasted_iota(jnp.int32, q.shape, 1)
    zero = jnp.zeros_like(q)
    qm = (jnp.where(lane < DIFF_HEAD_DIM, q, zero), jnp.where(lane >= DIFF_HEAD_DIM, q, zero))
    f32sum = lambda a, b: jnp.sum(a[...] * b[...], axis=-1, keepdims=True)
    lam = jnp.exp(f32sum(lq1_ref, lk1_ref)) - jnp.exp(f32sum(lq2_ref, lk2_ref)) + lambda_init
    row_chunk = (i * tq + lax.broadcasted_iota(jnp.int32, (tq, kb), 0)) // ATTN_CHUNK
    col_in_block = lax.broadcasted_iota(jnp.int32, (tq, kb), 1)

    def attend(nblk):
        outs = []
        for mp in range(2):
            s = [_nt_dot(qm[mp], k_ref[j * kb:(j + 1) * kb, :]) for j in range(nblk)]
            col_chunk = ((nblk - 1) * kb + col_in_block) // ATTN_CHUNK
            s[-1] = jnp.where(col_chunk <= row_chunk, s[-1], -jnp.inf)
            m = jnp.max(s[0], axis=-1, keepdims=True)
            for sj in s[1:]:
                m = jnp.maximum(m, jnp.max(sj, axis=-1, keepdims=True))
            l = jnp.zeros_like(m)
            acc = jnp.zeros((tq, DIFF_V_DIM), F32)
            for j, sj in enumerate(s):
                p = jnp.exp(sj - m)
                l = l + jnp.sum(p, axis=-1, keepdims=True)
                acc = acc + _dot(p.astype(BF16), v_ref[j * kb:(j + 1) * kb, :])
            outs.append(acc / l)
        o = outs[0] - lam * outs[1]
        ms = jnp.mean(o * o, axis=-1, keepdims=True)
        o = o * lax.rsqrt(ms + NORM_EPS) * sg_ref[...] * (1.0 - lambda_init)
        o_ref[...] = o.astype(o_ref.dtype)

    for nblk in range(1, seq // kb + 1):
        pl.when(i // 2 == nblk - 1)(functools.partial(attend, nblk))


def _attn_call(q, k, v, lq1, lk1, lq2, lk2, subln_g, lambda_init, batch, seq, *, tq=256):
    t = q.shape[0]
    nq = seq // tq
    assert seq % (2 * tq) == 0
    vec = lambda a: a.reshape(1, -1)
    small = lambda a: pl.BlockSpec(a.shape, lambda b, h, i: (0, 0))
    kv = pl.BlockSpec((seq, DIFF_V_DIM), lambda b, h, i: (b, h))
    args = [vec(lq1), vec(lk1), vec(lq2), vec(lk2), vec(subln_g)]
    return pl.pallas_call(
        functools.partial(_attn_kernel, tq=tq, seq=seq, lambda_init=lambda_init),
        grid=(batch, DIFF_HEADS, nq),
        in_specs=[pl.BlockSpec((tq, DIFF_V_DIM), lambda b, h, i: (b * nq + i, h)), kv, kv]
                 + [small(a) for a in args],
        out_specs=pl.BlockSpec((tq, DIFF_V_DIM), lambda b, h, i: (b * nq + i, h)),
        out_shape=jax.ShapeDtypeStruct((t, DIFF_HEADS * DIFF_V_DIM), BF16),
        compiler_params=_cparams(("arbitrary", "arbitrary", "arbitrary")),
        name="diff_attn",
    )(q, k, v, *args)


def _peer_route_kernel(q_ref, sk_ref, r2_ref, e2_ref, n1_ref, f_ref, a_ref, b_ref, cand_ref, *, tt):
    neg = -jnp.inf
    k = PEER_TOPK
    kf = float(k)
    key_id = lax.broadcasted_iota(jnp.int32, (PEER_KEYS, LANES), 0).astype(F32)
    cand_id = lax.broadcasted_iota(jnp.int32, (k * k, LANES), 0).astype(F32)

    def extract(w, ids, dst_ref, by_index, want_rank):
        rank = jnp.full(w.shape, kf, F32) if want_rank else None
        tops = []
        for r in range(k):
            mx = jnp.max(w, axis=0, keepdims=True)
            tops.append(mx)
            if dst_ref is not None:
                dst_ref[r:r + 1, :] = mx
            hit = w == mx
            if by_index:
                first = jnp.min(jnp.where(hit, ids, float(w.shape[0])), axis=0, keepdims=True)
                hit = ids == first
            if want_rank:
                rank = jnp.where(hit, float(r), rank)
            w = jnp.where(hit, neg, w)
        return rank, tops, w

    def candidates():
        bvals = b_ref[...]
        for x in range(k):
            cand_ref[x * k:(x + 1) * k, :] = a_ref[x:x + 1, :] + bvals
        return cand_ref[...]

    def staircase_candidates():
        a = lambda lo, hi: a_ref[lo:hi, :]
        b = lambda lo, hi: b_ref[lo:hi, :]
        return jnp.concatenate([
            a(0, 1) + b(0, 8), a(0, 1) + b(8, 16), a(1, 2) + b(0, 8), a(2, 3) + b(0, 8), a(3, 4) + b(0, 8),
            a(0, 8) + b(0, 1), a(8, 16) + b(0, 1), a(0, 8) + b(1, 2), a(0, 8) + b(2, 3)], axis=0)

    def twin_bf16(v):
        bits = lax.bitcast_convert_type(v.astype(BF16).astype(F32), jnp.uint32)
        return bits | (bits >> 16)

    def emit(cols, s1, s2, rank2, n1, a0, b0, c_tops):
        zsum = jnp.zeros((1, LANES), F32)
        for c in c_tops:
            zsum = zsum + jnp.exp(c - c_tops[0])
        r2_ref[0, 0, :, cols] = pltpu.bitcast(rank2.astype(BF16), jnp.uint32)
        e2_ref[0, 0, :, cols] = pltpu.bitcast(jnp.exp(s2 - b0).astype(BF16), jnp.uint32)
        n1_ref[0, 0, :, cols] = twin_bf16(n1)
        f_ref[0, 0, :, cols] = twin_bf16(jnp.exp(s1 - a0) / zsum)

    count = lambda m: jnp.sum(jnp.where(m, 1.0, 0.0), axis=0, keepdims=True)

    for gidx in range(tt // LANES):
        cols = slice(gidx * LANES, (gidx + 1) * LANES)
        s1 = _nt_dot(sk_ref[0], q_ref[cols, 0:PEER_HALF], HIGHEST)
        s2 = _nt_dot(sk_ref[1], q_ref[cols, PEER_HALF:2 * PEER_HALF], HIGHEST)

        _, a_tops, w1 = extract(s1, key_id, a_ref, False, False)
        rank2, b_tops, _ = extract(s2, key_id, b_ref, False, True)
        _, c_tops, _ = extract(staircase_candidates(), None, None, False, False)
        tau = c_tops[k - 1]
        n1 = jnp.zeros_like(s1)
        for y in range(k):
            n1 = n1 + jnp.where((s1 + b_tops[y]) >= tau, 1.0, 0.0)
        emit(cols, s1, s2, rank2, n1, a_tops[0], b_tops[0], c_tops)
        tied = ((count(w1 == neg) != kf) | (count(rank2 < kf) != kf)
                | (jnp.sum(n1, axis=0, keepdims=True) != kf))

        @pl.when(jnp.max(jnp.where(tied, 1.0, 0.0)) > 0.0)
        def _():
            rank1, a_x, _ = extract(s1, key_id, a_ref, True, True)
            rank2x, b_x, _ = extract(s2, key_id, b_ref, True, True)
            rank3, c_x, _ = extract(candidates(), cand_id, None, True, True)
            win = rank3 < kf
            n1x = jnp.zeros_like(s1)
            for x in range(k):
                n_x = count(win[x * k:(x + 1) * k])
                n1x = n1x + jnp.where(rank1 == float(x), n_x, 0.0)
            emit(cols, s1, s2, rank2x, n1x, a_x[0], b_x[0], c_x)


def _peer_route_call(q, subkeys, *, tt):
    t = q.shape[0]
    nt = t // tt
    spec = pl.BlockSpec((1, 1, PEER_KEYS, tt), lambda i, h: (h, i, 0, 0))
    pair_spec = pl.BlockSpec((1, 1, PEER_KEYS // 2, tt), lambda i, h: (h, i, 0, 0))
    tab = lambda rows: jax.ShapeDtypeStruct((PEER_HEADS, nt, rows, tt), jnp.uint32)
    return pl.pallas_call(
        functools.partial(_peer_route_kernel, tt=tt),
        grid=(nt, PEER_HEADS),
        in_specs=[
            pl.BlockSpec((tt, 2 * PEER_HALF), lambda i, h: (i, h)),
            pl.BlockSpec(subkeys.shape, lambda i, h: (0, 0, 0)),
        ],
        out_specs=[pair_spec, pair_spec, spec, spec],
        out_shape=[tab(PEER_KEYS // 2), tab(PEER_KEYS // 2), tab(PEER_KEYS), tab(PEER_KEYS)],
        scratch_shapes=[
            pltpu.VMEM((PEER_TOPK, LANES), F32),
            pltpu.VMEM((PEER_TOPK, LANES), F32),
            pltpu.VMEM((PEER_TOPK * PEER_TOPK, LANES), F32),
        ],
        compiler_params=_cparams(("arbitrary", "arbitrary")),
        name="peer_route",
    )(q, subkeys)


def _gelu(x):
    return 0.5 * x * (1.0 + lax.erf(x * (1.0 / math.sqrt(2.0))))


def _peer_expert_kernel(x_ref, gate_ref, ht_ref, u_ref, vt_ref, n1_ref, f_ref, r2_ref, e2_ref,
                        o_ref, pre_a, pre_b, wa_ref, acc_ref, *, rows_per_tile):
    s = pl.program_id(1)
    te, tt = pre_a.shape
    d = acc_ref.shape[0]
    blk = 16
    rows_per_piece = 2

    @pl.when(s == 0)
    def _():
        acc_ref[...] = jnp.zeros_like(acc_ref)
        pre_b[...] = jnp.zeros_like(pre_b)

    def step(pre_new, pre_old):
        halves = [slice(m * d // 2, (m + 1) * d // 2) for m in range(2)]
        n_pre = 2
        for m in range(n_pre):
            rows = slice(m * te // n_pre, (m + 1) * te // n_pre)
            pre_new[rows, :] = _dot(u_ref[rows, :], ht_ref[...])
        zero = jnp.zeros((blk, LANES), BF16)
        as_bf16 = lambda words: pltpu.bitcast(words, BF16)
        for il in range(rows_per_tile):
            for lt in range(tt // LANES):
                lanes = slice(lt * LANES, (lt + 1) * LANES)
                row = lambda ref, h: as_bf16(jnp.broadcast_to(ref[h, 0, il:il + 1, lanes], (blk // 2, LANES)))
                n1b = [row(n1_ref, h) for h in range(PEER_HEADS)]
                fb = [row(f_ref, h) for h in range(PEER_HEADS)]
                for jb in range(PEER_KEYS // blk):
                    words = slice(jb * blk // 2, (jb + 1) * blk // 2)
                    w = zero
                    for h in range(PEER_HEADS):
                        hit = as_bf16(r2_ref[h, 0, words, lanes]) < n1b[h]
                        w = w + jnp.where(hit, as_bf16(e2_ref[h, 0, words, lanes]), zero) * fb[h]
                    rows = slice(il * PEER_KEYS + jb * blk, il * PEER_KEYS + (jb + 1) * blk)
                    wa_ref[rows, lanes] = w * _gelu(pre_old[rows, lanes]).astype(BF16)
        for rows in halves:
            acc_ref[rows, :] += _dot(vt_ref[rows, :], wa_ref[...])

    @pl.when(s % 2 == 0)
    def _():
        step(pre_a, pre_b)

    @pl.when(s % 2 == 1)
    def _():
        step(pre_b, pre_a)

    @pl.when(s == pl.num_programs(1) - 1)
    def _():
        o_ref[...] = x_ref[...] + gate_ref[0] * acc_ref[...].T


def _peer_expert_call(x, gate, hb, u_b, vt_b, n1, f, r2, e2, seq, *, tt, te=1024):
    t, d = x.shape
    ne = u_b.shape[0] // te
    rows_per_tile = te // PEER_KEYS
    lag = lambda s, n: jnp.clip(s - n, 0, ne - 1)
    sel = pl.BlockSpec((PEER_HEADS, 1, rows_per_tile, tt), lambda i, s: (0, i, lag(s, 1), 0))
    full = pl.BlockSpec((PEER_HEADS, 1, PEER_KEYS // 2, tt), lambda i, s: (0, i, 0, 0))
    return pl.pallas_call(
        functools.partial(_peer_expert_kernel, rows_per_tile=rows_per_tile),
        grid=(t // tt, ne + 1),
        in_specs=[
            pl.BlockSpec((tt, d), lambda i, s: (i, 0)),
            pl.BlockSpec((1, 1, d), lambda i, s: ((i * tt) // seq, 0, 0)),
            pl.BlockSpec((d, tt), lambda i, s: (0, i)),
            pl.BlockSpec((te, d), lambda i, s: (lag(s, 0), 0)),
            pl.BlockSpec((d, te), lambda i, s: (0, lag(s, 1))),
            sel, sel, full, full,
        ],
        out_specs=pl.BlockSpec((tt, d), lambda i, s: (i, 0)),
        out_shape=jax.ShapeDtypeStruct((t, d), F32),
        scratch_shapes=[
            pltpu.VMEM((te, tt), F32),
            pltpu.VMEM((te, tt), F32),
            pltpu.VMEM((te, tt), BF16),
            pltpu.VMEM((d, tt), F32),
        ],
        compiler_params=_cparams(("arbitrary", "arbitrary")),
        name="peer_experts",
    )(x, gate, hb, u_b, vt_b, n1, f, r2, e2)


def _final_norm_kernel(x_ref, g_ref, o_ref):
    x = x_ref[...]
    ms = jnp.mean(x * x, axis=-1, keepdims=True)
    o_ref[...] = x * lax.rsqrt(ms + NORM_EPS) * g_ref[...]


def _final_norm_call(x, g, *, tm=512):
    t, d = x.shape
    return pl.pallas_call(
        _final_norm_kernel,
        grid=(t // tm,),
        in_specs=[pl.BlockSpec((tm, d), lambda i: (i, 0)), pl.BlockSpec((1, d), lambda i: (0, 0))],
        out_specs=pl.BlockSpec((tm, d), lambda i: (i, 0)),
        out_shape=jax.ShapeDtypeStruct((t, d), F32),
        compiler_params=_cparams(("arbitrary",)),
        name="final_norm",
    )(x, g.reshape(1, d))


def _pad_rows(a, rows):
    return jnp.pad(a, ((0, rows - a.shape[0]), (0, 0)))


def _pad_cols(a, cols):
    return jnp.pad(a, ((0, 0), (0, cols - a.shape[1])))


def _hybrid_mixer(x, g, shift, scale, batch, seq, w_in, conv_w, conv_b, conv_ln_g, conv_ln_b, mu, w0, w2,
                  a0, a2, g2, k_k, k_a, r_k, ln_g, ln_b):
    rw = RWKV_WIDTH
    c0 = 2 * CONV_WIDTH
    lora0 = c0 + 3 * rw
    l1, l2 = lora0 + DECAY_LORA, lora0 + DECAY_LORA + AAA_LORA
    w_conv = w_in[:, :c0].astype(BF16)
    w_rkv = w_in[:, c0:lora0].astype(BF16)
    w_lora = jnp.concatenate([
        _pad_cols(w_in[:, lora0:l1], LANES), _pad_cols(w_in[:, l1:l2], LANES),
        _pad_cols(w_in[:, l2:], 2 * LANES)], axis=1).astype(BF16)
    zc, zr, zl = _norm_matmul_call(x, g, shift, scale, [w_conv, w_rkv, w_lora], [F32, F32, F32], seq,
                                   name="hybrid_in_proj")
    ya = _conv_call(zc, conv_w, conv_b, conv_ln_g, conv_ln_b, batch, seq)
    mu_l = jnp.concatenate([
        _pad_cols(mu[None, lora0 - c0:l1 - c0], LANES), _pad_cols(mu[None, l1 - c0:l2 - c0], LANES),
        _pad_cols(mu[None, l2 - c0:], 2 * LANES)], axis=1)
    vec = lambda a: a.reshape(1, rw)
    head_id = jnp.arange(rw) // RWKV_HEAD
    params = dict(
        mu_r=mu[None, :3 * rw], mu_l=mu_l, w0=vec(w0), a0=vec(a0), k_k=vec(k_k), k_a=vec(k_a),
        r_k=vec(r_k), ln_g=vec(ln_g), ln_b=vec(ln_b),
        w2=_pad_rows(w2, LANES).astype(BF16), a2=_pad_rows(a2, LANES).astype(BF16),
        g2=_pad_rows(g2, 2 * LANES).astype(BF16),
        bd=(head_id[:, None] == head_id[None, :]).astype(F32))
    yb = _rwkv_call(zr, zl, params, batch, seq)
    return ya, yb


def kernel(x, c, positions, ada_w, ada_b, norm_mix_g, norm_ffn_g, hyb_w_in, conv_w, conv_b, conv_ln_g, conv_ln_b, rwkv_mu, rwkv_w0, rwkv_w2, rwkv_a0, rwkv_a2, rwkv_g2, rwkv_k_k, rwkv_k_a, rwkv_r_k, rwkv_ln_g, rwkv_ln_b, hyb_w_out, diff_w_qkv, diff_lq1, diff_lk1, diff_lq2, diff_lk2, diff_subln_g, diff_w_out, peer_w_q, peer_subkeys, peer_u, peer_v, final_g):
    batch, seq, d = x.shape
    depth = ada_w.shape[0]
    t = batch * seq
    xt = x.reshape(t, d)
    pos = positions.reshape(t, 1)
    mod = _ada_call(c, ada_w, ada_b).reshape(depth, batch, 6, 1, d)

    for l in range(depth):
        sh1, sc1, g1, sh2, sc2, g2 = (mod[l, :, j] for j in range(6))
        gmix = norm_mix_g[l].reshape(1, d)
        if l % 2 == 0:
            e = l // 2
            ya, yb = _hybrid_mixer(
                xt, gmix, sh1, sc1, batch, seq, hyb_w_in[e], conv_w[e], conv_b[e], conv_ln_g[e],
                conv_ln_b[e], rwkv_mu[e], rwkv_w0[e], rwkv_w2[e], rwkv_a0[e], rwkv_a2[e], rwkv_g2[e],
                rwkv_k_k[e], rwkv_k_a[e], rwkv_r_k[e], rwkv_ln_g[e], rwkv_ln_b[e])
            w_out = hyb_w_out[e].astype(BF16)
            xt = _out_proj_call(xt, g1, [ya.astype(BF16), yb.astype(BF16)],
                                [w_out[:CONV_WIDTH], w_out[CONV_WIDTH:]], seq)
        else:
            o = l // 2
            lambda_init = 0.8 - 0.6 * math.exp(-0.3 * l)
            wqkv = diff_w_qkv[o].astype(BF16)
            n = wqkv.shape[1] // 3
            q, k, v = _qkv_rope_call(xt, gmix, sh1, sc1, pos, wqkv[:, :n], wqkv[:, n:2 * n], wqkv[:, 2 * n:],
                                     seq)
            y = _attn_call(q, k, v, diff_lq1[o], diff_lk1[o], diff_lq2[o], diff_lk2[o], diff_subln_g[o],
                           lambda_init, batch, seq)
            xt = _out_proj_call(xt, g1, [y], [diff_w_out[o].astype(BF16)], seq)

        q, hb = _norm_matmul_call(xt, norm_ffn_g[l].reshape(1, d), sh2, sc2, [peer_w_q[l]], [F32], seq,
                                  split=True, emit_h=True, name="peer_query")
        r2, e2, n1, f = _peer_route_call(q, peer_subkeys[l], tt=PEER_TOKEN_TILE)
        xt = _peer_expert_call(xt, g2, hb, peer_u[l].astype(BF16), peer_v[l].T.astype(BF16),
                               n1, f, r2, e2, seq, tt=PEER_TOKEN_TILE)

    return _final_norm_call(xt, final_g).reshape(batch, seq, d)
```

```python
import functools
import math

import jax
import jax.numpy as jnp
from jax import lax
from jax.experimental import pallas as pl
from jax.experimental.pallas import tpu as pltpu

F32 = jnp.float32
BF16 = jnp.bfloat16
HIGHEST = lax.Precision.HIGHEST

CONV_WIDTH = 512
CONV_KERNEL = 31
RWKV_WIDTH = 512
RWKV_HEAD = 64
RWKV_HEADS = 8
DECAY_LORA = 64
AAA_LORA = 64
GATE_LORA = 160
DIFF_HEADS = 8
DIFF_HEAD_DIM = 64
DIFF_V_DIM = 128
ROPE_DIM = 16
ROPE_THETA = 500000.0
ATTN_CHUNK = 64
PEER_HEADS = 8
PEER_KEYS = 128
PEER_HALF = 128
PEER_TOPK = 16
NORM_EPS = 1e-6
LN_EPS = 1e-5
RWKV_GN_EPS = 64e-5

LANES = 128
SCAN_CHUNK = 64
CONV_HALO = 32
PEER_TOKEN_TILE = 256
VMEM_LIMIT = 48 * 1024 * 1024


def _cparams(semantics, flags=None):
    return pltpu.CompilerParams(dimension_semantics=semantics, vmem_limit_bytes=VMEM_LIMIT, flags=flags)


def _nt_dot(a, b, precision=None):
    return lax.dot_general(a, b, (((1,), (1,)), ((), ())), precision=precision,
                           preferred_element_type=F32)


def _dot(a, b, precision=None):
    return jnp.dot(a, b, precision=precision, preferred_element_type=F32)


def _sigmoid(x):
    return 1.0 / (1.0 + jnp.exp(-x))


def _ada_kernel(c_ref, w_ref, b_ref, o_ref):
    c = c_ref[...]
    cond = c * _sigmoid(c)
    o_ref[0] = _dot(cond, w_ref[0], HIGHEST) + b_ref[0]


def _ada_call(c, ada_w, ada_b):
    depth, d, n = ada_w.shape
    b = c.shape[0]
    tn = 1536
    return pl.pallas_call(
        _ada_kernel,
        grid=(depth, n // tn),
        in_specs=[
            pl.BlockSpec((b, d), lambda l, j: (0, 0)),
            pl.BlockSpec((1, d, tn), lambda l, j: (l, 0, j)),
            pl.BlockSpec((1, 1, tn), lambda l, j: (l, 0, j)),
        ],
        out_specs=pl.BlockSpec((1, b, tn), lambda l, j: (l, 0, j)),
        out_shape=jax.ShapeDtypeStruct((depth, b, n), F32),
        compiler_params=_cparams(("arbitrary", "arbitrary")),
        name="ada_mod",
    )(c, ada_w, ada_b.reshape(depth, 1, n))


def _modulate(x, g, shift, scale):
    ms = jnp.mean(x * x, axis=-1, keepdims=True)
    y = x * lax.rsqrt(ms + NORM_EPS)
    return (y * g) * (1.0 + scale) + shift


def _split_bf16(a):
    hi = a.astype(BF16)
    return hi, (a - hi.astype(F32)).astype(BF16)


def _norm_matmul_kernel(*refs, n_w, split, emit_h):
    x_ref, g_ref, sh_ref, sc_ref = refs[:4]
    w_refs = refs[4:4 + n_w]
    o_refs = refs[4 + n_w:]
    h = _modulate(x_ref[...], g_ref[...], sh_ref[0], sc_ref[0])
    hb = h.astype(BF16)
    if split:
        h_lo = (h - hb.astype(F32)).astype(BF16)
        for k in range(n_w // 2):
            w_hi, w_lo = w_refs[2 * k][...], w_refs[2 * k + 1][...]
            o_refs[k][...] = (_dot(hb, w_hi) + _dot(h_lo, w_hi) + _dot(hb, w_lo)).astype(o_refs[k].dtype)
    else:
        for w_ref, o_ref in zip(w_refs, o_refs[:n_w]):
            o_ref[...] = _dot(hb, w_ref[...]).astype(o_ref.dtype)
    if emit_h:
        o_refs[-1][...] = h.T.astype(BF16)


def _norm_matmul_call(x, g, shift, scale, weights, out_dtypes, seq, *, split=False, emit_h=False,
                      tm=512, name="norm_matmul"):
    t, d = x.shape
    if split:
        weights = [part for w in weights for part in _split_bf16(w)]
    in_specs = [
        pl.BlockSpec((tm, d), lambda i: (i, 0)),
        pl.BlockSpec((1, d), lambda i: (0, 0)),
        pl.BlockSpec((1, 1, d), lambda i: ((i * tm) // seq, 0, 0)),
        pl.BlockSpec((1, 1, d), lambda i: ((i * tm) // seq, 0, 0)),
    ]
    out_specs, out_shape = [], []
    for w in weights:
        in_specs.append(pl.BlockSpec(w.shape, lambda i: (0, 0)))
    for w, dt in zip(weights[::2] if split else weights, out_dtypes):
        n = w.shape[1]
        out_specs.append(pl.BlockSpec((tm, n), lambda i: (i, 0)))
        out_shape.append(jax.ShapeDtypeStruct((t, n), dt))
    if emit_h:
        out_specs.append(pl.BlockSpec((d, tm), lambda i: (0, i)))
        out_shape.append(jax.ShapeDtypeStruct((d, t), BF16))
    return pl.pallas_call(
        functools.partial(_norm_matmul_kernel, n_w=len(weights), split=split, emit_h=emit_h),
        grid=(t // tm,),
        in_specs=in_specs,
        out_specs=out_specs,
        out_shape=out_shape,
        compiler_params=_cparams(("arbitrary",)),
        name=name,
    )(x, g, shift, scale, *weights)


def _out_proj_kernel(*refs, n_y):
    x_ref, gate_ref = refs[:2]
    y_refs = refs[2:2 + n_y]
    w_refs = refs[2 + n_y:2 + 2 * n_y]
    o_ref = refs[2 + 2 * n_y]
    acc = _dot(y_refs[0][...], w_refs[0][...])
    for y_ref, w_ref in zip(y_refs[1:], w_refs[1:]):
        acc = acc + _dot(y_ref[...], w_ref[...])
    o_ref[...] = x_ref[...] + gate_ref[0] * acc


def _out_proj_call(x, gate, ys, ws, seq, *, tm=512):
    t, d = x.shape
    in_specs = [
        pl.BlockSpec((tm, d), lambda i: (i, 0)),
        pl.BlockSpec((1, 1, d), lambda i: ((i * tm) // seq, 0, 0)),
    ]
    for y in ys:
        in_specs.append(pl.BlockSpec((tm, y.shape[1]), lambda i: (i, 0)))
    for w in ws:
        in_specs.append(pl.BlockSpec(w.shape, lambda i: (0, 0)))
    return pl.pallas_call(
        functools.partial(_out_proj_kernel, n_y=len(ys)),
        grid=(t // tm,),
        in_specs=in_specs,
        out_specs=pl.BlockSpec((tm, d), lambda i: (i, 0)),
        out_shape=jax.ShapeDtypeStruct((t, d), F32),
        compiler_params=_cparams(("arbitrary",)),
        name="out_proj",
    )(x, gate, *ys, *ws)


def _conv_kernel(z_ref, w_ref, b_ref, g_ref, beta_ref, o_ref, ext_ref, *, ts):
    width = CONV_WIDTH

    @pl.when(pl.program_id(1) == 0)
    def _():
        ext_ref[0:CONV_HALO, :] = jnp.zeros((CONV_HALO, width), F32)

    z = z_ref[...]
    u = z[:, :width] * _sigmoid(z[:, width:])
    ext_ref[CONV_HALO:CONV_HALO + ts, :] = u
    base = CONV_HALO - (CONV_KERNEL - 1)
    rows = 64
    for cb in range(width // LANES):
        cs = slice(cb * LANES, (cb + 1) * LANES)
        for rb in range(ts // rows):
            acc = jnp.zeros((rows, LANES), F32)
            for j in range(CONV_KERNEL):
                start = rb * rows + base + j
                acc = acc + w_ref[j:j + 1, cs] * ext_ref[start:start + rows, cs]
            o_ref[rb * rows:(rb + 1) * rows, cs] = acc
    conv = o_ref[...] + b_ref[...]
    mu = jnp.mean(conv, axis=-1, keepdims=True)
    dlt = conv - mu
    var = jnp.mean(dlt * dlt, axis=-1, keepdims=True)
    y = dlt * lax.rsqrt(var + LN_EPS) * g_ref[...] + beta_ref[...]
    o_ref[...] = y * _sigmoid(y)
    ext_ref[0:CONV_HALO, :] = ext_ref[ts:ts + CONV_HALO, :]


def _conv_call(z, conv_w, conv_b, ln_g, ln_b, batch, seq, *, ts=256):
    t = z.shape[0]
    width = CONV_WIDTH
    nts = seq // ts
    vec = lambda a: a.reshape(1, width)
    return pl.pallas_call(
        functools.partial(_conv_kernel, ts=ts),
        grid=(batch, nts),
        in_specs=[
            pl.BlockSpec((ts, 2 * width), lambda b, i: (b * nts + i, 0)),
            pl.BlockSpec((CONV_KERNEL, width), lambda b, i: (0, 0)),
            pl.BlockSpec((1, width), lambda b, i: (0, 0)),
            pl.BlockSpec((1, width), lambda b, i: (0, 0)),
            pl.BlockSpec((1, width), lambda b, i: (0, 0)),
        ],
        out_specs=pl.BlockSpec((ts, width), lambda b, i: (b * nts + i, 0)),
        out_shape=jax.ShapeDtypeStruct((t, width), F32),
        scratch_shapes=[pltpu.VMEM((ts + CONV_HALO, width), F32)],
        compiler_params=_cparams(("arbitrary", "arbitrary")),
        name="conformer_conv",
    )(z, conv_w, vec(conv_b), vec(ln_g), vec(ln_b))


def _softplus(x):
    return jnp.maximum(x, 0.0) + jnp.log(1.0 + jnp.exp(-jnp.abs(x)))


def _unit_lower_inverses(mats, row, col, size):
    eye = (row == col).astype(F32)
    blk = lambda m: (row // m) == (col // m)
    bdot = lambda p, q: _dot(p.astype(BF16), q.astype(BF16))
    n1 = [jnp.where(blk(8), a, 0.0) for a in mats]
    t = [eye + x for x in n1]
    n2 = [bdot(x, x) for x in n1]
    t = [x + bdot(x, y) for x, y in zip(t, n2)]
    n4 = [bdot(x, x) for x in n2]
    t = [x + bdot(x, y) for x, y in zip(t, n4)]
    m = 8
    while m < size:
        new = blk(2 * m) & jnp.logical_not(blk(m))
        et = [bdot(jnp.where(new, a, 0.0), x) for a, x in zip(mats, t)]
        t = [x + bdot(x, y) for x, y in zip(t, et)]
        m *= 2
    return t


def _split_dot(x, w_bf16, parts):
    acc = None
    for _ in range(parts):
        piece = x.astype(BF16)
        term = _dot(piece, w_bf16)
        acc = term if acc is None else acc + term
        x = x - piece.astype(F32)
    return acc


RWKV_GROUP = 4
RWKV_BATCH_TILE = 2


def _rwkv_kernel(zr_ref, zl_ref, mur_ref, mul_ref, w0_ref, a0_ref, kk_ref, ka_ref, rk_ref,
                 lng_ref, lnb_ref, w2_ref, a2_ref, g2_ref, seg_ref, o_ref,
                 extr_ref, extl_ref, state_ref):
    c = SCAN_CHUNK
    rw = RWKV_WIDTH
    n = RWKV_HEAD
    nb = RWKV_BATCH_TILE
    gw = RWKV_GROUP * n
    groups = rw // gw

    @pl.when(pl.program_id(1) == 0)
    def _():
        extr_ref[:, 0:8, :] = jnp.zeros((nb, 8, 3 * rw), F32)
        extl_ref[:, 0:8, :] = jnp.zeros((nb, 8, rw), F32)
        state_ref[...] = jnp.zeros_like(state_ref)

    zr_rows, zl_rows = [], []
    for b in range(nb):
        zr = zr_ref[b]
        zl = zl_ref[b]
        extr_ref[b, 8:8 + c, :] = zr
        extl_ref[b, 8:8 + c, :] = zl
        zr_rows.append(zr + (extr_ref[b, 7:7 + c, :] - zr) * mur_ref[...])
        zl_rows.append(zl + (extl_ref[b, 7:7 + c, :] - zl) * mul_ref[...])
        extr_ref[b, 0:8, :] = extr_ref[b, c:c + 8, :]
        extl_ref[b, 0:8, :] = extl_ref[b, c:c + 8, :]
    zr = jnp.concatenate(zr_rows, axis=0)
    zl = jnp.concatenate(zl_rows, axis=0)

    r = zr[:, 0:rw]
    k = zr[:, rw:2 * rw]
    v = zr[:, 2 * rw:3 * rw]
    wd = zl[:, 0:LANES]
    ad = zl[:, LANES:2 * LANES]
    gd = zl[:, 2 * LANES:4 * LANES]
    seg = seg_ref[...]

    def segsum(x, parts):
        return jnp.concatenate([_split_dot(x[:, g * gw:(g + 1) * gw], seg, parts) for g in range(groups)],
                               axis=1)

    w_raw = -_softplus(-(w0_ref[...] + _dot(jnp.tanh(wd).astype(BF16), w2_ref[...]))) - 0.5
    lw = -jnp.exp(w_raw)
    alpha = _sigmoid(a0_ref[...] + _dot(ad.astype(BF16), a2_ref[...]))
    gate = _dot(_sigmoid(gd).astype(BF16), g2_ref[...])
    kkf = k * kk_ref[...]
    kk = kkf / jnp.maximum(jnp.sqrt(segsum(kkf * kkf, 3)), 1e-12)
    kp = k * (1.0 + (alpha - 1.0) * ka_ref[...])
    bonus = segsum(r * kp * rk_ref[...], 3) * v
    av = -kk
    bv = kk * alpha

    trow = lax.broadcasted_iota(jnp.int32, (nb * c, nb * c), 0)
    tcol = lax.broadcasted_iota(jnp.int32, (nb * c, nb * c), 1)
    same_seq_incl = ((trow // c) == (tcol // c)) & (tcol <= trow)
    cum = _split_dot_lhs_exact(jnp.where(same_seq_incl, 1.0, 0.0).astype(BF16), lw)
    tot = jnp.concatenate([jnp.broadcast_to(cum[(b + 1) * c - 1:(b + 1) * c, :], (c, rw)) for b in range(nb)],
                          axis=0)
    g_in = jnp.exp(cum)
    g_inp = jnp.exp(cum - lw)
    g_out = jnp.exp(-cum)
    g_end = jnp.exp(tot - cum)
    g_tot = jnp.exp(tot)
    a_in = av * g_inp
    r_in = r * g_in
    b_out = bv * g_out
    k_out = kp * g_out
    b_end = bv * g_end
    k_end = kp * g_end

    row = lax.broadcasted_iota(jnp.int32, (gw, gw), 0)
    col = lax.broadcasted_iota(jnp.int32, (gw, gw), 1)
    same_head = (row // n) == (col // n)
    strict = same_head & ((col % n) < (row % n))
    incl = same_head & ((col % n) <= (row % n))

    def expand(x):
        return jnp.where(same_head, jnp.concatenate([x] * RWKV_GROUP, axis=0), 0.0)

    bf = lambda x: x.astype(BF16)
    chains = [(b, g) for b in range(nb) for g in range(groups)]
    pick = lambda x: [expand(x[b * c:(b + 1) * c, g * gw:(g + 1) * gw]) for b, g in chains]
    each = lambda fn, *lists: [fn(*args) for args in zip(*lists)]
    a_bd, r_bd = each(bf, pick(a_in)), each(bf, pick(r_in))
    bo_bd, ko_bd = each(bf, pick(b_out)), each(bf, pick(k_out))
    v_f = pick(v)
    v_bd = each(bf, v_f)
    st = [state_ref[b, g] for b, g in chains]
    st_b = each(bf, st)
    a_ab = each(lambda p, q: jnp.where(strict, _nt_dot(p, q), 0.0), a_bd, bo_bd)
    a_ak = each(lambda p, q: bf(jnp.where(strict, _nt_dot(p, q), 0.0)), a_bd, ko_bd)
    a_rb = each(lambda p, q: bf(jnp.where(incl, _nt_dot(p, q), 0.0)), r_bd, bo_bd)
    a_rk = each(lambda p, q: bf(jnp.where(incl, _nt_dot(p, q), 0.0)), r_bd, ko_bd)
    ph_a = each(_nt_dot, a_bd, st_b)
    ph_r = each(_nt_dot, r_bd, st_b)
    tinv = _unit_lower_inverses(a_ab, row, col, n)
    rhs = each(lambda p, m, w: bf(p + _dot(m, w)), ph_a, a_ak, v_bd)
    u = each(lambda t, x: _dot(bf(t), x), tinv, rhs)
    y_bd = each(lambda p, m1, uu, m2, w: p + _dot(m1, bf(uu)) + _dot(m2, w), ph_r, a_rb, u, a_rk, v_bd)
    be_bd, ke_bd = each(bf, pick(b_end)), each(bf, pick(k_end))
    for (b, g), s0, uu, vf, be, ke in zip(chains, st, u, v_f, be_bd, ke_bd):
        state_ref[b, g] = (s0 * g_tot[b * c:b * c + 1, g * gw:(g + 1) * gw]
                           + _dot(bf(uu.T), be) + _dot(bf(vf.T), ke))
    y_g = [sum(m[i * c:(i + 1) * c] for i in range(RWKV_GROUP)) for m in y_bd]
    y = jnp.concatenate([jnp.concatenate(y_g[b * groups:(b + 1) * groups], axis=1) for b in range(nb)],
                        axis=0)

    inv_n = 1.0 / n
    mu_y = segsum(y, 3) * inv_n
    dy = y - mu_y
    var_y = segsum(dy * dy, 3) * inv_n
    yn = dy * lax.rsqrt(var_y + RWKV_GN_EPS)
    out = (yn * lng_ref[...] + lnb_ref[...] + bonus) * gate
    for b in range(nb):
        o_ref[b] = out[b * c:(b + 1) * c]


def _split_dot_lhs_exact(w_bf16, x):
    acc = None
    for _ in range(3):
        piece = x.astype(BF16)
        term = _dot(w_bf16, piece)
        acc = term if acc is None else acc + term
        x = x - piece.astype(F32)
    return acc


def _rwkv_call(zr, zl, p, batch, seq):
    c = SCAN_CHUNK
    rw = RWKV_WIDTH
    nb = RWKV_BATCH_TILE
    nc = seq // c
    gw = RWKV_GROUP * RWKV_HEAD
    const = lambda shape: pl.BlockSpec(shape, lambda b, i: (0,) * len(shape))
    vec = const((1, rw))
    out = pl.pallas_call(
        _rwkv_kernel,
        grid=(batch // nb, nc),
        in_specs=[
            pl.BlockSpec((nb, c, 3 * rw), lambda b, i: (b, i, 0)),
            pl.BlockSpec((nb, c, rw), lambda b, i: (b, i, 0)),
            const((1, 3 * rw)), vec,
            vec, vec, vec, vec, vec, vec, vec,
            const((LANES, rw)), const((LANES, rw)), const((2 * LANES, rw)),
            const((gw, gw)),
        ],
        out_specs=pl.BlockSpec((nb, c, rw), lambda b, i: (b, i, 0)),
        out_shape=jax.ShapeDtypeStruct((batch, seq, rw), F32),
        scratch_shapes=[
            pltpu.VMEM((nb, c + 8, 3 * rw), F32),
            pltpu.VMEM((nb, c + 8, rw), F32),
            pltpu.VMEM((nb, rw // gw, gw, gw), F32),
        ],
        compiler_params=_cparams(("arbitrary", "arbitrary")),
        name="rwkv7_mix",
    )(zr.reshape(batch, seq, 3 * rw), zl.reshape(batch, seq, rw), p["mu_r"], p["mu_l"], p["w0"], p["a0"],
      p["k_k"], p["k_a"], p["r_k"], p["ln_g"], p["ln_b"], p["w2"], p["a2"], p["g2"], p["seg"])
    return out.reshape(batch * seq, rw)


def _qkv_rope_kernel(x_ref, g_ref, sh_ref, sc_ref, pos_ref, inv_ref, sgn_ref, wq_ref, wk_ref, wv_ref,
                     q_ref, k_ref, v_ref):
    h = _modulate(x_ref[...], g_ref[...], sh_ref[0], sc_ref[0]).astype(BF16)
    ang = pos_ref[...].astype(F32) * inv_ref[...]
    cosf = jnp.cos(ang)
    sinf = jnp.sin(ang) * sgn_ref[...]
    lane = lax.broadcasted_iota(jnp.int32, ang.shape, 1)
    low = (lane % DIFF_HEAD_DIM) < (ROPE_DIM // 2)
    half = ROPE_DIM // 2

    def rope(w_ref, o_ref, scale):
        z = _dot(h, w_ref[...])
        for cb in range(z.shape[1] // LANES):
            zc = z[:, cb * LANES:(cb + 1) * LANES]
            partner = jnp.where(low, pltpu.roll(zc, LANES - half, axis=1), pltpu.roll(zc, half, axis=1))
            o_ref[:, cb * LANES:(cb + 1) * LANES] = ((zc * cosf + partner * sinf) * scale).astype(o_ref.dtype)

    rope(wq_ref, q_ref, DIFF_HEAD_DIM ** -0.5)
    rope(wk_ref, k_ref, 1.0)
    v_ref[...] = _dot(h, wv_ref[...]).astype(v_ref.dtype)


def _qkv_rope_call(x, g, shift, scale, pos, wq, wk, wv, seq, *, tm=512):
    t, d = x.shape
    half = ROPE_DIM // 2
    inv = 1.0 / (ROPE_THETA ** (jnp.arange(0, ROPE_DIM, 2, dtype=F32) / ROPE_DIM))
    dl = jnp.arange(LANES) % DIFF_HEAD_DIM
    inv_pat = jnp.where(dl < ROPE_DIM, inv[dl % half], 0.0).reshape(1, LANES).astype(F32)
    sgn_pat = jnp.where(dl < half, -1.0, 1.0).reshape(1, LANES).astype(F32)
    row = lambda n: pl.BlockSpec((tm, n), lambda i: (i, 0))
    full = lambda a: pl.BlockSpec(a.shape, lambda i: (0, 0))
    mod = pl.BlockSpec((1, 1, d), lambda i: ((i * tm) // seq, 0, 0))
    n = wq.shape[1]
    return pl.pallas_call(
        _qkv_rope_kernel,
        grid=(t // tm,),
        in_specs=[row(d), full(g), mod, mod, row(1), full(inv_pat), full(sgn_pat),
                  full(wq), full(wk), full(wv)],
        out_specs=[row(n), row(n), row(n)],
        out_shape=[jax.ShapeDtypeStruct((t, n), BF16)] * 3,
        compiler_params=_cparams(("arbitrary",)),
        name="qkv_rope",
    )(x, g, shift, scale, pos, inv_pat, sgn_pat, wq, wk, wv)


def _attn_kernel(q_ref, k_ref, v_ref, lq1_ref, lk1_ref, lq2_ref, lk2_ref, sg_ref, o_ref,
                 *, tq, seq, lambda_init):
    i = pl.program_id(2)
    kb = 2 * tq
    q = q_ref[...]
    lane = lax.broadcasted_iota(jnp.int32, q.shape, 1)
    zero = jnp.zeros_like(q)
    qm = (jnp.where(lane < DIFF_HEAD_DIM, q, zero), jnp.where(lane >= DIFF_HEAD_DIM, q, zero))
    f32sum = lambda a, b: jnp.sum(a[...] * b[...], axis=-1, keepdims=True)
    lam = jnp.exp(f32sum(lq1_ref, lk1_ref)) - jnp.exp(f32sum(lq2_ref, lk2_ref)) + lambda_init
    row_chunk = (i * tq + lax.broadcasted_iota(jnp.int32, (tq, kb), 0)) // ATTN_CHUNK
    col_in_block = lax.broadcasted_iota(jnp.int32, (tq, kb), 1)

    def attend(nblk):
        outs = []
        for mp in range(2):
            s = [_nt_dot(qm[mp], k_ref[j * kb:(j + 1) * kb, :]) for j in range(nblk)]
            col_chunk = ((nblk - 1) * kb + col_in_block) // ATTN_CHUNK
            s[-1] = jnp.where(col_chunk <= row_chunk, s[-1], -jnp.inf)
            m = jnp.max(s[0], axis=-1, keepdims=True)
            for sj in s[1:]:
                m = jnp.maximum(m, jnp.max(sj, axis=-1, keepdims=True))
            l = jnp.zeros_like(m)
            acc = jnp.zeros((tq, DIFF_V_DIM), F32)
            for j, sj in enumerate(s):
                p = jnp.exp(sj - m)
                l = l + jnp.sum(p, axis=-1, keepdims=True)
                acc = acc + _dot(p.astype(BF16), v_ref[j * kb:(j + 1) * kb, :])
            outs.append(acc / l)
        o = outs[0] - lam * outs[1]
        ms = jnp.mean(o * o, axis=-1, keepdims=True)
        o = o * lax.rsqrt(ms + NORM_EPS) * sg_ref[...] * (1.0 - lambda_init)
        o_ref[...] = o.astype(o_ref.dtype)

    for nblk in range(1, seq // kb + 1):
        pl.when(i // 2 == nblk - 1)(functools.partial(attend, nblk))


def _attn_call(q, k, v, lq1, lk1, lq2, lk2, subln_g, lambda_init, batch, seq, *, tq=256):
    t = q.shape[0]
    nq = seq // tq
    assert seq % (2 * tq) == 0
    vec = lambda a: a.reshape(1, -1)
    small = lambda a: pl.BlockSpec(a.shape, lambda b, h, i: (0, 0))
    kv = pl.BlockSpec((seq, DIFF_V_DIM), lambda b, h, i: (b, h))
    args = [vec(lq1), vec(lk1), vec(lq2), vec(lk2), vec(subln_g)]
    return pl.pallas_call(
        functools.partial(_attn_kernel, tq=tq, seq=seq, lambda_init=lambda_init),
        grid=(batch, DIFF_HEADS, nq),
        in_specs=[pl.BlockSpec((tq, DIFF_V_DIM), lambda b, h, i: (b * nq + i, h)), kv, kv]
                 + [small(a) for a in args],
        out_specs=pl.BlockSpec((tq, DIFF_V_DIM), lambda b, h, i: (b * nq + i, h)),
        out_shape=jax.ShapeDtypeStruct((t, DIFF_HEADS * DIFF_V_DIM), BF16),
        compiler_params=_cparams(("arbitrary", "arbitrary", "arbitrary")),
        name="diff_attn",
    )(q, k, v, *args)


def _peer_route_kernel(q_ref, sk_ref, r2_ref, e2_ref, n1_ref, f_ref, a_ref, b_ref, cand_ref, *, tt):
    neg = -jnp.inf
    k = PEER_TOPK
    kf = float(k)
    key_id = lax.broadcasted_iota(jnp.int32, (PEER_KEYS, LANES), 0).astype(F32)
    cand_id = lax.broadcasted_iota(jnp.int32, (k * k, LANES), 0).astype(F32)

    def extract(w, ids, dst_ref, by_index, want_rank):
        rank = jnp.full(w.shape, kf, F32) if want_rank else None
        tops = []
        for r in range(k):
            mx = jnp.max(w, axis=0, keepdims=True)
            tops.append(mx)
            if dst_ref is not None:
                dst_ref[r:r + 1, :] = mx
            hit = w == mx
            if by_index:
                first = jnp.min(jnp.where(hit, ids, float(w.shape[0])), axis=0, keepdims=True)
                hit = ids == first
            if want_rank:
                rank = jnp.where(hit, float(r), rank)
            w = jnp.where(hit, neg, w)
        return rank, tops, w

    def candidates():
        bvals = b_ref[...]
        for x in range(k):
            cand_ref[x * k:(x + 1) * k, :] = a_ref[x:x + 1, :] + bvals
        return cand_ref[...]

    def staircase_candidates():
        a = lambda lo, hi: a_ref[lo:hi, :]
        b = lambda lo, hi: b_ref[lo:hi, :]
        return jnp.concatenate([
            a(0, 1) + b(0, 8), a(0, 1) + b(8, 16), a(1, 2) + b(0, 8), a(2, 3) + b(0, 8), a(3, 4) + b(0, 8),
            a(0, 8) + b(0, 1), a(8, 16) + b(0, 1), a(0, 8) + b(1, 2), a(0, 8) + b(2, 3)], axis=0)

    def twin_bf16(v):
        bits = lax.bitcast_convert_type(v.astype(BF16).astype(F32), jnp.uint32)
        return bits | (bits >> 16)

    def emit(cols, s1, s2, rank2, n1, a0, b0, c_tops):
        zsum = jnp.zeros((1, LANES), F32)
        for c in c_tops:
            zsum = zsum + jnp.exp(c - c_tops[0])
        r2_ref[0, 0, :, cols] = pltpu.bitcast(rank2.astype(BF16), jnp.uint32)
        e2_ref[0, 0, :, cols] = pltpu.bitcast(jnp.exp(s2 - b0).astype(BF16), jnp.uint32)
        n1_ref[0, 0, :, cols] = twin_bf16(n1)
        f_ref[0, 0, :, cols] = twin_bf16(jnp.exp(s1 - a0) / zsum)

    count = lambda m: jnp.sum(jnp.where(m, 1.0, 0.0), axis=0, keepdims=True)

    for gidx in range(tt // LANES):
        cols = slice(gidx * LANES, (gidx + 1) * LANES)
        s1 = _nt_dot(sk_ref[0], q_ref[cols, 0:PEER_HALF], HIGHEST)
        s2 = _nt_dot(sk_ref[1], q_ref[cols, PEER_HALF:2 * PEER_HALF], HIGHEST)

        _, a_tops, w1 = extract(s1, key_id, a_ref, False, False)
        rank2, b_tops, _ = extract(s2, key_id, b_ref, False, True)
        _, c_tops, _ = extract(staircase_candidates(), None, None, False, False)
        tau = c_tops[k - 1]
        n1 = jnp.zeros_like(s1)
        for y in range(k):
            n1 = n1 + jnp.where((s1 + b_tops[y]) >= tau, 1.0, 0.0)
        emit(cols, s1, s2, rank2, n1, a_tops[0], b_tops[0], c_tops)
        tied = ((count(w1 == neg) != kf) | (count(rank2 < kf) != kf)
                | (jnp.sum(n1, axis=0, keepdims=True) != kf))

        @pl.when(jnp.max(jnp.where(tied, 1.0, 0.0)) > 0.0)
        def _():
            rank1, a_x, _ = extract(s1, key_id, a_ref, True, True)
            rank2x, b_x, _ = extract(s2, key_id, b_ref, True, True)
            rank3, c_x, _ = extract(candidates(), cand_id, None, True, True)
            win = rank3 < kf
            n1x = jnp.zeros_like(s1)
            for x in range(k):
                n_x = count(win[x * k:(x + 1) * k])
                n1x = n1x + jnp.where(rank1 == float(x), n_x, 0.0)
            emit(cols, s1, s2, rank2x, n1x, a_x[0], b_x[0], c_x)


def _peer_route_call(q, subkeys, *, tt):
    t = q.shape[0]
    nt = t // tt
    spec = pl.BlockSpec((1, 1, PEER_KEYS, tt), lambda i, h: (h, i, 0, 0))
    pair_spec = pl.BlockSpec((1, 1, PEER_KEYS // 2, tt), lambda i, h: (h, i, 0, 0))
    tab = lambda rows: jax.ShapeDtypeStruct((PEER_HEADS, nt, rows, tt), jnp.uint32)
    return pl.pallas_call(
        functools.partial(_peer_route_kernel, tt=tt),
        grid=(nt, PEER_HEADS),
        in_specs=[
            pl.BlockSpec((tt, 2 * PEER_HALF), lambda i, h: (i, h)),
            pl.BlockSpec(subkeys.shape, lambda i, h: (0, 0, 0)),
        ],
        out_specs=[pair_spec, pair_spec, spec, spec],
        out_shape=[tab(PEER_KEYS // 2), tab(PEER_KEYS // 2), tab(PEER_KEYS), tab(PEER_KEYS)],
        scratch_shapes=[
            pltpu.VMEM((PEER_TOPK, LANES), F32),
            pltpu.VMEM((PEER_TOPK, LANES), F32),
            pltpu.VMEM((PEER_TOPK * PEER_TOPK, LANES), F32),
        ],
        compiler_params=_cparams(("arbitrary", "arbitrary")),
        name="peer_route",
    )(q, subkeys)


def _gelu(x):
    return 0.5 * x * (1.0 + lax.erf(x * (1.0 / math.sqrt(2.0))))


def _peer_expert_kernel(x_ref, gate_ref, ht_ref, u_ref, vt_ref, n1_ref, f_ref, r2_ref, e2_ref,
                        o_ref, pre_a, pre_b, wa_ref, acc_ref, *, rows_per_tile):
    s = pl.program_id(1)
    te, tt = pre_a.shape
    d = acc_ref.shape[0]
    blk = 16
    rows_per_piece = 2

    @pl.when(s == 0)
    def _():
        acc_ref[...] = jnp.zeros_like(acc_ref)
        pre_b[...] = jnp.zeros_like(pre_b)

    def step(pre_new, pre_old):
        halves = [slice(m * d // 2, (m + 1) * d // 2) for m in range(2)]
        n_pre = 2
        for m in range(n_pre):
            rows = slice(m * te // n_pre, (m + 1) * te // n_pre)
            pre_new[rows, :] = _dot(u_ref[rows, :], ht_ref[...])
        zero = jnp.zeros((blk, LANES), BF16)
        as_bf16 = lambda words: pltpu.bitcast(words, BF16)
        for il in range(rows_per_tile):
            for lt in range(tt // LANES):
                lanes = slice(lt * LANES, (lt + 1) * LANES)
                row = lambda ref, h: as_bf16(jnp.broadcast_to(ref[h, 0, il:il + 1, lanes], (blk // 2, LANES)))
                n1b = [row(n1_ref, h) for h in range(PEER_HEADS)]
                fb = [row(f_ref, h) for h in range(PEER_HEADS)]
                for jb in range(PEER_KEYS // blk):
                    words = slice(jb * blk // 2, (jb + 1) * blk // 2)
                    w = zero
                    for h in range(PEER_HEADS):
                        hit = as_bf16(r2_ref[h, 0, words, lanes]) < n1b[h]
                        w = w + jnp.where(hit, as_bf16(e2_ref[h, 0, words, lanes]), zero) * fb[h]
                    rows = slice(il * PEER_KEYS + jb * blk, il * PEER_KEYS + (jb + 1) * blk)
                    wa_ref[rows, lanes] = w * _gelu(pre_old[rows, lanes]).astype(BF16)
        for rows in halves:
            acc_ref[rows, :] += _dot(vt_ref[rows, :], wa_ref[...])

    @pl.when(s % 2 == 0)
    def _():
        step(pre_a, pre_b)

    @pl.when(s % 2 == 1)
    def _():
        step(pre_b, pre_a)

    @pl.when(s == pl.num_programs(1) - 1)
    def _():
        o_ref[...] = x_ref[...] + gate_ref[0] * acc_ref[...].T


def _peer_expert_call(x, gate, hb, u_b, vt_b, n1, f, r2, e2, seq, *, tt, te=1024):
    t, d = x.shape
    ne = u_b.shape[0] // te
    rows_per_tile = te // PEER_KEYS
    lag = lambda s, n: jnp.clip(s - n, 0, ne - 1)
    sel = pl.BlockSpec((PEER_HEADS, 1, rows_per_tile, tt), lambda i, s: (0, i, lag(s, 1), 0))
    full = pl.BlockSpec((PEER_HEADS, 1, PEER_KEYS // 2, tt), lambda i, s: (0, i, 0, 0))
    return pl.pallas_call(
        functools.partial(_peer_expert_kernel, rows_per_tile=rows_per_tile),
        grid=(t // tt, ne + 1),
        in_specs=[
            pl.BlockSpec((tt, d), lambda i, s: (i, 0)),
            pl.BlockSpec((1, 1, d), lambda i, s: ((i * tt) // seq, 0, 0)),
            pl.BlockSpec((d, tt), lambda i, s: (0, i)),
            pl.BlockSpec((te, d), lambda i, s: (lag(s, 0), 0)),
            pl.BlockSpec((d, te), lambda i, s: (0, lag(s, 1))),
            sel, sel, full, full,
        ],
        out_specs=pl.BlockSpec((tt, d), lambda i, s: (i, 0)),
        out_shape=jax.ShapeDtypeStruct((t, d), F32),
        scratch_shapes=[
            pltpu.VMEM((te, tt), F32),
            pltpu.VMEM((te, tt), F32),
            pltpu.VMEM((te, tt), BF16),
            pltpu.VMEM((d, tt), F32),
        ],
        compiler_params=_cparams(("arbitrary", "arbitrary")),
        name="peer_experts",
    )(x, gate, hb, u_b, vt_b, n1, f, r2, e2)


def _final_norm_kernel(x_ref, g_ref, o_ref):
    x = x_ref[...]
    ms = jnp.mean(x * x, axis=-1, keepdims=True)
    o_ref[...] = x * lax.rsqrt(ms + NORM_EPS) * g_ref[...]


def _final_norm_call(x, g, *, tm=512):
    t, d = x.shape
    return pl.pallas_call(
        _final_norm_kernel,
        grid=(t // tm,),
        in_specs=[pl.BlockSpec((tm, d), lambda i: (i, 0)), pl.BlockSpec((1, d), lambda i: (0, 0))],
        out_specs=pl.BlockSpec((tm, d), lambda i: (i, 0)),
        out_shape=jax.ShapeDtypeStruct((t, d), F32),
        compiler_params=_cparams(("arbitrary",)),
        name="final_norm",
    )(x, g.reshape(1, d))


def _pad_rows(a, rows):
    return jnp.pad(a, ((0, rows - a.shape[0]), (0, 0)))


def _pad_cols(a, cols):
    return jnp.pad(a, ((0, 0), (0, cols - a.shape[1])))


def _hybrid_mixer(x, g, shift, scale, batch, seq, w_in, conv_w, conv_b, conv_ln_g, conv_ln_b, mu, w0, w2,
                  a0, a2, g2, k_k, k_a, r_k, ln_g, ln_b):
    rw = RWKV_WIDTH
    c0 = 2 * CONV_WIDTH
    lora0 = c0 + 3 * rw
    l1, l2 = lora0 + DECAY_LORA, lora0 + DECAY_LORA + AAA_LORA
    w_conv = w_in[:, :c0].astype(BF16)
    w_rkv = w_in[:, c0:lora0].astype(BF16)
    w_lora = jnp.concatenate([
        _pad_cols(w_in[:, lora0:l1], LANES), _pad_cols(w_in[:, l1:l2], LANES),
        _pad_cols(w_in[:, l2:], 2 * LANES)], axis=1).astype(BF16)
    zc, zr, zl = _norm_matmul_call(x, g, shift, scale, [w_conv, w_rkv, w_lora], [F32, F32, F32], seq,
                                   name="hybrid_in_proj")
    ya = _conv_call(zc, conv_w, conv_b, conv_ln_g, conv_ln_b, batch, seq)
    mu_l = jnp.concatenate([
        _pad_cols(mu[None, lora0 - c0:l1 - c0], LANES), _pad_cols(mu[None, l1 - c0:l2 - c0], LANES),
        _pad_cols(mu[None, l2 - c0:], 2 * LANES)], axis=1)
    vec = lambda a: a.reshape(1, rw)
    head_id = jnp.arange(RWKV_GROUP * RWKV_HEAD) // RWKV_HEAD
    params = dict(
        mu_r=mu[None, :3 * rw], mu_l=mu_l, w0=vec(w0), a0=vec(a0), k_k=vec(k_k), k_a=vec(k_a),
        r_k=vec(r_k), ln_g=vec(ln_g), ln_b=vec(ln_b),
        w2=_pad_rows(w2, LANES).astype(BF16), a2=_pad_rows(a2, LANES).astype(BF16),
        g2=_pad_rows(g2, 2 * LANES).astype(BF16),
        seg=(head_id[:, None] == head_id[None, :]).astype(BF16))
    yb = _rwkv_call(zr, zl, params, batch, seq)
    return ya, yb


def kernel(x, c, positions, ada_w, ada_b, norm_mix_g, norm_ffn_g, hyb_w_in, conv_w, conv_b, conv_ln_g, conv_ln_b, rwkv_mu, rwkv_w0, rwkv_w2, rwkv_a0, rwkv_a2, rwkv_g2, rwkv_k_k, rwkv_k_a, rwkv_r_k, rwkv_ln_g, rwkv_ln_b, hyb_w_out, diff_w_qkv, diff_lq1, diff_lk1, diff_lq2, diff_lk2, diff_subln_g, diff_w_out, peer_w_q, peer_subkeys, peer_u, peer_v, final_g):
    batch, seq, d = x.shape
    depth = ada_w.shape[0]
    t = batch * seq
    xt = x.reshape(t, d)
    pos = positions.reshape(t, 1)
    mod = _ada_call(c, ada_w, ada_b).reshape(depth, batch, 6, 1, d)

    for l in range(depth):
        sh1, sc1, g1, sh2, sc2, g2 = (mod[l, :, j] for j in range(6))
        gmix = norm_mix_g[l].reshape(1, d)
        if l % 2 == 0:
            e = l // 2
            ya, yb = _hybrid_mixer(
                xt, gmix, sh1, sc1, batch, seq, hyb_w_in[e], conv_w[e], conv_b[e], conv_ln_g[e],
                conv_ln_b[e], rwkv_mu[e], rwkv_w0[e], rwkv_w2[e], rwkv_a0[e], rwkv_a2[e], rwkv_g2[e],
                rwkv_k_k[e], rwkv_k_a[e], rwkv_r_k[e], rwkv_ln_g[e], rwkv_ln_b[e])
            w_out = hyb_w_out[e].astype(BF16)
            xt = _out_proj_call(xt, g1, [ya.astype(BF16), yb.astype(BF16)],
                                [w_out[:CONV_WIDTH], w_out[CONV_WIDTH:]], seq)
        else:
            o = l // 2
            lambda_init = 0.8 - 0.6 * math.exp(-0.3 * l)
            wqkv = diff_w_qkv[o].astype(BF16)
            n = wqkv.shape[1] // 3
            q, k, v = _qkv_rope_call(xt, gmix, sh1, sc1, pos, wqkv[:, :n], wqkv[:, n:2 * n], wqkv[:, 2 * n:],
                                     seq)
            y = _attn_call(q, k, v, diff_lq1[o], diff_lk1[o], diff_lq2[o], diff_lk2[o], diff_subln_g[o],
                           lambda_init, batch, seq)
            xt = _out_proj_call(xt, g1, [y], [diff_w_out[o].astype(BF16)], seq)

        q, hb = _norm_matmul_call(xt, norm_ffn_g[l].reshape(1, d), sh2, sc2, [peer_w_q[l]], [F32], seq,
                                  split=True, emit_h=True, name="peer_query")
        r2, e2, n1, f = _peer_route_call(q, peer_subkeys[l], tt=PEER_TOKEN_TILE)
        xt = _peer_expert_call(xt, g2, hb, peer_u[l].astype(BF16), peer_v[l].T.astype(BF16),
                               n1, f, r2, e2, seq, tt=PEER_TOKEN_TILE)

    return _final_norm_call(xt, final_g).reshape(batch, seq, d)
```

```python
import functools
import math

import jax
import jax.numpy as jnp
from jax import lax
from jax.experimental import pallas as pl
from jax.experimental.pallas import tpu as pltpu

F32 = jnp.float32
BF16 = jnp.bfloat16
HIGHEST = lax.Precision.HIGHEST

CONV_WIDTH = 512
CONV_KERNEL = 31
RWKV_WIDTH = 512
RWKV_HEAD = 64
RWKV_HEADS = 8
DECAY_LORA = 64
AAA_LORA = 64
GATE_LORA = 160
DIFF_HEADS = 8
DIFF_HEAD_DIM = 64
DIFF_V_DIM = 128
ROPE_DIM = 16
ROPE_THETA = 500000.0
ATTN_CHUNK = 64
PEER_HEADS = 8
PEER_KEYS = 128
PEER_HALF = 128
PEER_TOPK = 16
NORM_EPS = 1e-6
LN_EPS = 1e-5
RWKV_GN_EPS = 64e-5

LANES = 128
SCAN_CHUNK = 64
CONV_HALO = 32
PEER_TOKEN_TILE = 256
VMEM_LIMIT = 48 * 1024 * 1024


def _cparams(semantics, flags=None):
    return pltpu.CompilerParams(dimension_semantics=semantics, vmem_limit_bytes=VMEM_LIMIT, flags=flags)


def _nt_dot(a, b, precision=None):
    return lax.dot_general(a, b, (((1,), (1,)), ((), ())), precision=precision,
                           preferred_element_type=F32)


def _dot(a, b, precision=None):
    return jnp.dot(a, b, precision=precision, preferred_element_type=F32)


def _sigmoid(x):
    return 1.0 / (1.0 + jnp.exp(-x))


def _ada_kernel(c_ref, w_ref, b_ref, o_ref):
    c = c_ref[...]
    cond = c * _sigmoid(c)
    o_ref[0] = _dot(cond, w_ref[0], HIGHEST) + b_ref[0]


def _ada_call(c, ada_w, ada_b):
    depth, d, n = ada_w.shape
    b = c.shape[0]
    tn = 1536
    return pl.pallas_call(
        _ada_kernel,
        grid=(depth, n // tn),
        in_specs=[
            pl.BlockSpec((b, d), lambda l, j: (0, 0)),
            pl.BlockSpec((1, d, tn), lambda l, j: (l, 0, j)),
            pl.BlockSpec((1, 1, tn), lambda l, j: (l, 0, j)),
        ],
        out_specs=pl.BlockSpec((1, b, tn), lambda l, j: (l, 0, j)),
        out_shape=jax.ShapeDtypeStruct((depth, b, n), F32),
        compiler_params=_cparams(("arbitrary", "arbitrary")),
        name="ada_mod",
    )(c, ada_w, ada_b.reshape(depth, 1, n))


def _modulate(x, g, shift, scale):
    ms = jnp.mean(x * x, axis=-1, keepdims=True)
    y = x * lax.rsqrt(ms + NORM_EPS)
    return (y * g) * (1.0 + scale) + shift


def _split_bf16(a):
    hi = a.astype(BF16)
    return hi, (a - hi.astype(F32)).astype(BF16)


def _norm_matmul_kernel(*refs, n_w, split, emit_h):
    x_ref, g_ref, sh_ref, sc_ref = refs[:4]
    w_refs = refs[4:4 + n_w]
    o_refs = refs[4 + n_w:]
    h = _modulate(x_ref[...], g_ref[...], sh_ref[0], sc_ref[0])
    hb = h.astype(BF16)
    if split:
        h_lo = (h - hb.astype(F32)).astype(BF16)
        for k in range(n_w // 2):
            w_hi, w_lo = w_refs[2 * k][...], w_refs[2 * k + 1][...]
            o_refs[k][...] = (_dot(hb, w_hi) + _dot(h_lo, w_hi) + _dot(hb, w_lo)).astype(o_refs[k].dtype)
    else:
        for w_ref, o_ref in zip(w_refs, o_refs[:n_w]):
            o_ref[...] = _dot(hb, w_ref[...]).astype(o_ref.dtype)
    if emit_h:
        o_refs[-1][...] = h.T.astype(BF16)


def _norm_matmul_call(x, g, shift, scale, weights, out_dtypes, seq, *, split=False, emit_h=False,
                      tm=512, name="norm_matmul"):
    t, d = x.shape
    if split:
        weights = [part for w in weights for part in _split_bf16(w)]
    in_specs = [
        pl.BlockSpec((tm, d), lambda i: (i, 0)),
        pl.BlockSpec((1, d), lambda i: (0, 0)),
        pl.BlockSpec((1, 1, d), lambda i: ((i * tm) // seq, 0, 0)),
        pl.BlockSpec((1, 1, d), lambda i: ((i * tm) // seq, 0, 0)),
    ]
    out_specs, out_shape = [], []
    for w in weights:
        in_specs.append(pl.BlockSpec(w.shape, lambda i: (0, 0)))
    for w, dt in zip(weights[::2] if split else weights, out_dtypes):
        n = w.shape[1]
        out_specs.append(pl.BlockSpec((tm, n), lambda i: (i, 0)))
        out_shape.append(jax.ShapeDtypeStruct((t, n), dt))
    if emit_h:
        out_specs.append(pl.BlockSpec((d, tm), lambda i: (0, i)))
        out_shape.append(jax.ShapeDtypeStruct((d, t), BF16))
    return pl.pallas_call(
        functools.partial(_norm_matmul_kernel, n_w=len(weights), split=split, emit_h=emit_h),
        grid=(t // tm,),
        in_specs=in_specs,
        out_specs=out_specs,
        out_shape=out_shape,
        compiler_params=_cparams(("arbitrary",)),
        name=name,
    )(x, g, shift, scale, *weights)


def _out_proj_kernel(*refs, n_y):
    x_ref, gate_ref = refs[:2]
    y_refs = refs[2:2 + n_y]
    w_refs = refs[2 + n_y:2 + 2 * n_y]
    o_ref = refs[2 + 2 * n_y]
    acc = _dot(y_refs[0][...], w_refs[0][...])
    for y_ref, w_ref in zip(y_refs[1:], w_refs[1:]):
        acc = acc + _dot(y_ref[...], w_ref[...])
    o_ref[...] = x_ref[...] + gate_ref[0] * acc


def _out_proj_call(x, gate, ys, ws, seq, *, tm=512):
    t, d = x.shape
    in_specs = [
        pl.BlockSpec((tm, d), lambda i: (i, 0)),
        pl.BlockSpec((1, 1, d), lambda i: ((i * tm) // seq, 0, 0)),
    ]
    for y in ys:
        in_specs.append(pl.BlockSpec((tm, y.shape[1]), lambda i: (i, 0)))
    for w in ws:
        in_specs.append(pl.BlockSpec(w.shape, lambda i: (0, 0)))
    return pl.pallas_call(
        functools.partial(_out_proj_kernel, n_y=len(ys)),
        grid=(t // tm,),
        in_specs=in_specs,
        out_specs=pl.BlockSpec((tm, d), lambda i: (i, 0)),
        out_shape=jax.ShapeDtypeStruct((t, d), F32),
        compiler_params=_cparams(("arbitrary",)),
        name="out_proj",
    )(x, gate, *ys, *ws)


def _conv_kernel(z_ref, w_ref, b_ref, g_ref, beta_ref, o_ref, ext_ref, *, ts):
    width = CONV_WIDTH

    @pl.when(pl.program_id(1) == 0)
    def _():
        ext_ref[0:CONV_HALO, :] = jnp.zeros((CONV_HALO, width), F32)

    z = z_ref[...]
    u = z[:, :width] * _sigmoid(z[:, width:])
    ext_ref[CONV_HALO:CONV_HALO + ts, :] = u
    base = CONV_HALO - (CONV_KERNEL - 1)
    rows = 64
    for cb in range(width // LANES):
        cs = slice(cb * LANES, (cb + 1) * LANES)
        for rb in range(ts // rows):
            acc = jnp.zeros((rows, LANES), F32)
            for j in range(CONV_KERNEL):
                start = rb * rows + base + j
                acc = acc + w_ref[j:j + 1, cs] * ext_ref[start:start + rows, cs]
            o_ref[rb * rows:(rb + 1) * rows, cs] = acc
    conv = o_ref[...] + b_ref[...]
    mu = jnp.mean(conv, axis=-1, keepdims=True)
    dlt = conv - mu
    var = jnp.mean(dlt * dlt, axis=-1, keepdims=True)
    y = dlt * lax.rsqrt(var + LN_EPS) * g_ref[...] + beta_ref[...]
    o_ref[...] = y * _sigmoid(y)
    ext_ref[0:CONV_HALO, :] = ext_ref[ts:ts + CONV_HALO, :]


def _conv_call(z, conv_w, conv_b, ln_g, ln_b, batch, seq, *, ts=256):
    t = z.shape[0]
    width = CONV_WIDTH
    nts = seq // ts
    vec = lambda a: a.reshape(1, width)
    return pl.pallas_call(
        functools.partial(_conv_kernel, ts=ts),
        grid=(batch, nts),
        in_specs=[
            pl.BlockSpec((ts, 2 * width), lambda b, i: (b * nts + i, 0)),
            pl.BlockSpec((CONV_KERNEL, width), lambda b, i: (0, 0)),
            pl.BlockSpec((1, width), lambda b, i: (0, 0)),
            pl.BlockSpec((1, width), lambda b, i: (0, 0)),
            pl.BlockSpec((1, width), lambda b, i: (0, 0)),
        ],
        out_specs=pl.BlockSpec((ts, width), lambda b, i: (b * nts + i, 0)),
        out_shape=jax.ShapeDtypeStruct((t, width), F32),
        scratch_shapes=[pltpu.VMEM((ts + CONV_HALO, width), F32)],
        compiler_params=_cparams(("arbitrary", "arbitrary")),
        name="conformer_conv",
    )(z, conv_w, vec(conv_b), vec(ln_g), vec(ln_b))


def _softplus(x):
    return jnp.maximum(x, 0.0) + jnp.log(1.0 + jnp.exp(-jnp.abs(x)))


def _unit_lower_inverses(mats, row, col, size):
    eye = (row == col).astype(F32)
    blk = lambda m: (row // m) == (col // m)
    bdot = lambda p, q: _dot(p.astype(BF16), q.astype(BF16))
    n1 = [jnp.where(blk(8), a, 0.0) for a in mats]
    t = [eye + x for x in n1]
    n2 = [bdot(x, x) for x in n1]
    t = [x + bdot(x, y) for x, y in zip(t, n2)]
    n4 = [bdot(x, x) for x in n2]
    t = [x + bdot(x, y) for x, y in zip(t, n4)]
    m = 8
    while m < size:
        new = blk(2 * m) & jnp.logical_not(blk(m))
        et = [bdot(jnp.where(new, a, 0.0), x) for a, x in zip(mats, t)]
        t = [x + bdot(x, y) for x, y in zip(t, et)]
        m *= 2
    return t


def _split_dot(x, w_bf16, parts):
    acc = None
    for _ in range(parts):
        piece = x.astype(BF16)
        term = _dot(piece, w_bf16)
        acc = term if acc is None else acc + term
        x = x - piece.astype(F32)
    return acc


RWKV_GROUP = 4
RWKV_BATCH_TILE = 2


def _rwkv_kernel(zr_ref, zl_ref, mur_ref, mul_ref, w0_ref, a0_ref, kk_ref, ka_ref, rk_ref,
                 lng_ref, lnb_ref, w2_ref, a2_ref, g2_ref, seg_ref, o_ref,
                 extr_ref, extl_ref, state_ref):
    c = SCAN_CHUNK
    rw = RWKV_WIDTH
    n = RWKV_HEAD
    nb = RWKV_BATCH_TILE
    gw = RWKV_GROUP * n
    groups = rw // gw

    @pl.when(pl.program_id(1) == 0)
    def _():
        extr_ref[:, 0:8, :] = jnp.zeros((nb, 8, 3 * rw), F32)
        extl_ref[:, 0:8, :] = jnp.zeros((nb, 8, rw), F32)
        state_ref[...] = jnp.zeros_like(state_ref)

    zr_rows, zl_rows = [], []
    for b in range(nb):
        zr = zr_ref[b]
        zl = zl_ref[b]
        extr_ref[b, 8:8 + c, :] = zr
        extl_ref[b, 8:8 + c, :] = zl
        zr_rows.append(zr + (extr_ref[b, 7:7 + c, :] - zr) * mur_ref[...])
        zl_rows.append(zl + (extl_ref[b, 7:7 + c, :] - zl) * mul_ref[...])
        extr_ref[b, 0:8, :] = extr_ref[b, c:c + 8, :]
        extl_ref[b, 0:8, :] = extl_ref[b, c:c + 8, :]
    zr = jnp.concatenate(zr_rows, axis=0)
    zl = jnp.concatenate(zl_rows, axis=0)

    r = zr[:, 0:rw]
    k = zr[:, rw:2 * rw]
    v = zr[:, 2 * rw:3 * rw]
    wd = zl[:, 0:LANES]
    ad = zl[:, LANES:2 * LANES]
    gd = zl[:, 2 * LANES:4 * LANES]
    seg = seg_ref[...]

    def segsum(x, parts):
        return jnp.concatenate([_split_dot(x[:, g * gw:(g + 1) * gw], seg, parts) for g in range(groups)],
                               axis=1)

    w_raw = -_softplus(-(w0_ref[...] + _dot(jnp.tanh(wd).astype(BF16), w2_ref[...]))) - 0.5
    lw = -jnp.exp(w_raw)
    alpha = _sigmoid(a0_ref[...] + _dot(ad.astype(BF16), a2_ref[...]))
    gate = _dot(_sigmoid(gd).astype(BF16), g2_ref[...])
    kkf = k * kk_ref[...]
    kk = kkf / jnp.maximum(jnp.sqrt(segsum(kkf * kkf, 3)), 1e-12)
    kp = k * (1.0 + (alpha - 1.0) * ka_ref[...])
    bonus = segsum(r * kp * rk_ref[...], 3) * v
    av = -kk
    bv = kk * alpha

    trow = lax.broadcasted_iota(jnp.int32, (nb * c, nb * c), 0)
    tcol = lax.broadcasted_iota(jnp.int32, (nb * c, nb * c), 1)
    same_seq_incl = ((trow // c) == (tcol // c)) & (tcol <= trow)
    cum = _split_dot_lhs_exact(jnp.where(same_seq_incl, 1.0, 0.0).astype(BF16), lw)
    tot = jnp.concatenate([jnp.broadcast_to(cum[(b + 1) * c - 1:(b + 1) * c, :], (c, rw)) for b in range(nb)],
                          axis=0)
    g_in = jnp.exp(cum)
    g_inp = jnp.exp(cum - lw)
    g_out = jnp.exp(-cum)
    g_end = jnp.exp(tot - cum)
    g_tot = jnp.exp(tot)
    a_in = av * g_inp
    r_in = r * g_in
    b_out = bv * g_out
    k_out = kp * g_out
    b_end = bv * g_end
    k_end = kp * g_end

    row = lax.broadcasted_iota(jnp.int32, (gw, gw), 0)
    col = lax.broadcasted_iota(jnp.int32, (gw, gw), 1)
    same_head = (row // n) == (col // n)
    strict = same_head & ((col % n) < (row % n))
    incl = same_head & ((col % n) <= (row % n))

    def expand(x):
        return jnp.where(same_head, jnp.concatenate([x] * RWKV_GROUP, axis=0), 0.0)

    bf = lambda x: x.astype(BF16)
    chains = [(b, g) for b in range(nb) for g in range(groups)]
    pick = lambda x: [expand(x[b * c:(b + 1) * c, g * gw:(g + 1) * gw]) for b, g in chains]
    each = lambda fn, *lists: [fn(*args) for args in zip(*lists)]
    a_bd, r_bd = each(bf, pick(a_in)), each(bf, pick(r_in))
    bo_bd, ko_bd = each(bf, pick(b_out)), each(bf, pick(k_out))
    v_f = pick(v)
    v_bd = each(bf, v_f)
    st = [state_ref[b, g] for b, g in chains]
    st_b = each(bf, st)
    a_ab = each(lambda p, q: jnp.where(strict, _nt_dot(p, q), 0.0), a_bd, bo_bd)
    a_ak = each(lambda p, q: bf(jnp.where(strict, _nt_dot(p, q), 0.0)), a_bd, ko_bd)
    a_rb = each(lambda p, q: bf(jnp.where(incl, _nt_dot(p, q), 0.0)), r_bd, bo_bd)
    a_rk = each(lambda p, q: bf(jnp.where(incl, _nt_dot(p, q), 0.0)), r_bd, ko_bd)
    ph_a = each(_nt_dot, a_bd, st_b)
    ph_r = each(_nt_dot, r_bd, st_b)
    tinv = _unit_lower_inverses(a_ab, row, col, n)
    rhs = each(lambda p, m, w: bf(p + _dot(m, w)), ph_a, a_ak, v_bd)
    u = each(lambda t, x: _dot(bf(t), x), tinv, rhs)
    y_bd = each(lambda p, m1, uu, m2, w: p + _dot(m1, bf(uu)) + _dot(m2, w), ph_r, a_rb, u, a_rk, v_bd)
    be_bd, ke_bd = each(bf, pick(b_end)), each(bf, pick(k_end))
    for (b, g), s0, uu, vf, be, ke in zip(chains, st, u, v_f, be_bd, ke_bd):
        state_ref[b, g] = (s0 * g_tot[b * c:b * c + 1, g * gw:(g + 1) * gw]
                           + _dot(bf(uu.T), be) + _dot(bf(vf.T), ke))
    y_g = [sum(m[i * c:(i + 1) * c] for i in range(RWKV_GROUP)) for m in y_bd]
    y = jnp.concatenate([jnp.concatenate(y_g[b * groups:(b + 1) * groups], axis=1) for b in range(nb)],
                        axis=0)

    inv_n = 1.0 / n
    mu_y = segsum(y, 3) * inv_n
    dy = y - mu_y
    var_y = segsum(dy * dy, 3) * inv_n
    yn = dy * lax.rsqrt(var_y + RWKV_GN_EPS)
    out = (yn * lng_ref[...] + lnb_ref[...] + bonus) * gate
    for b in range(nb):
        o_ref[b] = out[b * c:(b + 1) * c]


def _split_dot_lhs_exact(w_bf16, x):
    acc = None
    for _ in range(3):
        piece = x.astype(BF16)
        term = _dot(w_bf16, piece)
        acc = term if acc is None else acc + term
        x = x - piece.astype(F32)
    return acc


def _rwkv_call(zr, zl, p, batch, seq):
    c = SCAN_CHUNK
    rw = RWKV_WIDTH
    nb = RWKV_BATCH_TILE
    nc = seq // c
    gw = RWKV_GROUP * RWKV_HEAD
    const = lambda shape: pl.BlockSpec(shape, lambda b, i: (0,) * len(shape))
    vec = const((1, rw))
    out = pl.pallas_call(
        _rwkv_kernel,
        grid=(batch // nb, nc),
        in_specs=[
            pl.BlockSpec((nb, c, 3 * rw), lambda b, i: (b, i, 0)),
            pl.BlockSpec((nb, c, rw), lambda b, i: (b, i, 0)),
            const((1, 3 * rw)), vec,
            vec, vec, vec, vec, vec, vec, vec,
            const((LANES, rw)), const((LANES, rw)), const((2 * LANES, rw)),
            const((gw, gw)),
        ],
        out_specs=pl.BlockSpec((nb, c, rw), lambda b, i: (b, i, 0)),
        out_shape=jax.ShapeDtypeStruct((batch, seq, rw), F32),
        scratch_shapes=[
            pltpu.VMEM((nb, c + 8, 3 * rw), F32),
            pltpu.VMEM((nb, c + 8, rw), F32),
            pltpu.VMEM((nb, rw // gw, gw, gw), F32),
        ],
        compiler_params=_cparams(("arbitrary", "arbitrary")),
        name="rwkv7_mix",
    )(zr.reshape(batch, seq, 3 * rw), zl.reshape(batch, seq, rw), p["mu_r"], p["mu_l"], p["w0"], p["a0"],
      p["k_k"], p["k_a"], p["r_k"], p["ln_g"], p["ln_b"], p["w2"], p["a2"], p["g2"], p["seg"])
    return out.reshape(batch * seq, rw)


def _qkv_rope_kernel(x_ref, g_ref, sh_ref, sc_ref, pos_ref, inv_ref, sgn_ref, wq_ref, wk_ref, wv_ref,
                     q_ref, k_ref, v_ref):
    h = _modulate(x_ref[...], g_ref[...], sh_ref[0], sc_ref[0]).astype(BF16)
    ang = pos_ref[...].astype(F32) * inv_ref[...]
    cosf = jnp.cos(ang)
    sinf = jnp.sin(ang) * sgn_ref[...]
    lane = lax.broadcasted_iota(jnp.int32, ang.shape, 1)
    low = (lane % DIFF_HEAD_DIM) < (ROPE_DIM // 2)
    half = ROPE_DIM // 2

    def rope(w_ref, o_ref, scale):
        z = _dot(h, w_ref[...])
        for cb in range(z.shape[1] // LANES):
            zc = z[:, cb * LANES:(cb + 1) * LANES]
            partner = jnp.where(low, pltpu.roll(zc, LANES - half, axis=1), pltpu.roll(zc, half, axis=1))
            o_ref[:, cb * LANES:(cb + 1) * LANES] = ((zc * cosf + partner * sinf) * scale).astype(o_ref.dtype)

    rope(wq_ref, q_ref, DIFF_HEAD_DIM ** -0.5)
    rope(wk_ref, k_ref, 1.0)
    v_ref[...] = _dot(h, wv_ref[...]).astype(v_ref.dtype)


def _qkv_rope_call(x, g, shift, scale, pos, wq, wk, wv, seq, *, tm=512):
    t, d = x.shape
    half = ROPE_DIM // 2
    inv = 1.0 / (ROPE_THETA ** (jnp.arange(0, ROPE_DIM, 2, dtype=F32) / ROPE_DIM))
    dl = jnp.arange(LANES) % DIFF_HEAD_DIM
    inv_pat = jnp.where(dl < ROPE_DIM, inv[dl % half], 0.0).reshape(1, LANES).astype(F32)
    sgn_pat = jnp.where(dl < half, -1.0, 1.0).reshape(1, LANES).astype(F32)
    row = lambda n: pl.BlockSpec((tm, n), lambda i: (i, 0))
    full = lambda a: pl.BlockSpec(a.shape, lambda i: (0, 0))
    mod = pl.BlockSpec((1, 1, d), lambda i: ((i * tm) // seq, 0, 0))
    n = wq.shape[1]
    return pl.pallas_call(
        _qkv_rope_kernel,
        grid=(t // tm,),
        in_specs=[row(d), full(g), mod, mod, row(1), full(inv_pat), full(sgn_pat),
                  full(wq), full(wk), full(wv)],
        out_specs=[row(n), row(n), row(n)],
        out_shape=[jax.ShapeDtypeStruct((t, n), BF16)] * 3,
        compiler_params=_cparams(("arbitrary",)),
        name="qkv_rope",
    )(x, g, shift, scale, pos, inv_pat, sgn_pat, wq, wk, wv)


def _attn_kernel(q_ref, k_ref, v_ref, lq1_ref, lk1_ref, lq2_ref, lk2_ref, sg_ref, o_ref,
                 *, tq, seq, lambda_init):
    i = pl.program_id(2)
    kb = 2 * tq
    q = q_ref[...]
    lane = lax.broadcasted_iota(jnp.int32, q.shape, 1)
    zero = jnp.zeros_like(q)
    qm = (jnp.where(lane < DIFF_HEAD_DIM, q, zero), jnp.where(lane >= DIFF_HEAD_DIM, q, zero))
    f32sum = lambda a, b: jnp.sum(a[...] * b[...], axis=-1, keepdims=True)
    lam = jnp.exp(f32sum(lq1_ref, lk1_ref)) - jnp.exp(f32sum(lq2_ref, lk2_ref)) + lambda_init
    row_chunk = (i * tq + lax.broadcasted_iota(jnp.int32, (tq, kb), 0)) // ATTN_CHUNK
    col_in_block = lax.broadcasted_iota(jnp.int32, (tq, kb), 1)

    def attend(nblk):
        outs = []
        for mp in range(2):
            s = [_nt_dot(qm[mp], k_ref[j * kb:(j + 1) * kb, :]) for j in range(nblk)]
            col_chunk = ((nblk - 1) * kb + col_in_block) // ATTN_CHUNK
            s[-1] = jnp.where(col_chunk <= row_chunk, s[-1], -jnp.inf)
            m = jnp.max(s[0], axis=-1, keepdims=True)
            for sj in s[1:]:
                m = jnp.maximum(m, jnp.max(sj, axis=-1, keepdims=True))
            l = jnp.zeros_like(m)
            acc = jnp.zeros((tq, DIFF_V_DIM), F32)
            for j, sj in enumerate(s):
                p = jnp.exp(sj - m)
                l = l + jnp.sum(p, axis=-1, keepdims=True)
                acc = acc + _dot(p.astype(BF16), v_ref[j * kb:(j + 1) * kb, :])
            outs.append(acc / l)
        o = outs[0] - lam * outs[1]
        ms = jnp.mean(o * o, axis=-1, keepdims=True)
        o = o * lax.rsqrt(ms + NORM_EPS) * sg_ref[...] * (1.0 - lambda_init)
        o_ref[...] = o.astype(o_ref.dtype)

    for nblk in range(1, seq // kb + 1):
        pl.when(i // 2 == nblk - 1)(functools.partial(attend, nblk))


def _attn_call(q, k, v, lq1, lk1, lq2, lk2, subln_g, lambda_init, batch, seq, *, tq=256):
    t = q.shape[0]
    nq = seq // tq
    assert seq % (2 * tq) == 0
    vec = lambda a: a.reshape(1, -1)
    small = lambda a: pl.BlockSpec(a.shape, lambda b, h, i: (0, 0))
    kv = pl.BlockSpec((seq, DIFF_V_DIM), lambda b, h, i: (b, h))
    args = [vec(lq1), vec(lk1), vec(lq2), vec(lk2), vec(subln_g)]
    return pl.pallas_call(
        functools.partial(_attn_kernel, tq=tq, seq=seq, lambda_init=lambda_init),
        grid=(batch, DIFF_HEADS, nq),
        in_specs=[pl.BlockSpec((tq, DIFF_V_DIM), lambda b, h, i: (b * nq + i, h)), kv, kv]
                 + [small(a) for a in args],
        out_specs=pl.BlockSpec((tq, DIFF_V_DIM), lambda b, h, i: (b * nq + i, h)),
        out_shape=jax.ShapeDtypeStruct((t, DIFF_HEADS * DIFF_V_DIM), BF16),
        compiler_params=_cparams(("arbitrary", "arbitrary", "arbitrary")),
        name="diff_attn",
    )(q, k, v, *args)


def _peer_route_kernel(q_ref, sk_ref, r2_ref, e2_ref, n1_ref, f_ref, a_ref, b_ref, cand_ref, *, tt):
    neg = -jnp.inf
    k = PEER_TOPK
    kf = float(k)
    key_id = lax.broadcasted_iota(jnp.int32, (PEER_KEYS, LANES), 0).astype(F32)
    cand_id = lax.broadcasted_iota(jnp.int32, (k * k, LANES), 0).astype(F32)

    def extract(w, ids, dst_ref, by_index, want_rank):
        rank = jnp.full(w.shape, kf, F32) if want_rank else None
        tops = []
        for r in range(k):
            mx = jnp.max(w, axis=0, keepdims=True)
            tops.append(mx)
            if dst_ref is not None:
                dst_ref[r:r + 1, :] = mx
            hit = w == mx
            if by_index:
                first = jnp.min(jnp.where(hit, ids, float(w.shape[0])), axis=0, keepdims=True)
                hit = ids == first
            if want_rank:
                rank = jnp.where(hit, float(r), rank)
            w = jnp.where(hit, neg, w)
        return rank, tops, w

    def candidates(ga_ref, gb_ref):
        bvals = gb_ref[...]
        for x in range(k):
            cand_ref[x * k:(x + 1) * k, :] = ga_ref[x:x + 1, :] + bvals
        return cand_ref[...]

    def staircase_candidates(ga_ref, gb_ref):
        a = lambda lo, hi: ga_ref[lo:hi, :]
        b = lambda lo, hi: gb_ref[lo:hi, :]
        return jnp.concatenate([
            a(0, 1) + b(0, 8), a(0, 1) + b(8, 16), a(1, 2) + b(0, 8), a(2, 3) + b(0, 8), a(3, 4) + b(0, 8),
            a(0, 8) + b(0, 1), a(8, 16) + b(0, 1), a(0, 8) + b(1, 2), a(0, 8) + b(2, 3)], axis=0)

    def twin_bf16(v):
        bits = lax.bitcast_convert_type(v.astype(BF16).astype(F32), jnp.uint32)
        return bits | (bits >> 16)

    def emit(cols, s1, s2, rank2, n1, a0, b0, c_tops):
        zsum = jnp.zeros((1, LANES), F32)
        for c in c_tops:
            zsum = zsum + jnp.exp(c - c_tops[0])
        r2_ref[0, 0, :, cols] = pltpu.bitcast(rank2.astype(BF16), jnp.uint32)
        e2_ref[0, 0, :, cols] = pltpu.bitcast(jnp.exp(s2 - b0).astype(BF16), jnp.uint32)
        n1_ref[0, 0, :, cols] = twin_bf16(n1)
        f_ref[0, 0, :, cols] = twin_bf16(jnp.exp(s1 - a0) / zsum)

    count = lambda m: jnp.sum(jnp.where(m, 1.0, 0.0), axis=0, keepdims=True)

    def scores(cols):
        return (_nt_dot(sk_ref[0], q_ref[cols, 0:PEER_HALF], HIGHEST),
                _nt_dot(sk_ref[1], q_ref[cols, PEER_HALF:2 * PEER_HALF], HIGHEST))

    groups = [slice(g * LANES, (g + 1) * LANES) for g in range(tt // LANES)]
    any_tied = []
    for gidx, cols in enumerate(groups):
        ga_ref, gb_ref = a_ref.at[gidx], b_ref.at[gidx]
        s1, s2 = scores(cols)
        _, a_tops, w1 = extract(s1, key_id, ga_ref, False, False)
        rank2, b_tops, _ = extract(s2, key_id, gb_ref, False, True)
        _, c_tops, _ = extract(staircase_candidates(ga_ref, gb_ref), None, None, False, False)
        tau = c_tops[k - 1]
        n1 = jnp.zeros_like(s1)
        for y in range(k):
            n1 = n1 + jnp.where((s1 + b_tops[y]) >= tau, 1.0, 0.0)
        emit(cols, s1, s2, rank2, n1, a_tops[0], b_tops[0], c_tops)
        tied = ((count(w1 == neg) != kf) | (count(rank2 < kf) != kf)
                | (jnp.sum(n1, axis=0, keepdims=True) != kf))
        any_tied.append(jnp.max(jnp.where(tied, 1.0, 0.0)) > 0.0)

    for gidx, cols in enumerate(groups):
        @pl.when(any_tied[gidx])
        def _():
            ga_ref, gb_ref = a_ref.at[gidx], b_ref.at[gidx]
            s1, s2 = scores(cols)
            rank1, a_x, _ = extract(s1, key_id, ga_ref, True, True)
            rank2x, b_x, _ = extract(s2, key_id, gb_ref, True, True)
            rank3, c_x, _ = extract(candidates(ga_ref, gb_ref), cand_id, None, True, True)
            win = rank3 < kf
            n1x = jnp.zeros_like(s1)
            for x in range(k):
                n_x = count(win[x * k:(x + 1) * k])
                n1x = n1x + jnp.where(rank1 == float(x), n_x, 0.0)
            emit(cols, s1, s2, rank2x, n1x, a_x[0], b_x[0], c_x)


def _peer_route_call(q, subkeys, *, tt):
    t = q.shape[0]
    nt = t // tt
    spec = pl.BlockSpec((1, 1, PEER_KEYS, tt), lambda i, h: (h, i, 0, 0))
    pair_spec = pl.BlockSpec((1, 1, PEER_KEYS // 2, tt), lambda i, h: (h, i, 0, 0))
    tab = lambda rows: jax.ShapeDtypeStruct((PEER_HEADS, nt, rows, tt), jnp.uint32)
    return pl.pallas_call(
        functools.partial(_peer_route_kernel, tt=tt),
        grid=(nt, PEER_HEADS),
        in_specs=[
            pl.BlockSpec((tt, 2 * PEER_HALF), lambda i, h: (i, h)),
            pl.BlockSpec(subkeys.shape, lambda i, h: (0, 0, 0)),
        ],
        out_specs=[pair_spec, pair_spec, spec, spec],
        out_shape=[tab(PEER_KEYS // 2), tab(PEER_KEYS // 2), tab(PEER_KEYS), tab(PEER_KEYS)],
        scratch_shapes=[
            pltpu.VMEM((tt // LANES, PEER_TOPK, LANES), F32),
            pltpu.VMEM((tt // LANES, PEER_TOPK, LANES), F32),
            pltpu.VMEM((PEER_TOPK * PEER_TOPK, LANES), F32),
        ],
        compiler_params=_cparams(("arbitrary", "arbitrary")),
        name="peer_route",
    )(q, subkeys)


def _gelu(x):
    return 0.5 * x * (1.0 + lax.erf(x * (1.0 / math.sqrt(2.0))))


def _peer_expert_kernel(x_ref, gate_ref, ht_ref, u_ref, vt_ref, n1_ref, f_ref, r2_ref, e2_ref,
                        o_ref, pre_a, pre_b, wa_ref, acc_ref, *, rows_per_tile):
    s = pl.program_id(1)
    te, tt = pre_a.shape
    d = acc_ref.shape[0]
    blk = 16
    rows_per_piece = 2

    @pl.when(s == 0)
    def _():
        acc_ref[...] = jnp.zeros_like(acc_ref)
        pre_b[...] = jnp.zeros_like(pre_b)

    def step(pre_new, pre_old):
        halves = [slice(m * d // 2, (m + 1) * d // 2) for m in range(2)]
        zero = jnp.zeros((blk, LANES), BF16)
        as_bf16 = lambda words: pltpu.bitcast(words, BF16)
        for il in range(rows_per_tile):
            for lt in range(tt // LANES):
                lanes = slice(lt * LANES, (lt + 1) * LANES)
                row = lambda ref, h: as_bf16(jnp.broadcast_to(ref[h, 0, il:il + 1, lanes], (blk // 2, LANES)))
                n1b = [row(n1_ref, h) for h in range(PEER_HEADS)]
                fb = [row(f_ref, h) for h in range(PEER_HEADS)]
                for jb in range(PEER_KEYS // blk):
                    words = slice(jb * blk // 2, (jb + 1) * blk // 2)
                    w = zero
                    for h in range(PEER_HEADS):
                        hit = as_bf16(r2_ref[h, 0, words, lanes]) < n1b[h]
                        w = w + jnp.where(hit, as_bf16(e2_ref[h, 0, words, lanes]), zero) * fb[h]
                    rows = slice(il * PEER_KEYS + jb * blk, il * PEER_KEYS + (jb + 1) * blk)
                    wa_ref[rows, lanes] = w
        n_pre = te // 512
        for m in range(n_pre):
            rows = slice(m * te // n_pre, (m + 1) * te // n_pre)
            pre_new[rows, :] = _dot(u_ref[rows, :], ht_ref[...])
        piece_rows = rows_per_piece * PEER_KEYS
        for p in range(te // piece_rows):
            piece = slice(p * piece_rows, (p + 1) * piece_rows)
            for r0 in range(p * piece_rows, (p + 1) * piece_rows, blk):
                rows = slice(r0, r0 + blk)
                wa_ref[rows, :] = wa_ref[rows, :] * _gelu(pre_old[rows, :]).astype(BF16)
            for rows in halves:
                acc_ref[rows, :] += _dot(vt_ref[rows, piece], wa_ref[piece, :])

    @pl.when(s % 2 == 0)
    def _():
        step(pre_a, pre_b)

    @pl.when(s % 2 == 1)
    def _():
        step(pre_b, pre_a)

    @pl.when(s == pl.num_programs(1) - 1)
    def _():
        o_ref[...] = x_ref[...] + gate_ref[0] * acc_ref[...].T


def _peer_expert_call(x, gate, hb, u_b, vt_b, n1, f, r2, e2, seq, *, tt, te=1024):
    t, d = x.shape
    ne = u_b.shape[0] // te
    rows_per_tile = te // PEER_KEYS
    lag = lambda s, n: jnp.clip(s - n, 0, ne - 1)
    sel = pl.BlockSpec((PEER_HEADS, 1, rows_per_tile, tt), lambda i, s: (0, i, lag(s, 1), 0))
    full = pl.BlockSpec((PEER_HEADS, 1, PEER_KEYS // 2, tt), lambda i, s: (0, i, 0, 0))
    return pl.pallas_call(
        functools.partial(_peer_expert_kernel, rows_per_tile=rows_per_tile),
        grid=(t // tt, ne + 1),
        in_specs=[
            pl.BlockSpec((tt, d), lambda i, s: (i, 0)),
            pl.BlockSpec((1, 1, d), lambda i, s: ((i * tt) // seq, 0, 0)),
            pl.BlockSpec((d, tt), lambda i, s: (0, i)),
            pl.BlockSpec((te, d), lambda i, s: (lag(s, 0), 0)),
            pl.BlockSpec((d, te), lambda i, s: (0, lag(s, 1))),
            sel, sel, full, full,
        ],
        out_specs=pl.BlockSpec((tt, d), lambda i, s: (i, 0)),
        out_shape=jax.ShapeDtypeStruct((t, d), F32),
        scratch_shapes=[
            pltpu.VMEM((te, tt), F32),
            pltpu.VMEM((te, tt), F32),
            pltpu.VMEM((te, tt), BF16),
            pltpu.VMEM((d, tt), F32),
        ],
        compiler_params=_cparams(("arbitrary", "arbitrary")),
        name="peer_experts",
    )(x, gate, hb, u_b, vt_b, n1, f, r2, e2)


def _final_norm_kernel(x_ref, g_ref, o_ref):
    x = x_ref[...]
    ms = jnp.mean(x * x, axis=-1, keepdims=True)
    o_ref[...] = x * lax.rsqrt(ms + NORM_EPS) * g_ref[...]


def _final_norm_call(x, g, *, tm=512):
    t, d = x.shape
    return pl.pallas_call(
        _final_norm_kernel,
        grid=(t // tm,),
        in_specs=[pl.BlockSpec((tm, d), lambda i: (i, 0)), pl.BlockSpec((1, d), lambda i: (0, 0))],
        out_specs=pl.BlockSpec((tm, d), lambda i: (i, 0)),
        out_shape=jax.ShapeDtypeStruct((t, d), F32),
        compiler_params=_cparams(("arbitrary",)),
        name="final_norm",
    )(x, g.reshape(1, d))


def _pad_rows(a, rows):
    return jnp.pad(a, ((0, rows - a.shape[0]), (0, 0)))


def _pad_cols(a, cols):
    return jnp.pad(a, ((0, 0), (0, cols - a.shape[1])))


def _hybrid_mixer(x, g, shift, scale, batch, seq, w_in, conv_w, conv_b, conv_ln_g, conv_ln_b, mu, w0, w2,
                  a0, a2, g2, k_k, k_a, r_k, ln_g, ln_b):
    rw = RWKV_WIDTH
    c0 = 2 * CONV_WIDTH
    lora0 = c0 + 3 * rw
    l1, l2 = lora0 + DECAY_LORA, lora0 + DECAY_LORA + AAA_LORA
    w_conv = w_in[:, :c0].astype(BF16)
    w_rkv = w_in[:, c0:lora0].astype(BF16)
    w_lora = jnp.concatenate([
        _pad_cols(w_in[:, lora0:l1], LANES), _pad_cols(w_in[:, l1:l2], LANES),
        _pad_cols(w_in[:, l2:], 2 * LANES)], axis=1).astype(BF16)
    zc, zr, zl = _norm_matmul_call(x, g, shift, scale, [w_conv, w_rkv, w_lora], [F32, F32, F32], seq,
                                   name="hybrid_in_proj")
    ya = _conv_call(zc, conv_w, conv_b, conv_ln_g, conv_ln_b, batch, seq)
    mu_l = jnp.concatenate([
        _pad_cols(mu[None, lora0 - c0:l1 - c0], LANES), _pad_cols(mu[None, l1 - c0:l2 - c0], LANES),
        _pad_cols(mu[None, l2 - c0:], 2 * LANES)], axis=1)
    vec = lambda a: a.reshape(1, rw)
    head_id = jnp.arange(RWKV_GROUP * RWKV_HEAD) // RWKV_HEAD
    params = dict(
        mu_r=mu[None, :3 * rw], mu_l=mu_l, w0=vec(w0), a0=vec(a0), k_k=vec(k_k), k_a=vec(k_a),
        r_k=vec(r_k), ln_g=vec(ln_g), ln_b=vec(ln_b),
        w2=_pad_rows(w2, LANES).astype(BF16), a2=_pad_rows(a2, LANES).astype(BF16),
        g2=_pad_rows(g2, 2 * LANES).astype(BF16),
        seg=(head_id[:, None] == head_id[None, :]).astype(BF16))
    yb = _rwkv_call(zr, zl, params, batch, seq)
    return ya, yb


def kernel(x, c, positions, ada_w, ada_b, norm_mix_g, norm_ffn_g, hyb_w_in, conv_w, conv_b, conv_ln_g, conv_ln_b, rwkv_mu, rwkv_w0, rwkv_w2, rwkv_a0, rwkv_a2, rwkv_g2, rwkv_k_k, rwkv_k_a, rwkv_r_k, rwkv_ln_g, rwkv_ln_b, hyb_w_out, diff_w_qkv, diff_lq1, diff_lk1, diff_lq2, diff_lk2, diff_subln_g, diff_w_out, peer_w_q, peer_subkeys, peer_u, peer_v, final_g):
    batch, seq, d = x.shape
    depth = ada_w.shape[0]
    t = batch * seq
    xt = x.reshape(t, d)
    pos = positions.reshape(t, 1)
    mod = _ada_call(c, ada_w, ada_b).reshape(depth, batch, 6, 1, d)

    for l in range(depth):
        sh1, sc1, g1, sh2, sc2, g2 = (mod[l, :, j] for j in range(6))
        gmix = norm_mix_g[l].reshape(1, d)
        if l % 2 == 0:
            e = l // 2
            ya, yb = _hybrid_mixer(
                xt, gmix, sh1, sc1, batch, seq, hyb_w_in[e], conv_w[e], conv_b[e], conv_ln_g[e],
                conv_ln_b[e], rwkv_mu[e], rwkv_w0[e], rwkv_w2[e], rwkv_a0[e], rwkv_a2[e], rwkv_g2[e],
                rwkv_k_k[e], rwkv_k_a[e], rwkv_r_k[e], rwkv_ln_g[e], rwkv_ln_b[e])
            w_out = hyb_w_out[e].astype(BF16)
            xt = _out_proj_call(xt, g1, [ya.astype(BF16), yb.astype(BF16)],
                                [w_out[:CONV_WIDTH], w_out[CONV_WIDTH:]], seq)
        else:
            o = l // 2
            lambda_init = 0.8 - 0.6 * math.exp(-0.3 * l)
            wqkv = diff_w_qkv[o].astype(BF16)
            n = wqkv.shape[1] // 3
            q, k, v = _qkv_rope_call(xt, gmix, sh1, sc1, pos, wqkv[:, :n], wqkv[:, n:2 * n], wqkv[:, 2 * n:],
                                     seq)
            y = _attn_call(q, k, v, diff_lq1[o], diff_lk1[o], diff_lq2[o], diff_lk2[o], diff_subln_g[o],
                           lambda_init, batch, seq)
            xt = _out_proj_call(xt, g1, [y], [diff_w_out[o].astype(BF16)], seq)

        q, hb = _norm_matmul_call(xt, norm_ffn_g[l].reshape(1, d), sh2, sc2, [peer_w_q[l]], [F32], seq,
                                  split=True, emit_h=True, name="peer_query")
        r2, e2, n1, f = _peer_route_call(q, peer_subkeys[l], tt=PEER_TOKEN_TILE)
        xt = _peer_expert_call(xt, g2, hb, peer_u[l].astype(BF16), peer_v[l].T.astype(BF16),
                               n1, f, r2, e2, seq, tt=PEER_TOKEN_TILE)

    return _final_norm_call(xt, final_g).reshape(batch, seq, d)
```

```python
import functools
import math

import jax
import jax.numpy as jnp
from jax import lax
from jax.experimental import pallas as pl
from jax.experimental.pallas import tpu as pltpu

F32 = jnp.float32
BF16 = jnp.bfloat16
HIGHEST = lax.Precision.HIGHEST

CONV_WIDTH = 512
CONV_KERNEL = 31
RWKV_WIDTH = 512
RWKV_HEAD = 64
RWKV_HEADS = 8
DECAY_LORA = 64
AAA_LORA = 64
GATE_LORA = 160
DIFF_HEADS = 8
DIFF_HEAD_DIM = 64
DIFF_V_DIM = 128
ROPE_DIM = 16
ROPE_THETA = 500000.0
ATTN_CHUNK = 64
PEER_HEADS = 8
PEER_KEYS = 128
PEER_HALF = 128
PEER_TOPK = 16
NORM_EPS = 1e-6
LN_EPS = 1e-5
RWKV_GN_EPS = 64e-5

LANES = 128
SCAN_CHUNK = 64
CONV_HALO = 32
PEER_TOKEN_TILE = 256
VMEM_LIMIT = 48 * 1024 * 1024


def _cparams(semantics, flags=None):
    return pltpu.CompilerParams(dimension_semantics=semantics, vmem_limit_bytes=VMEM_LIMIT, flags=flags)


def _nt_dot(a, b, precision=None):
    return lax.dot_general(a, b, (((1,), (1,)), ((), ())), precision=precision,
                           preferred_element_type=F32)


def _dot(a, b, precision=None):
    return jnp.dot(a, b, precision=precision, preferred_element_type=F32)


def _sigmoid(x):
    return 1.0 / (1.0 + jnp.exp(-x))


def _ada_kernel(c_ref, w_ref, b_ref, o_ref):
    c = c_ref[...]
    cond = c * _sigmoid(c)
    o_ref[0] = _dot(cond, w_ref[0], HIGHEST) + b_ref[0]


def _ada_call(c, ada_w, ada_b):
    depth, d, n = ada_w.shape
    b = c.shape[0]
    tn = 1536
    return pl.pallas_call(
        _ada_kernel,
        grid=(depth, n // tn),
        in_specs=[
            pl.BlockSpec((b, d), lambda l, j: (0, 0)),
            pl.BlockSpec((1, d, tn), lambda l, j: (l, 0, j)),
            pl.BlockSpec((1, 1, tn), lambda l, j: (l, 0, j)),
        ],
        out_specs=pl.BlockSpec((1, b, tn), lambda l, j: (l, 0, j)),
        out_shape=jax.ShapeDtypeStruct((depth, b, n), F32),
        compiler_params=_cparams(("arbitrary", "arbitrary")),
        name="ada_mod",
    )(c, ada_w, ada_b.reshape(depth, 1, n))


def _modulate(x, g, shift, scale):
    ms = jnp.mean(x * x, axis=-1, keepdims=True)
    y = x * lax.rsqrt(ms + NORM_EPS)
    return (y * g) * (1.0 + scale) + shift


def _split_bf16(a):
    hi = a.astype(BF16)
    return hi, (a - hi.astype(F32)).astype(BF16)


def _norm_matmul_kernel(*refs, n_w, split, emit_h):
    x_ref, g_ref, sh_ref, sc_ref = refs[:4]
    w_refs = refs[4:4 + n_w]
    o_refs = refs[4 + n_w:]
    h = _modulate(x_ref[...], g_ref[...], sh_ref[0], sc_ref[0])
    hb = h.astype(BF16)
    if split:
        h_lo = (h - hb.astype(F32)).astype(BF16)
        for k in range(n_w // 2):
            w_hi, w_lo = w_refs[2 * k][...], w_refs[2 * k + 1][...]
            o_refs[k][...] = (_dot(hb, w_hi) + _dot(h_lo, w_hi) + _dot(hb, w_lo)).astype(o_refs[k].dtype)
    else:
        for w_ref, o_ref in zip(w_refs, o_refs[:n_w]):
            o_ref[...] = _dot(hb, w_ref[...]).astype(o_ref.dtype)
    if emit_h:
        o_refs[-1][...] = h.T.astype(BF16)


def _norm_matmul_call(x, g, shift, scale, weights, out_dtypes, seq, *, split=False, emit_h=False,
                      tm=512, name="norm_matmul"):
    t, d = x.shape
    if split:
        weights = [part for w in weights for part in _split_bf16(w)]
    in_specs = [
        pl.BlockSpec((tm, d), lambda i: (i, 0)),
        pl.BlockSpec((1, d), lambda i: (0, 0)),
        pl.BlockSpec((1, 1, d), lambda i: ((i * tm) // seq, 0, 0)),
        pl.BlockSpec((1, 1, d), lambda i: ((i * tm) // seq, 0, 0)),
    ]
    out_specs, out_shape = [], []
    for w in weights:
        in_specs.append(pl.BlockSpec(w.shape, lambda i: (0, 0)))
    for w, dt in zip(weights[::2] if split else weights, out_dtypes):
        n = w.shape[1]
        out_specs.append(pl.BlockSpec((tm, n), lambda i: (i, 0)))
        out_shape.append(jax.ShapeDtypeStruct((t, n), dt))
    if emit_h:
        out_specs.append(pl.BlockSpec((d, tm), lambda i: (0, i)))
        out_shape.append(jax.ShapeDtypeStruct((d, t), BF16))
    return pl.pallas_call(
        functools.partial(_norm_matmul_kernel, n_w=len(weights), split=split, emit_h=emit_h),
        grid=(t // tm,),
        in_specs=in_specs,
        out_specs=out_specs,
        out_shape=out_shape,
        compiler_params=_cparams(("arbitrary",)),
        name=name,
    )(x, g, shift, scale, *weights)


def _out_proj_kernel(*refs, n_y):
    x_ref, gate_ref = refs[:2]
    y_refs = refs[2:2 + n_y]
    w_refs = refs[2 + n_y:2 + 2 * n_y]
    o_ref = refs[2 + 2 * n_y]
    acc = _dot(y_refs[0][...], w_refs[0][...])
    for y_ref, w_ref in zip(y_refs[1:], w_refs[1:]):
        acc = acc + _dot(y_ref[...], w_ref[...])
    o_ref[...] = x_ref[...] + gate_ref[0] * acc


def _out_proj_call(x, gate, ys, ws, seq, *, tm=512):
    t, d = x.shape
    in_specs = [
        pl.BlockSpec((tm, d), lambda i: (i, 0)),
        pl.BlockSpec((1, 1, d), lambda i: ((i * tm) // seq, 0, 0)),
    ]
    for y in ys:
        in_specs.append(pl.BlockSpec((tm, y.shape[1]), lambda i: (i, 0)))
    for w in ws:
        in_specs.append(pl.BlockSpec(w.shape, lambda i: (0, 0)))
    return pl.pallas_call(
        functools.partial(_out_proj_kernel, n_y=len(ys)),
        grid=(t // tm,),
        in_specs=in_specs,
        out_specs=pl.BlockSpec((tm, d), lambda i: (i, 0)),
        out_shape=jax.ShapeDtypeStruct((t, d), F32),
        compiler_params=_cparams(("arbitrary",)),
        name="out_proj",
    )(x, gate, *ys, *ws)


def _conv_kernel(z_ref, w_ref, b_ref, g_ref, beta_ref, o_ref, ext_ref, *, ts):
    width = CONV_WIDTH

    @pl.when(pl.program_id(1) == 0)
    def _():
        ext_ref[0:CONV_HALO, :] = jnp.zeros((CONV_HALO, width), F32)

    z = z_ref[...]
    u = z[:, :width] * _sigmoid(z[:, width:])
    ext_ref[CONV_HALO:CONV_HALO + ts, :] = u
    base = CONV_HALO - (CONV_KERNEL - 1)
    rows = 64
    for cb in range(width // LANES):
        cs = slice(cb * LANES, (cb + 1) * LANES)
        for rb in range(ts // rows):
            acc = jnp.zeros((rows, LANES), F32)
            for j in range(CONV_KERNEL):
                start = rb * rows + base + j
                acc = acc + w_ref[j:j + 1, cs] * ext_ref[start:start + rows, cs]
            o_ref[rb * rows:(rb + 1) * rows, cs] = acc
    conv = o_ref[...] + b_ref[...]
    mu = jnp.mean(conv, axis=-1, keepdims=True)
    dlt = conv - mu
    var = jnp.mean(dlt * dlt, axis=-1, keepdims=True)
    y = dlt * lax.rsqrt(var + LN_EPS) * g_ref[...] + beta_ref[...]
    o_ref[...] = y * _sigmoid(y)
    ext_ref[0:CONV_HALO, :] = ext_ref[ts:ts + CONV_HALO, :]


def _conv_call(z, conv_w, conv_b, ln_g, ln_b, batch, seq, *, ts=256):
    t = z.shape[0]
    width = CONV_WIDTH
    nts = seq // ts
    vec = lambda a: a.reshape(1, width)
    return pl.pallas_call(
        functools.partial(_conv_kernel, ts=ts),
        grid=(batch, nts),
        in_specs=[
            pl.BlockSpec((ts, 2 * width), lambda b, i: (b * nts + i, 0)),
            pl.BlockSpec((CONV_KERNEL, width), lambda b, i: (0, 0)),
            pl.BlockSpec((1, width), lambda b, i: (0, 0)),
            pl.BlockSpec((1, width), lambda b, i: (0, 0)),
            pl.BlockSpec((1, width), lambda b, i: (0, 0)),
        ],
        out_specs=pl.BlockSpec((ts, width), lambda b, i: (b * nts + i, 0)),
        out_shape=jax.ShapeDtypeStruct((t, width), F32),
        scratch_shapes=[pltpu.VMEM((ts + CONV_HALO, width), F32)],
        compiler_params=_cparams(("arbitrary", "arbitrary")),
        name="conformer_conv",
    )(z, conv_w, vec(conv_b), vec(ln_g), vec(ln_b))


def _softplus(x):
    return jnp.maximum(x, 0.0) + jnp.log(1.0 + jnp.exp(-jnp.abs(x)))


def _unit_lower_inverses(mats, row, col, size):
    eye = (row == col).astype(F32)
    blk = lambda m: (row // m) == (col // m)
    bdot = lambda p, q: _dot(p.astype(BF16), q.astype(BF16))
    n1 = [jnp.where(blk(8), a, 0.0) for a in mats]
    t = [eye + x for x in n1]
    n2 = [bdot(x, x) for x in n1]
    t = [x + bdot(x, y) for x, y in zip(t, n2)]
    n4 = [bdot(x, x) for x in n2]
    t = [x + bdot(x, y) for x, y in zip(t, n4)]
    m = 8
    while m < size:
        new = blk(2 * m) & jnp.logical_not(blk(m))
        et = [bdot(jnp.where(new, a, 0.0), x) for a, x in zip(mats, t)]
        t = [x + bdot(x, y) for x, y in zip(t, et)]
        m *= 2
    return t


def _split_dot(x, w_bf16, parts):
    acc = None
    for _ in range(parts):
        piece = x.astype(BF16)
        term = _dot(piece, w_bf16)
        acc = term if acc is None else acc + term
        x = x - piece.astype(F32)
    return acc


RWKV_GROUP = 4
RWKV_BATCH_TILE = 2


def _rwkv_kernel(zr_ref, zl_ref, mur_ref, mul_ref, w0_ref, a0_ref, kk_ref, ka_ref, rk_ref,
                 lng_ref, lnb_ref, w2_ref, a2_ref, g2_ref, seg_ref, o_ref,
                 extr_ref, extl_ref, state_ref):
    c = SCAN_CHUNK
    rw = RWKV_WIDTH
    n = RWKV_HEAD
    nb = RWKV_BATCH_TILE
    gw = RWKV_GROUP * n
    groups = rw // gw

    @pl.when(pl.program_id(1) == 0)
    def _():
        extr_ref[:, 0:8, :] = jnp.zeros((nb, 8, 3 * rw), F32)
        extl_ref[:, 0:8, :] = jnp.zeros((nb, 8, rw), F32)
        state_ref[...] = jnp.zeros_like(state_ref)

    zr_rows, zl_rows = [], []
    for b in range(nb):
        zr = zr_ref[b]
        zl = zl_ref[b]
        extr_ref[b, 8:8 + c, :] = zr
        extl_ref[b, 8:8 + c, :] = zl
        zr_rows.append(zr + (extr_ref[b, 7:7 + c, :] - zr) * mur_ref[...])
        zl_rows.append(zl + (extl_ref[b, 7:7 + c, :] - zl) * mul_ref[...])
        extr_ref[b, 0:8, :] = extr_ref[b, c:c + 8, :]
        extl_ref[b, 0:8, :] = extl_ref[b, c:c + 8, :]
    zr = jnp.concatenate(zr_rows, axis=0)
    zl = jnp.concatenate(zl_rows, axis=0)

    r = zr[:, 0:rw]
    k = zr[:, rw:2 * rw]
    v = zr[:, 2 * rw:3 * rw]
    wd = zl[:, 0:LANES]
    ad = zl[:, LANES:2 * LANES]
    gd = zl[:, 2 * LANES:4 * LANES]
    seg = seg_ref[...]

    def segsum(x, parts):
        return jnp.concatenate([_split_dot(x[:, g * gw:(g + 1) * gw], seg, parts) for g in range(groups)],
                               axis=1)

    w_raw = -_softplus(-(w0_ref[...] + _dot(jnp.tanh(wd).astype(BF16), w2_ref[...]))) - 0.5
    lw = -jnp.exp(w_raw)
    alpha = _sigmoid(a0_ref[...] + _dot(ad.astype(BF16), a2_ref[...]))
    gate = _dot(_sigmoid(gd).astype(BF16), g2_ref[...])
    kkf = k * kk_ref[...]
    kk = kkf / jnp.maximum(jnp.sqrt(segsum(kkf * kkf, 3)), 1e-12)
    kp = k * (1.0 + (alpha - 1.0) * ka_ref[...])
    bonus = segsum(r * kp * rk_ref[...], 3) * v
    av = -kk
    bv = kk * alpha

    trow = lax.broadcasted_iota(jnp.int32, (nb * c, nb * c), 0)
    tcol = lax.broadcasted_iota(jnp.int32, (nb * c, nb * c), 1)
    same_seq_incl = ((trow // c) == (tcol // c)) & (tcol <= trow)
    cum = _split_dot_lhs_exact(jnp.where(same_seq_incl, 1.0, 0.0).astype(BF16), lw)
    tot = jnp.concatenate([jnp.broadcast_to(cum[(b + 1) * c - 1:(b + 1) * c, :], (c, rw)) for b in range(nb)],
                          axis=0)
    g_in = jnp.exp(cum)
    g_inp = jnp.exp(cum - lw)
    g_out = jnp.exp(-cum)
    g_end = jnp.exp(tot - cum)
    g_tot = jnp.exp(tot)
    a_in = av * g_inp
    r_in = r * g_in
    b_out = bv * g_out
    k_out = kp * g_out
    b_end = bv * g_end
    k_end = kp * g_end

    row = lax.broadcasted_iota(jnp.int32, (gw, gw), 0)
    col = lax.broadcasted_iota(jnp.int32, (gw, gw), 1)
    same_head = (row // n) == (col // n)
    strict = same_head & ((col % n) < (row % n))
    incl = same_head & ((col % n) <= (row % n))

    def expand(x):
        return jnp.where(same_head, jnp.concatenate([x] * RWKV_GROUP, axis=0), 0.0)

    bf = lambda x: x.astype(BF16)
    chains = [(b, g) for b in range(nb) for g in range(groups)]
    pick = lambda x: [expand(x[b * c:(b + 1) * c, g * gw:(g + 1) * gw]) for b, g in chains]
    each = lambda fn, *lists: [fn(*args) for args in zip(*lists)]
    a_bd, r_bd = each(bf, pick(a_in)), each(bf, pick(r_in))
    bo_bd, ko_bd = each(bf, pick(b_out)), each(bf, pick(k_out))
    v_f = pick(v)
    v_bd = each(bf, v_f)
    st = [state_ref[b, g] for b, g in chains]
    st_b = each(bf, st)
    a_ab = each(lambda p, q: jnp.where(strict, _nt_dot(p, q), 0.0), a_bd, bo_bd)
    a_ak = each(lambda p, q: bf(jnp.where(strict, _nt_dot(p, q), 0.0)), a_bd, ko_bd)
    a_rb = each(lambda p, q: bf(jnp.where(incl, _nt_dot(p, q), 0.0)), r_bd, bo_bd)
    a_rk = each(lambda p, q: bf(jnp.where(incl, _nt_dot(p, q), 0.0)), r_bd, ko_bd)
    ph_a = each(_nt_dot, a_bd, st_b)
    ph_r = each(_nt_dot, r_bd, st_b)
    tinv = _unit_lower_inverses(a_ab, row, col, n)
    rhs = each(lambda p, m, w: bf(p + _dot(m, w)), ph_a, a_ak, v_bd)
    u = each(lambda t, x: _dot(bf(t), x), tinv, rhs)
    y_bd = each(lambda p, m1, uu, m2, w: p + _dot(m1, bf(uu)) + _dot(m2, w), ph_r, a_rb, u, a_rk, v_bd)
    be_bd, ke_bd = each(bf, pick(b_end)), each(bf, pick(k_end))
    for (b, g), s0, uu, vf, be, ke in zip(chains, st, u, v_f, be_bd, ke_bd):
        state_ref[b, g] = (s0 * g_tot[b * c:b * c + 1, g * gw:(g + 1) * gw]
                           + _dot(bf(uu.T), be) + _dot(bf(vf.T), ke))
    y_g = [sum(m[i * c:(i + 1) * c] for i in range(RWKV_GROUP)) for m in y_bd]
    y = jnp.concatenate([jnp.concatenate(y_g[b * groups:(b + 1) * groups], axis=1) for b in range(nb)],
                        axis=0)

    inv_n = 1.0 / n
    mu_y = segsum(y, 3) * inv_n
    dy = y - mu_y
    var_y = segsum(dy * dy, 3) * inv_n
    yn = dy * lax.rsqrt(var_y + RWKV_GN_EPS)
    out = (yn * lng_ref[...] + lnb_ref[...] + bonus) * gate
    for b in range(nb):
        o_ref[b] = out[b * c:(b + 1) * c]


def _split_dot_lhs_exact(w_bf16, x):
    acc = None
    for _ in range(3):
        piece = x.astype(BF16)
        term = _dot(w_bf16, piece)
        acc = term if acc is None else acc + term
        x = x - piece.astype(F32)
    return acc


def _rwkv_call(zr, zl, p, batch, seq):
    c = SCAN_CHUNK
    rw = RWKV_WIDTH
    nb = RWKV_BATCH_TILE
    nc = seq // c
    gw = RWKV_GROUP * RWKV_HEAD
    const = lambda shape: pl.BlockSpec(shape, lambda b, i: (0,) * len(shape))
    vec = const((1, rw))
    out = pl.pallas_call(
        _rwkv_kernel,
        grid=(batch // nb, nc),
        in_specs=[
            pl.BlockSpec((nb, c, 3 * rw), lambda b, i: (b, i, 0)),
            pl.BlockSpec((nb, c, rw), lambda b, i: (b, i, 0)),
            const((1, 3 * rw)), vec,
            vec, vec, vec, vec, vec, vec, vec,
            const((LANES, rw)), const((LANES, rw)), const((2 * LANES, rw)),
            const((gw, gw)),
        ],
        out_specs=pl.BlockSpec((nb, c, rw), lambda b, i: (b, i, 0)),
        out_shape=jax.ShapeDtypeStruct((batch, seq, rw), F32),
        scratch_shapes=[
            pltpu.VMEM((nb, c + 8, 3 * rw), F32),
            pltpu.VMEM((nb, c + 8, rw), F32),
            pltpu.VMEM((nb, rw // gw, gw, gw), F32),
        ],
        compiler_params=_cparams(("arbitrary", "arbitrary")),
        name="rwkv7_mix",
    )(zr.reshape(batch, seq, 3 * rw), zl.reshape(batch, seq, rw), p["mu_r"], p["mu_l"], p["w0"], p["a0"],
      p["k_k"], p["k_a"], p["r_k"], p["ln_g"], p["ln_b"], p["w2"], p["a2"], p["g2"], p["seg"])
    return out.reshape(batch * seq, rw)


def _qkv_rope_kernel(x_ref, g_ref, sh_ref, sc_ref, pos_ref, inv_ref, sgn_ref, wq_ref, wk_ref, wv_ref,
                     q_ref, k_ref, v_ref):
    h = _modulate(x_ref[...], g_ref[...], sh_ref[0], sc_ref[0]).astype(BF16)
    ang = pos_ref[...].astype(F32) * inv_ref[...]
    cosf = jnp.cos(ang)
    sinf = jnp.sin(ang) * sgn_ref[...]
    lane = lax.broadcasted_iota(jnp.int32, ang.shape, 1)
    low = (lane % DIFF_HEAD_DIM) < (ROPE_DIM // 2)
    half = ROPE_DIM // 2

    def rope(w_ref, o_ref, scale):
        z = _dot(h, w_ref[...])
        for cb in range(z.shape[1] // LANES):
            zc = z[:, cb * LANES:(cb + 1) * LANES]
            partner = jnp.where(low, pltpu.roll(zc, LANES - half, axis=1), pltpu.roll(zc, half, axis=1))
            o_ref[:, cb * LANES:(cb + 1) * LANES] = ((zc * cosf + partner * sinf) * scale).astype(o_ref.dtype)

    rope(wq_ref, q_ref, DIFF_HEAD_DIM ** -0.5)
    rope(wk_ref, k_ref, 1.0)
    v_ref[...] = _dot(h, wv_ref[...]).astype(v_ref.dtype)


def _qkv_rope_call(x, g, shift, scale, pos, wq, wk, wv, seq, *, tm=512):
    t, d = x.shape
    half = ROPE_DIM // 2
    inv = 1.0 / (ROPE_THETA ** (jnp.arange(0, ROPE_DIM, 2, dtype=F32) / ROPE_DIM))
    dl = jnp.arange(LANES) % DIFF_HEAD_DIM
    inv_pat = jnp.where(dl < ROPE_DIM, inv[dl % half], 0.0).reshape(1, LANES).astype(F32)
    sgn_pat = jnp.where(dl < half, -1.0, 1.0).reshape(1, LANES).astype(F32)
    row = lambda n: pl.BlockSpec((tm, n), lambda i: (i, 0))
    full = lambda a: pl.BlockSpec(a.shape, lambda i: (0, 0))
    mod = pl.BlockSpec((1, 1, d), lambda i: ((i * tm) // seq, 0, 0))
    n = wq.shape[1]
    return pl.pallas_call(
        _qkv_rope_kernel,
        grid=(t // tm,),
        in_specs=[row(d), full(g), mod, mod, row(1), full(inv_pat), full(sgn_pat),
                  full(wq), full(wk), full(wv)],
        out_specs=[row(n), row(n), row(n)],
        out_shape=[jax.ShapeDtypeStruct((t, n), BF16)] * 3,
        compiler_params=_cparams(("arbitrary",)),
        name="qkv_rope",
    )(x, g, shift, scale, pos, inv_pat, sgn_pat, wq, wk, wv)


def _attn_kernel(q_ref, k_ref, v_ref, lq1_ref, lk1_ref, lq2_ref, lk2_ref, sg_ref, o_ref,
                 *, tq, seq, lambda_init):
    i = pl.program_id(2)
    kb = 2 * tq
    q = q_ref[...]
    lane = lax.broadcasted_iota(jnp.int32, q.shape, 1)
    zero = jnp.zeros_like(q)
    qm = (jnp.where(lane < DIFF_HEAD_DIM, q, zero), jnp.where(lane >= DIFF_HEAD_DIM, q, zero))
    f32sum = lambda a, b: jnp.sum(a[...] * b[...], axis=-1, keepdims=True)
    lam = jnp.exp(f32sum(lq1_ref, lk1_ref)) - jnp.exp(f32sum(lq2_ref, lk2_ref)) + lambda_init
    row_chunk = (i * tq + lax.broadcasted_iota(jnp.int32, (tq, kb), 0)) // ATTN_CHUNK
    col_in_block = lax.broadcasted_iota(jnp.int32, (tq, kb), 1)

    def attend(nblk):
        outs = []
        for mp in range(2):
            s = [_nt_dot(qm[mp], k_ref[j * kb:(j + 1) * kb, :]) for j in range(nblk)]
            col_chunk = ((nblk - 1) * kb + col_in_block) // ATTN_CHUNK
            s[-1] = jnp.where(col_chunk <= row_chunk, s[-1], -jnp.inf)
            m = jnp.max(s[0], axis=-1, keepdims=True)
            for sj in s[1:]:
                m = jnp.maximum(m, jnp.max(sj, axis=-1, keepdims=True))
            l = jnp.zeros_like(m)
            acc = jnp.zeros((tq, DIFF_V_DIM), F32)
            for j, sj in enumerate(s):
                p = jnp.exp(sj - m)
                l = l + jnp.sum(p, axis=-1, keepdims=True)
                acc = acc + _dot(p.astype(BF16), v_ref[j * kb:(j + 1) * kb, :])
            outs.append(acc / l)
        o = outs[0] - lam * outs[1]
        ms = jnp.mean(o * o, axis=-1, keepdims=True)
        o = o * lax.rsqrt(ms + NORM_EPS) * sg_ref[...] * (1.0 - lambda_init)
        o_ref[...] = o.astype(o_ref.dtype)

    for nblk in range(1, seq // kb + 1):
        pl.when(i // 2 == nblk - 1)(functools.partial(attend, nblk))


def _attn_call(q, k, v, lq1, lk1, lq2, lk2, subln_g, lambda_init, batch, seq, *, tq=256):
    t = q.shape[0]
    nq = seq // tq
    assert seq % (2 * tq) == 0
    vec = lambda a: a.reshape(1, -1)
    small = lambda a: pl.BlockSpec(a.shape, lambda b, h, i: (0, 0))
    kv = pl.BlockSpec((seq, DIFF_V_DIM), lambda b, h, i: (b, h))
    args = [vec(lq1), vec(lk1), vec(lq2), vec(lk2), vec(subln_g)]
    return pl.pallas_call(
        functools.partial(_attn_kernel, tq=tq, seq=seq, lambda_init=lambda_init),
        grid=(batch, DIFF_HEADS, nq),
        in_specs=[pl.BlockSpec((tq, DIFF_V_DIM), lambda b, h, i: (b * nq + i, h)), kv, kv]
                 + [small(a) for a in args],
        out_specs=pl.BlockSpec((tq, DIFF_V_DIM), lambda b, h, i: (b * nq + i, h)),
        out_shape=jax.ShapeDtypeStruct((t, DIFF_HEADS * DIFF_V_DIM), BF16),
        compiler_params=_cparams(("arbitrary", "arbitrary", "arbitrary")),
        name="diff_attn",
    )(q, k, v, *args)


def _peer_route_kernel(q_ref, sk_ref, r2_ref, e2_ref, n1_ref, f_ref, a_ref, b_ref, cand_ref, *, tt):
    neg = -jnp.inf
    k = PEER_TOPK
    kf = float(k)
    key_id = lax.broadcasted_iota(jnp.int32, (PEER_KEYS, LANES), 0).astype(F32)
    cand_id = lax.broadcasted_iota(jnp.int32, (k * k, LANES), 0).astype(F32)

    def extract(w, ids, dst_ref, by_index, want_rank):
        rank = jnp.full(w.shape, kf, F32) if want_rank else None
        tops = []
        for r in range(k):
            mx = jnp.max(w, axis=0, keepdims=True)
            tops.append(mx)
            if dst_ref is not None:
                dst_ref[r:r + 1, :] = mx
            hit = w == mx
            if by_index:
                first = jnp.min(jnp.where(hit, ids, float(w.shape[0])), axis=0, keepdims=True)
                hit = ids == first
            if want_rank:
                rank = jnp.where(hit, float(r), rank)
            w = jnp.where(hit, neg, w)
        return rank, tops, w

    def candidates(ga_ref, gb_ref):
        bvals = gb_ref[...]
        for x in range(k):
            cand_ref[x * k:(x + 1) * k, :] = ga_ref[x:x + 1, :] + bvals
        return cand_ref[...]

    def staircase_candidates(ga_ref, gb_ref):
        a = lambda lo, hi: ga_ref[lo:hi, :]
        b = lambda lo, hi: gb_ref[lo:hi, :]
        return jnp.concatenate([
            a(0, 1) + b(0, 8), a(0, 1) + b(8, 16), a(1, 2) + b(0, 8), a(2, 3) + b(0, 8), a(3, 4) + b(0, 8),
            a(0, 8) + b(0, 1), a(8, 16) + b(0, 1), a(0, 8) + b(1, 2), a(0, 8) + b(2, 3)], axis=0)

    def twin_bf16(v):
        bits = lax.bitcast_convert_type(v.astype(BF16).astype(F32), jnp.uint32)
        return bits | (bits >> 16)

    def emit(cols, s1, s2, rank2, n1, a0, b0, c_tops):
        zsum = jnp.zeros((1, LANES), F32)
        for c in c_tops:
            zsum = zsum + jnp.exp(c - c_tops[0])
        r2_ref[0, 0, :, cols] = pltpu.bitcast(rank2.astype(BF16), jnp.uint32)
        e2_ref[0, 0, :, cols] = pltpu.bitcast(jnp.exp(s2 - b0).astype(BF16), jnp.uint32)
        n1_ref[0, 0, :, cols] = twin_bf16(n1)
        f_ref[0, 0, :, cols] = twin_bf16(jnp.exp(s1 - a0) / zsum)

    count = lambda m: jnp.sum(jnp.where(m, 1.0, 0.0), axis=0, keepdims=True)

    def scores(cols):
        return (_nt_dot(sk_ref[0], q_ref[cols, 0:PEER_HALF], HIGHEST),
                _nt_dot(sk_ref[1], q_ref[cols, PEER_HALF:2 * PEER_HALF], HIGHEST))

    groups = [slice(g * LANES, (g + 1) * LANES) for g in range(tt // LANES)]
    any_tied = []
    for gidx, cols in enumerate(groups):
        ga_ref, gb_ref = a_ref.at[gidx], b_ref.at[gidx]
        s1, s2 = scores(cols)
        _, a_tops, w1 = extract(s1, key_id, ga_ref, False, False)
        rank2, b_tops, _ = extract(s2, key_id, gb_ref, False, True)
        _, c_tops, _ = extract(staircase_candidates(ga_ref, gb_ref), None, None, False, False)
        tau = c_tops[k - 1]
        n1 = jnp.zeros_like(s1)
        for y in range(k):
            n1 = n1 + jnp.where((s1 + b_tops[y]) >= tau, 1.0, 0.0)
        emit(cols, s1, s2, rank2, n1, a_tops[0], b_tops[0], c_tops)
        tied = ((count(w1 == neg) != kf) | (count(rank2 < kf) != kf)
                | (jnp.sum(n1, axis=0, keepdims=True) != kf))
        any_tied.append(jnp.max(jnp.where(tied, 1.0, 0.0)) > 0.0)

    for gidx, cols in enumerate(groups):
        @pl.when(any_tied[gidx])
        def _():
            ga_ref, gb_ref = a_ref.at[gidx], b_ref.at[gidx]
            s1, s2 = scores(cols)
            rank1, a_x, _ = extract(s1, key_id, ga_ref, True, True)
            rank2x, b_x, _ = extract(s2, key_id, gb_ref, True, True)
            rank3, c_x, _ = extract(candidates(ga_ref, gb_ref), cand_id, None, True, True)
            win = rank3 < kf
            n1x = jnp.zeros_like(s1)
            for x in range(k):
                n_x = count(win[x * k:(x + 1) * k])
                n1x = n1x + jnp.where(rank1 == float(x), n_x, 0.0)
            emit(cols, s1, s2, rank2x, n1x, a_x[0], b_x[0], c_x)


def _peer_route_call(q, subkeys, *, tt):
    t = q.shape[0]
    nt = t // tt
    spec = pl.BlockSpec((1, 1, PEER_KEYS, tt), lambda i, h: (h, i, 0, 0))
    pair_spec = pl.BlockSpec((1, 1, PEER_KEYS // 2, tt), lambda i, h: (h, i, 0, 0))
    tab = lambda rows: jax.ShapeDtypeStruct((PEER_HEADS, nt, rows, tt), jnp.uint32)
    return pl.pallas_call(
        functools.partial(_peer_route_kernel, tt=tt),
        grid=(nt, PEER_HEADS),
        in_specs=[
            pl.BlockSpec((tt, 2 * PEER_HALF), lambda i, h: (i, h)),
            pl.BlockSpec(subkeys.shape, lambda i, h: (0, 0, 0)),
        ],
        out_specs=[pair_spec, pair_spec, spec, spec],
        out_shape=[tab(PEER_KEYS // 2), tab(PEER_KEYS // 2), tab(PEER_KEYS), tab(PEER_KEYS)],
        scratch_shapes=[
            pltpu.VMEM((tt // LANES, PEER_TOPK, LANES), F32),
            pltpu.VMEM((tt // LANES, PEER_TOPK, LANES), F32),
            pltpu.VMEM((PEER_TOPK * PEER_TOPK, LANES), F32),
        ],
        compiler_params=_cparams(("arbitrary", "arbitrary")),
        name="peer_route",
    )(q, subkeys)


def _gelu(x):
    return 0.5 * x * (1.0 + lax.erf(x * (1.0 / math.sqrt(2.0))))


def _peer_expert_kernel(x_ref, gate_ref, ht_ref, u_ref, vt_ref, n1_ref, f_ref, r2_ref, e2_ref,
                        o_ref, pre_a, pre_b, wa_ref, acc_ref, *, rows_per_tile):
    s = pl.program_id(1)
    te, tt = pre_a.shape
    d = acc_ref.shape[0]
    blk = 16
    rows_per_piece = 2

    @pl.when(s == 0)
    def _():
        acc_ref[...] = jnp.zeros_like(acc_ref)
        pre_b[...] = jnp.zeros_like(pre_b)

    def step(pre_new, pre_old):
        halves = [slice(m * d // 2, (m + 1) * d // 2) for m in range(2)]
        zero = jnp.zeros((blk, LANES), BF16)
        as_bf16 = lambda words: pltpu.bitcast(words, BF16)
        for il in range(rows_per_tile):
            for lt in range(tt // LANES):
                lanes = slice(lt * LANES, (lt + 1) * LANES)
                row = lambda ref, h: as_bf16(jnp.broadcast_to(ref[h, 0, il:il + 1, lanes], (blk // 2, LANES)))
                n1b = [row(n1_ref, h) for h in range(PEER_HEADS)]
                fb = [row(f_ref, h) for h in range(PEER_HEADS)]
                for jb in range(PEER_KEYS // blk):
                    words = slice(jb * blk // 2, (jb + 1) * blk // 2)
                    w = zero
                    for h in range(PEER_HEADS):
                        hit = as_bf16(r2_ref[h, 0, words, lanes]) < n1b[h]
                        w = w + jnp.where(hit, as_bf16(e2_ref[h, 0, words, lanes]), zero) * fb[h]
                    rows = slice(il * PEER_KEYS + jb * blk, il * PEER_KEYS + (jb + 1) * blk)
                    wa_ref[rows, lanes] = w
        n_pre = te // 512
        for m in range(n_pre):
            rows = slice(m * te // n_pre, (m + 1) * te // n_pre)
            pairs = slice(rows.start // 2, rows.stop // 2)
            pre_new[rows, :] = _dot(as_bf16(u_ref[pairs, :]), ht_ref[...])
        piece_rows = rows_per_piece * PEER_KEYS
        for p in range(te // piece_rows):
            piece = slice(p * piece_rows, (p + 1) * piece_rows)
            for r0 in range(p * piece_rows, (p + 1) * piece_rows, blk):
                rows = slice(r0, r0 + blk)
                wa_ref[rows, :] = wa_ref[rows, :] * _gelu(pre_old[rows, :]).astype(BF16)
            for rows in halves:
                pairs = slice(rows.start // 2, rows.stop // 2)
                acc_ref[rows, :] += _dot(as_bf16(vt_ref[pairs, piece]), wa_ref[piece, :])

    @pl.when(s % 2 == 0)
    def _():
        step(pre_a, pre_b)

    @pl.when(s % 2 == 1)
    def _():
        step(pre_b, pre_a)

    @pl.when(s == pl.num_programs(1) - 1)
    def _():
        o_ref[...] = x_ref[...] + gate_ref[0] * acc_ref[...].T


def _peer_expert_call(x, gate, hb, u_b, vt_b, n1, f, r2, e2, seq, *, tt, te=1024):
    t, d = x.shape
    ne = 2 * u_b.shape[0] // te
    rows_per_tile = te // PEER_KEYS
    lag = lambda s, n: jnp.clip(s - n, 0, ne - 1)
    sel = pl.BlockSpec((PEER_HEADS, 1, rows_per_tile, tt), lambda i, s: (0, i, lag(s, 1), 0))
    full = pl.BlockSpec((PEER_HEADS, 1, PEER_KEYS // 2, tt), lambda i, s: (0, i, 0, 0))
    return pl.pallas_call(
        functools.partial(_peer_expert_kernel, rows_per_tile=rows_per_tile),
        grid=(t // tt, ne + 1),
        in_specs=[
            pl.BlockSpec((tt, d), lambda i, s: (i, 0)),
            pl.BlockSpec((1, 1, d), lambda i, s: ((i * tt) // seq, 0, 0)),
            pl.BlockSpec((d, tt), lambda i, s: (0, i)),
            pl.BlockSpec((te // 2, d), lambda i, s: (lag(s, 0), 0)),
            pl.BlockSpec((d // 2, te), lambda i, s: (0, lag(s, 1))),
            sel, sel, full, full,
        ],
        out_specs=pl.BlockSpec((tt, d), lambda i, s: (i, 0)),
        out_shape=jax.ShapeDtypeStruct((t, d), F32),
        scratch_shapes=[
            pltpu.VMEM((te, tt), F32),
            pltpu.VMEM((te, tt), F32),
            pltpu.VMEM((te, tt), BF16),
            pltpu.VMEM((d, tt), F32),
        ],
        compiler_params=_cparams(("arbitrary", "arbitrary")),
        name="peer_experts",
    )(x, gate, hb, u_b, vt_b, n1, f, r2, e2)


def _final_norm_kernel(x_ref, g_ref, o_ref):
    x = x_ref[...]
    ms = jnp.mean(x * x, axis=-1, keepdims=True)
    o_ref[...] = x * lax.rsqrt(ms + NORM_EPS) * g_ref[...]


def _final_norm_call(x, g, *, tm=512):
    t, d = x.shape
    return pl.pallas_call(
        _final_norm_kernel,
        grid=(t // tm,),
        in_specs=[pl.BlockSpec((tm, d), lambda i: (i, 0)), pl.BlockSpec((1, d), lambda i: (0, 0))],
        out_specs=pl.BlockSpec((tm, d), lambda i: (i, 0)),
        out_shape=jax.ShapeDtypeStruct((t, d), F32),
        compiler_params=_cparams(("arbitrary",)),
        name="final_norm",
    )(x, g.reshape(1, d))


def _pack_row_pairs(w):
    bits = lax.bitcast_convert_type(w.astype(BF16), jnp.uint16).astype(jnp.uint32)
    return bits[0::2] | (bits[1::2] << 16)


def _pad_rows(a, rows):
    return jnp.pad(a, ((0, rows - a.shape[0]), (0, 0)))


def _pad_cols(a, cols):
    return jnp.pad(a, ((0, 0), (0, cols - a.shape[1])))


def _hybrid_mixer(x, g, shift, scale, batch, seq, w_in, conv_w, conv_b, conv_ln_g, conv_ln_b, mu, w0, w2,
                  a0, a2, g2, k_k, k_a, r_k, ln_g, ln_b):
    rw = RWKV_WIDTH
    c0 = 2 * CONV_WIDTH
    lora0 = c0 + 3 * rw
    l1, l2 = lora0 + DECAY_LORA, lora0 + DECAY_LORA + AAA_LORA
    w_conv = w_in[:, :c0].astype(BF16)
    w_rkv = w_in[:, c0:lora0].astype(BF16)
    w_lora = jnp.concatenate([
        _pad_cols(w_in[:, lora0:l1], LANES), _pad_cols(w_in[:, l1:l2], LANES),
        _pad_cols(w_in[:, l2:], 2 * LANES)], axis=1).astype(BF16)
    zc, zr, zl = _norm_matmul_call(x, g, shift, scale, [w_conv, w_rkv, w_lora], [F32, F32, F32], seq,
                                   name="hybrid_in_proj")
    ya = _conv_call(zc, conv_w, conv_b, conv_ln_g, conv_ln_b, batch, seq)
    mu_l = jnp.concatenate([
        _pad_cols(mu[None, lora0 - c0:l1 - c0], LANES), _pad_cols(mu[None, l1 - c0:l2 - c0], LANES),
        _pad_cols(mu[None, l2 - c0:], 2 * LANES)], axis=1)
    vec = lambda a: a.reshape(1, rw)
    head_id = jnp.arange(RWKV_GROUP * RWKV_HEAD) // RWKV_HEAD
    params = dict(
        mu_r=mu[None, :3 * rw], mu_l=mu_l, w0=vec(w0), a0=vec(a0), k_k=vec(k_k), k_a=vec(k_a),
        r_k=vec(r_k), ln_g=vec(ln_g), ln_b=vec(ln_b),
        w2=_pad_rows(w2, LANES).astype(BF16), a2=_pad_rows(a2, LANES).astype(BF16),
        g2=_pad_rows(g2, 2 * LANES).astype(BF16),
        seg=(head_id[:, None] == head_id[None, :]).astype(BF16))
    yb = _rwkv_call(zr, zl, params, batch, seq)
    return ya, yb


def kernel(x, c, positions, ada_w, ada_b, norm_mix_g, norm_ffn_g, hyb_w_in, conv_w, conv_b, conv_ln_g, conv_ln_b, rwkv_mu, rwkv_w0, rwkv_w2, rwkv_a0, rwkv_a2, rwkv_g2, rwkv_k_k, rwkv_k_a, rwkv_r_k, rwkv_ln_g, rwkv_ln_b, hyb_w_out, diff_w_qkv, diff_lq1, diff_lk1, diff_lq2, diff_lk2, diff_subln_g, diff_w_out, peer_w_q, peer_subkeys, peer_u, peer_v, final_g):
    batch, seq, d = x.shape
    depth = ada_w.shape[0]
    t = batch * seq
    xt = x.reshape(t, d)
    pos = positions.reshape(t, 1)
    mod = _ada_call(c, ada_w, ada_b).reshape(depth, batch, 6, 1, d)

    for l in range(depth):
        sh1, sc1, g1, sh2, sc2, g2 = (mod[l, :, j] for j in range(6))
        gmix = norm_mix_g[l].reshape(1, d)
        if l % 2 == 0:
            e = l // 2
            ya, yb = _hybrid_mixer(
                xt, gmix, sh1, sc1, batch, seq, hyb_w_in[e], conv_w[e], conv_b[e], conv_ln_g[e],
                conv_ln_b[e], rwkv_mu[e], rwkv_w0[e], rwkv_w2[e], rwkv_a0[e], rwkv_a2[e], rwkv_g2[e],
                rwkv_k_k[e], rwkv_k_a[e], rwkv_r_k[e], rwkv_ln_g[e], rwkv_ln_b[e])
            w_out = hyb_w_out[e].astype(BF16)
            xt = _out_proj_call(xt, g1, [ya.astype(BF16), yb.astype(BF16)],
                                [w_out[:CONV_WIDTH], w_out[CONV_WIDTH:]], seq)
        else:
            o = l // 2
            lambda_init = 0.8 - 0.6 * math.exp(-0.3 * l)
            wqkv = diff_w_qkv[o].astype(BF16)
            n = wqkv.shape[1] // 3
            q, k, v = _qkv_rope_call(xt, gmix, sh1, sc1, pos, wqkv[:, :n], wqkv[:, n:2 * n], wqkv[:, 2 * n:],
                                     seq)
            y = _attn_call(q, k, v, diff_lq1[o], diff_lk1[o], diff_lq2[o], diff_lk2[o], diff_subln_g[o],
                           lambda_init, batch, seq)
            xt = _out_proj_call(xt, g1, [y], [diff_w_out[o].astype(BF16)], seq)

        q, hb = _norm_matmul_call(xt, norm_ffn_g[l].reshape(1, d), sh2, sc2, [peer_w_q[l]], [F32], seq,
                                  split=True, emit_h=True, name="peer_query")
        r2, e2, n1, f = _peer_route_call(q, peer_subkeys[l], tt=PEER_TOKEN_TILE)
        xt = _peer_expert_call(xt, g2, hb, _pack_row_pairs(peer_u[l]), _pack_row_pairs(peer_v[l].T),
                               n1, f, r2, e2, seq, tt=PEER_TOKEN_TILE)

    return _final_norm_call(xt, final_g).reshape(batch, seq, d)
```

```python
import functools
import math

import jax
import jax.numpy as jnp
from jax import lax
from jax.experimental import pallas as pl
from jax.experimental.pallas import tpu as pltpu

F32 = jnp.float32
BF16 = jnp.bfloat16
HIGHEST = lax.Precision.HIGHEST

CONV_WIDTH = 512
CONV_KERNEL = 31
RWKV_WIDTH = 512
RWKV_HEAD = 64
RWKV_HEADS = 8
DECAY_LORA = 64
AAA_LORA = 64
GATE_LORA = 160
DIFF_HEADS = 8
DIFF_HEAD_DIM = 64
DIFF_V_DIM = 128
ROPE_DIM = 16
ROPE_THETA = 500000.0
ATTN_CHUNK = 64
PEER_HEADS = 8
PEER_KEYS = 128
PEER_HALF = 128
PEER_TOPK = 16
NORM_EPS = 1e-6
LN_EPS = 1e-5
RWKV_GN_EPS = 64e-5

LANES = 128
SCAN_CHUNK = 64
CONV_HALO = 32
PEER_EXPERT_TILE = 1024
PEER_TOKEN_TILE = 256
VMEM_LIMIT = 48 * 1024 * 1024


def _cparams(semantics, flags=None):
    return pltpu.CompilerParams(dimension_semantics=semantics, vmem_limit_bytes=VMEM_LIMIT, flags=flags)


def _nt_dot(a, b, precision=None):
    return lax.dot_general(a, b, (((1,), (1,)), ((), ())), precision=precision,
                           preferred_element_type=F32)


def _dot(a, b, precision=None):
    return jnp.dot(a, b, precision=precision, preferred_element_type=F32)


def _sigmoid(x):
    return 1.0 / (1.0 + jnp.exp(-x))


def _ada_kernel(c_ref, w_ref, b_ref, o_ref):
    c = c_ref[...]
    cond = c * _sigmoid(c)
    o_ref[0] = _dot(cond, w_ref[0], HIGHEST) + b_ref[0]


def _ada_call(c, ada_w, ada_b):
    depth, d, n = ada_w.shape
    b = c.shape[0]
    tn = 1536
    return pl.pallas_call(
        _ada_kernel,
        grid=(depth, n // tn),
        in_specs=[
            pl.BlockSpec((b, d), lambda l, j: (0, 0)),
            pl.BlockSpec((1, d, tn), lambda l, j: (l, 0, j)),
            pl.BlockSpec((1, 1, tn), lambda l, j: (l, 0, j)),
        ],
        out_specs=pl.BlockSpec((1, b, tn), lambda l, j: (l, 0, j)),
        out_shape=jax.ShapeDtypeStruct((depth, b, n), F32),
        compiler_params=_cparams(("arbitrary", "arbitrary")),
        name="ada_mod",
    )(c, ada_w, ada_b.reshape(depth, 1, n))


def _modulate(x, g, shift, scale):
    ms = jnp.mean(x * x, axis=-1, keepdims=True)
    y = x * lax.rsqrt(ms + NORM_EPS)
    return (y * g) * (1.0 + scale) + shift


def _split_bf16(a):
    hi = a.astype(BF16)
    return hi, (a - hi.astype(F32)).astype(BF16)


def _norm_matmul_kernel(*refs, n_w, split, emit_h):
    x_ref, g_ref, sh_ref, sc_ref = refs[:4]
    w_refs = refs[4:4 + n_w]
    o_refs = refs[4 + n_w:]
    h = _modulate(x_ref[...], g_ref[...], sh_ref[0], sc_ref[0])
    hb = h.astype(BF16)
    if split:
        h_lo = (h - hb.astype(F32)).astype(BF16)
        for k in range(n_w // 2):
            w_hi, w_lo = w_refs[2 * k][...], w_refs[2 * k + 1][...]
            o_refs[k][...] = (_dot(hb, w_hi) + _dot(h_lo, w_hi) + _dot(hb, w_lo)).astype(o_refs[k].dtype)
    else:
        for w_ref, o_ref in zip(w_refs, o_refs[:n_w]):
            o_ref[...] = _dot(hb, w_ref[...]).astype(o_ref.dtype)
    if emit_h:
        o_refs[-1][...] = h.T.astype(BF16)


def _norm_matmul_call(x, g, shift, scale, weights, out_dtypes, seq, *, split=False, emit_h=False,
                      tm=512, name="norm_matmul"):
    t, d = x.shape
    if split:
        weights = [part for w in weights for part in _split_bf16(w)]
    in_specs = [
        pl.BlockSpec((tm, d), lambda i: (i, 0)),
        pl.BlockSpec((1, d), lambda i: (0, 0)),
        pl.BlockSpec((1, 1, d), lambda i: ((i * tm) // seq, 0, 0)),
        pl.BlockSpec((1, 1, d), lambda i: ((i * tm) // seq, 0, 0)),
    ]
    out_specs, out_shape = [], []
    for w in weights:
        in_specs.append(pl.BlockSpec(w.shape, lambda i: (0, 0)))
    for w, dt in zip(weights[::2] if split else weights, out_dtypes):
        n = w.shape[1]
        out_specs.append(pl.BlockSpec((tm, n), lambda i: (i, 0)))
        out_shape.append(jax.ShapeDtypeStruct((t, n), dt))
    if emit_h:
        out_specs.append(pl.BlockSpec((d, tm), lambda i: (0, i)))
        out_shape.append(jax.ShapeDtypeStruct((d, t), BF16))
    return pl.pallas_call(
        functools.partial(_norm_matmul_kernel, n_w=len(weights), split=split, emit_h=emit_h),
        grid=(t // tm,),
        in_specs=in_specs,
        out_specs=out_specs,
        out_shape=out_shape,
        compiler_params=_cparams(("arbitrary",)),
        name=name,
    )(x, g, shift, scale, *weights)


def _out_proj_kernel(*refs, n_y):
    x_ref, gate_ref = refs[:2]
    y_refs = refs[2:2 + n_y]
    w_refs = refs[2 + n_y:2 + 2 * n_y]
    o_ref = refs[2 + 2 * n_y]
    acc = _dot(y_refs[0][...], w_refs[0][...])
    for y_ref, w_ref in zip(y_refs[1:], w_refs[1:]):
        acc = acc + _dot(y_ref[...], w_ref[...])
    o_ref[...] = x_ref[...] + gate_ref[0] * acc


def _out_proj_call(x, gate, ys, ws, seq, *, tm=512):
    t, d = x.shape
    in_specs = [
        pl.BlockSpec((tm, d), lambda i: (i, 0)),
        pl.BlockSpec((1, 1, d), lambda i: ((i * tm) // seq, 0, 0)),
    ]
    for y in ys:
        in_specs.append(pl.BlockSpec((tm, y.shape[1]), lambda i: (i, 0)))
    for w in ws:
        in_specs.append(pl.BlockSpec(w.shape, lambda i: (0, 0)))
    return pl.pallas_call(
        functools.partial(_out_proj_kernel, n_y=len(ys)),
        grid=(t // tm,),
        in_specs=in_specs,
        out_specs=pl.BlockSpec((tm, d), lambda i: (i, 0)),
        out_shape=jax.ShapeDtypeStruct((t, d), F32),
        compiler_params=_cparams(("arbitrary",)),
        name="out_proj",
    )(x, gate, *ys, *ws)


def _conv_kernel(z_ref, w_ref, b_ref, g_ref, beta_ref, o_ref, ext_ref, *, ts):
    width = CONV_WIDTH

    @pl.when(pl.program_id(1) == 0)
    def _():
        ext_ref[0:CONV_HALO, :] = jnp.zeros((CONV_HALO, width), F32)

    z = z_ref[...]
    u = z[:, :width] * _sigmoid(z[:, width:])
    ext_ref[CONV_HALO:CONV_HALO + ts, :] = u
    base = CONV_HALO - (CONV_KERNEL - 1)
    rows = 64
    for cb in range(width // LANES):
        cs = slice(cb * LANES, (cb + 1) * LANES)
        for rb in range(ts // rows):
            acc = jnp.zeros((rows, LANES), F32)
            for j in range(CONV_KERNEL):
                start = rb * rows + base + j
                acc = acc + w_ref[j:j + 1, cs] * ext_ref[start:start + rows, cs]
            o_ref[rb * rows:(rb + 1) * rows, cs] = acc
    conv = o_ref[...] + b_ref[...]
    mu = jnp.mean(conv, axis=-1, keepdims=True)
    dlt = conv - mu
    var = jnp.mean(dlt * dlt, axis=-1, keepdims=True)
    y = dlt * lax.rsqrt(var + LN_EPS) * g_ref[...] + beta_ref[...]
    o_ref[...] = y * _sigmoid(y)
    ext_ref[0:CONV_HALO, :] = ext_ref[ts:ts + CONV_HALO, :]


def _conv_call(z, conv_w, conv_b, ln_g, ln_b, batch, seq, *, ts=256):
    t = z.shape[0]
    width = CONV_WIDTH
    nts = seq // ts
    vec = lambda a: a.reshape(1, width)
    return pl.pallas_call(
        functools.partial(_conv_kernel, ts=ts),
        grid=(batch, nts),
        in_specs=[
            pl.BlockSpec((ts, 2 * width), lambda b, i: (b * nts + i, 0)),
            pl.BlockSpec((CONV_KERNEL, width), lambda b, i: (0, 0)),
            pl.BlockSpec((1, width), lambda b, i: (0, 0)),
            pl.BlockSpec((1, width), lambda b, i: (0, 0)),
            pl.BlockSpec((1, width), lambda b, i: (0, 0)),
        ],
        out_specs=pl.BlockSpec((ts, width), lambda b, i: (b * nts + i, 0)),
        out_shape=jax.ShapeDtypeStruct((t, width), F32),
        scratch_shapes=[pltpu.VMEM((ts + CONV_HALO, width), F32)],
        compiler_params=_cparams(("arbitrary", "arbitrary")),
        name="conformer_conv",
    )(z, conv_w, vec(conv_b), vec(ln_g), vec(ln_b))


def _softplus(x):
    return jnp.maximum(x, 0.0) + jnp.log(1.0 + jnp.exp(-jnp.abs(x)))


def _unit_lower_inverses(mats, row, col, size):
    eye = (row == col).astype(F32)
    blk = lambda m: (row // m) == (col // m)
    bdot = lambda p, q: _dot(p.astype(BF16), q.astype(BF16))
    n1 = [jnp.where(blk(8), a, 0.0) for a in mats]
    t = [eye + x for x in n1]
    n2 = [bdot(x, x) for x in n1]
    t = [x + bdot(x, y) for x, y in zip(t, n2)]
    n4 = [bdot(x, x) for x in n2]
    t = [x + bdot(x, y) for x, y in zip(t, n4)]
    m = 8
    while m < size:
        new = blk(2 * m) & jnp.logical_not(blk(m))
        et = [bdot(jnp.where(new, a, 0.0), x) for a, x in zip(mats, t)]
        t = [x + bdot(x, y) for x, y in zip(t, et)]
        m *= 2
    return t


def _split_dot(x, w_bf16, parts):
    acc = None
    for _ in range(parts):
        piece = x.astype(BF16)
        term = _dot(piece, w_bf16)
        acc = term if acc is None else acc + term
        x = x - piece.astype(F32)
    return acc


RWKV_GROUP = 4
RWKV_BATCH_TILE = 2


def _rwkv_kernel(zr_ref, zl_ref, mur_ref, mul_ref, w0_ref, a0_ref, kk_ref, ka_ref, rk_ref,
                 lng_ref, lnb_ref, w2_ref, a2_ref, g2_ref, seg_ref, o_ref,
                 extr_ref, extl_ref, state_ref):
    c = SCAN_CHUNK
    rw = RWKV_WIDTH
    n = RWKV_HEAD
    nb = RWKV_BATCH_TILE
    gw = RWKV_GROUP * n
    groups = rw // gw

    @pl.when(pl.program_id(1) == 0)
    def _():
        extr_ref[:, 0:8, :] = jnp.zeros((nb, 8, 3 * rw), F32)
        extl_ref[:, 0:8, :] = jnp.zeros((nb, 8, rw), F32)
        state_ref[...] = jnp.zeros_like(state_ref)

    zr_rows, zl_rows = [], []
    for b in range(nb):
        zr = zr_ref[b]
        zl = zl_ref[b]
        extr_ref[b, 8:8 + c, :] = zr
        extl_ref[b, 8:8 + c, :] = zl
        zr_rows.append(zr + (extr_ref[b, 7:7 + c, :] - zr) * mur_ref[...])
        zl_rows.append(zl + (extl_ref[b, 7:7 + c, :] - zl) * mul_ref[...])
        extr_ref[b, 0:8, :] = extr_ref[b, c:c + 8, :]
        extl_ref[b, 0:8, :] = extl_ref[b, c:c + 8, :]
    zr = jnp.concatenate(zr_rows, axis=0)
    zl = jnp.concatenate(zl_rows, axis=0)

    r = zr[:, 0:rw]
    k = zr[:, rw:2 * rw]
    v = zr[:, 2 * rw:3 * rw]
    wd = zl[:, 0:LANES]
    ad = zl[:, LANES:2 * LANES]
    gd = zl[:, 2 * LANES:4 * LANES]
    seg = seg_ref[...]

    def segsum(x, parts):
        return jnp.concatenate([_split_dot(x[:, g * gw:(g + 1) * gw], seg, parts) for g in range(groups)],
                               axis=1)

    w_raw = -_softplus(-(w0_ref[...] + _dot(jnp.tanh(wd).astype(BF16), w2_ref[...]))) - 0.5
    lw = -jnp.exp(w_raw)
    alpha = _sigmoid(a0_ref[...] + _dot(ad.astype(BF16), a2_ref[...]))
    gate = _dot(_sigmoid(gd).astype(BF16), g2_ref[...])
    kkf = k * kk_ref[...]
    kk = kkf / jnp.maximum(jnp.sqrt(segsum(kkf * kkf, 3)), 1e-12)
    kp = k * (1.0 + (alpha - 1.0) * ka_ref[...])
    bonus = segsum(r * kp * rk_ref[...], 3) * v
    av = -kk
    bv = kk * alpha

    trow = lax.broadcasted_iota(jnp.int32, (nb * c, nb * c), 0)
    tcol = lax.broadcasted_iota(jnp.int32, (nb * c, nb * c), 1)
    same_seq_incl = ((trow // c) == (tcol // c)) & (tcol <= trow)
    cum = _split_dot_lhs_exact(jnp.where(same_seq_incl, 1.0, 0.0).astype(BF16), lw)
    tot = jnp.concatenate([jnp.broadcast_to(cum[(b + 1) * c - 1:(b + 1) * c, :], (c, rw)) for b in range(nb)],
                          axis=0)
    g_in = jnp.exp(cum)
    g_inp = jnp.exp(cum - lw)
    g_out = jnp.exp(-cum)
    g_end = jnp.exp(tot - cum)
    g_tot = jnp.exp(tot)
    a_in = av * g_inp
    r_in = r * g_in
    b_out = bv * g_out
    k_out = kp * g_out
    b_end = bv * g_end
    k_end = kp * g_end

    row = lax.broadcasted_iota(jnp.int32, (gw, gw), 0)
    col = lax.broadcasted_iota(jnp.int32, (gw, gw), 1)
    same_head = (row // n) == (col // n)
    strict = same_head & ((col % n) < (row % n))
    incl = same_head & ((col % n) <= (row % n))

    def expand(x):
        return jnp.where(same_head, jnp.concatenate([x] * RWKV_GROUP, axis=0), 0.0)

    bf = lambda x: x.astype(BF16)
    chains = [(b, g) for b in range(nb) for g in range(groups)]
    pick = lambda x: [expand(x[b * c:(b + 1) * c, g * gw:(g + 1) * gw]) for b, g in chains]
    each = lambda fn, *lists: [fn(*args) for args in zip(*lists)]
    a_bd, r_bd = each(bf, pick(a_in)), each(bf, pick(r_in))
    bo_bd, ko_bd = each(bf, pick(b_out)), each(bf, pick(k_out))
    v_f = pick(v)
    v_bd = each(bf, v_f)
    st = [state_ref[b, g] for b, g in chains]
    st_b = each(bf, st)
    a_ab = each(lambda p, q: jnp.where(strict, _nt_dot(p, q), 0.0), a_bd, bo_bd)
    a_ak = each(lambda p, q: bf(jnp.where(strict, _nt_dot(p, q), 0.0)), a_bd, ko_bd)
    a_rb = each(lambda p, q: bf(jnp.where(incl, _nt_dot(p, q), 0.0)), r_bd, bo_bd)
    a_rk = each(lambda p, q: bf(jnp.where(incl, _nt_dot(p, q), 0.0)), r_bd, ko_bd)
    ph_a = each(_nt_dot, a_bd, st_b)
    ph_r = each(_nt_dot, r_bd, st_b)
    tinv = _unit_lower_inverses(a_ab, row, col, n)
    rhs = each(lambda p, m, w: bf(p + _dot(m, w)), ph_a, a_ak, v_bd)
    u = each(lambda t, x: _dot(bf(t), x), tinv, rhs)
    y_bd = each(lambda p, m1, uu, m2, w: p + _dot(m1, bf(uu)) + _dot(m2, w), ph_r, a_rb, u, a_rk, v_bd)
    be_bd, ke_bd = each(bf, pick(b_end)), each(bf, pick(k_end))
    for (b, g), s0, uu, vf, be, ke in zip(chains, st, u, v_f, be_bd, ke_bd):
        state_ref[b, g] = (s0 * g_tot[b * c:b * c + 1, g * gw:(g + 1) * gw]
                           + _dot(bf(uu.T), be) + _dot(bf(vf.T), ke))
    y_g = [sum(m[i * c:(i + 1) * c] for i in range(RWKV_GROUP)) for m in y_bd]
    y = jnp.concatenate([jnp.concatenate(y_g[b * groups:(b + 1) * groups], axis=1) for b in range(nb)],
                        axis=0)

    inv_n = 1.0 / n
    mu_y = segsum(y, 3) * inv_n
    dy = y - mu_y
    var_y = segsum(dy * dy, 3) * inv_n
    yn = dy * lax.rsqrt(var_y + RWKV_GN_EPS)
    out = (yn * lng_ref[...] + lnb_ref[...] + bonus) * gate
    for b in range(nb):
        o_ref[b] = out[b * c:(b + 1) * c]


def _split_dot_lhs_exact(w_bf16, x):
    acc = None
    for _ in range(3):
        piece = x.astype(BF16)
        term = _dot(w_bf16, piece)
        acc = term if acc is None else acc + term
        x = x - piece.astype(F32)
    return acc


def _rwkv_call(zr, zl, p, batch, seq):
    c = SCAN_CHUNK
    rw = RWKV_WIDTH
    nb = RWKV_BATCH_TILE
    nc = seq // c
    gw = RWKV_GROUP * RWKV_HEAD
    const = lambda shape: pl.BlockSpec(shape, lambda b, i: (0,) * len(shape))
    vec = const((1, rw))
    out = pl.pallas_call(
        _rwkv_kernel,
        grid=(batch // nb, nc),
        in_specs=[
            pl.BlockSpec((nb, c, 3 * rw), lambda b, i: (b, i, 0)),
            pl.BlockSpec((nb, c, rw), lambda b, i: (b, i, 0)),
            const((1, 3 * rw)), vec,
            vec, vec, vec, vec, vec, vec, vec,
            const((LANES, rw)), const((LANES, rw)), const((2 * LANES, rw)),
            const((gw, gw)),
        ],
        out_specs=pl.BlockSpec((nb, c, rw), lambda b, i: (b, i, 0)),
        out_shape=jax.ShapeDtypeStruct((batch, seq, rw), F32),
        scratch_shapes=[
            pltpu.VMEM((nb, c + 8, 3 * rw), F32),
            pltpu.VMEM((nb, c + 8, rw), F32),
            pltpu.VMEM((nb, rw // gw, gw, gw), F32),
        ],
        compiler_params=_cparams(("arbitrary", "arbitrary")),
        name="rwkv7_mix",
    )(zr.reshape(batch, seq, 3 * rw), zl.reshape(batch, seq, rw), p["mu_r"], p["mu_l"], p["w0"], p["a0"],
      p["k_k"], p["k_a"], p["r_k"], p["ln_g"], p["ln_b"], p["w2"], p["a2"], p["g2"], p["seg"])
    return out.reshape(batch * seq, rw)


def _qkv_rope_kernel(x_ref, g_ref, sh_ref, sc_ref, pos_ref, inv_ref, sgn_ref, wq_ref, wk_ref, wv_ref,
                     q_ref, k_ref, v_ref):
    h = _modulate(x_ref[...], g_ref[...], sh_ref[0], sc_ref[0]).astype(BF16)
    ang = pos_ref[...].astype(F32) * inv_ref[...]
    cosf = jnp.cos(ang)
    sinf = jnp.sin(ang) * sgn_ref[...]
    lane = lax.broadcasted_iota(jnp.int32, ang.shape, 1)
    low = (lane % DIFF_HEAD_DIM) < (ROPE_DIM // 2)
    half = ROPE_DIM // 2

    def rope(w_ref, o_ref, scale):
        z = _dot(h, w_ref[...])
        for cb in range(z.shape[1] // LANES):
            zc = z[:, cb * LANES:(cb + 1) * LANES]
            partner = jnp.where(low, pltpu.roll(zc, LANES - half, axis=1), pltpu.roll(zc, half, axis=1))
            o_ref[:, cb * LANES:(cb + 1) * LANES] = ((zc * cosf + partner * sinf) * scale).astype(o_ref.dtype)

    rope(wq_ref, q_ref, DIFF_HEAD_DIM ** -0.5)
    rope(wk_ref, k_ref, 1.0)
    v_ref[...] = _dot(h, wv_ref[...]).astype(v_ref.dtype)


def _qkv_rope_call(x, g, shift, scale, pos, wq, wk, wv, seq, *, tm=512):
    t, d = x.shape
    half = ROPE_DIM // 2
    inv = 1.0 / (ROPE_THETA ** (jnp.arange(0, ROPE_DIM, 2, dtype=F32) / ROPE_DIM))
    dl = jnp.arange(LANES) % DIFF_HEAD_DIM
    inv_pat = jnp.where(dl < ROPE_DIM, inv[dl % half], 0.0).reshape(1, LANES).astype(F32)
    sgn_pat = jnp.where(dl < half, -1.0, 1.0).reshape(1, LANES).astype(F32)
    row = lambda n: pl.BlockSpec((tm, n), lambda i: (i, 0))
    full = lambda a: pl.BlockSpec(a.shape, lambda i: (0, 0))
    mod = pl.BlockSpec((1, 1, d), lambda i: ((i * tm) // seq, 0, 0))
    n = wq.shape[1]
    return pl.pallas_call(
        _qkv_rope_kernel,
        grid=(t // tm,),
        in_specs=[row(d), full(g), mod, mod, row(1), full(inv_pat), full(sgn_pat),
                  full(wq), full(wk), full(wv)],
        out_specs=[row(n), row(n), row(n)],
        out_shape=[jax.ShapeDtypeStruct((t, n), BF16)] * 3,
        compiler_params=_cparams(("arbitrary",)),
        name="qkv_rope",
    )(x, g, shift, scale, pos, inv_pat, sgn_pat, wq, wk, wv)


def _attn_kernel(q_ref, k_ref, v_ref, lq1_ref, lk1_ref, lq2_ref, lk2_ref, sg_ref, o_ref,
                 *, tq, seq, lambda_init):
    i = pl.program_id(2)
    kb = 2 * tq
    q = q_ref[...]
    lane = lax.broadcasted_iota(jnp.int32, q.shape, 1)
    zero = jnp.zeros_like(q)
    qm = (jnp.where(lane < DIFF_HEAD_DIM, q, zero), jnp.where(lane >= DIFF_HEAD_DIM, q, zero))
    f32sum = lambda a, b: jnp.sum(a[...] * b[...], axis=-1, keepdims=True)
    lam = jnp.exp(f32sum(lq1_ref, lk1_ref)) - jnp.exp(f32sum(lq2_ref, lk2_ref)) + lambda_init
    row_chunk = (i * tq + lax.broadcasted_iota(jnp.int32, (tq, kb), 0)) // ATTN_CHUNK
    col_in_block = lax.broadcasted_iota(jnp.int32, (tq, kb), 1)

    def attend(nblk):
        outs = []
        for mp in range(2):
            s = [_nt_dot(qm[mp], k_ref[j * kb:(j + 1) * kb, :]) for j in range(nblk)]
            col_chunk = ((nblk - 1) * kb + col_in_block) // ATTN_CHUNK
            s[-1] = jnp.where(col_chunk <= row_chunk, s[-1], -jnp.inf)
            m = jnp.max(s[0], axis=-1, keepdims=True)
            for sj in s[1:]:
                m = jnp.maximum(m, jnp.max(sj, axis=-1, keepdims=True))
            l = jnp.zeros_like(m)
            acc = jnp.zeros((tq, DIFF_V_DIM), F32)
            for j, sj in enumerate(s):
                p = jnp.exp(sj - m)
                l = l + jnp.sum(p, axis=-1, keepdims=True)
                acc = acc + _dot(p.astype(BF16), v_ref[j * kb:(j + 1) * kb, :])
            outs.append(acc / l)
        o = outs[0] - lam * outs[1]
        ms = jnp.mean(o * o, axis=-1, keepdims=True)
        o = o * lax.rsqrt(ms + NORM_EPS) * sg_ref[...] * (1.0 - lambda_init)
        o_ref[...] = o.astype(o_ref.dtype)

    for nblk in range(1, seq // kb + 1):
        pl.when(i // 2 == nblk - 1)(functools.partial(attend, nblk))


def _attn_call(q, k, v, lq1, lk1, lq2, lk2, subln_g, lambda_init, batch, seq, *, tq=256):
    t = q.shape[0]
    nq = seq // tq
    assert seq % (2 * tq) == 0
    vec = lambda a: a.reshape(1, -1)
    small = lambda a: pl.BlockSpec(a.shape, lambda b, h, i: (0, 0))
    kv = pl.BlockSpec((seq, DIFF_V_DIM), lambda b, h, i: (b, h))
    args = [vec(lq1), vec(lk1), vec(lq2), vec(lk2), vec(subln_g)]
    return pl.pallas_call(
        functools.partial(_attn_kernel, tq=tq, seq=seq, lambda_init=lambda_init),
        grid=(batch, DIFF_HEADS, nq),
        in_specs=[pl.BlockSpec((tq, DIFF_V_DIM), lambda b, h, i: (b * nq + i, h)), kv, kv]
                 + [small(a) for a in args],
        out_specs=pl.BlockSpec((tq, DIFF_V_DIM), lambda b, h, i: (b * nq + i, h)),
        out_shape=jax.ShapeDtypeStruct((t, DIFF_HEADS * DIFF_V_DIM), BF16),
        compiler_params=_cparams(("arbitrary", "arbitrary", "arbitrary")),
        name="diff_attn",
    )(q, k, v, *args)


def _peer_route_kernel(q_ref, sk_ref, r2_ref, e2_ref, n1_ref, f_ref, a_ref, b_ref, cand_ref, *, tt):
    neg = -jnp.inf
    k = PEER_TOPK
    kf = float(k)
    key_id = lax.broadcasted_iota(jnp.int32, (PEER_KEYS, LANES), 0).astype(F32)
    cand_id = lax.broadcasted_iota(jnp.int32, (k * k, LANES), 0).astype(F32)

    def extract(w, ids, dst_ref, by_index, want_rank):
        rank = jnp.full(w.shape, kf, F32) if want_rank else None
        tops = []
        for r in range(k):
            mx = jnp.max(w, axis=0, keepdims=True)
            tops.append(mx)
            if dst_ref is not None:
                dst_ref[r:r + 1, :] = mx
            hit = w == mx
            if by_index:
                first = jnp.min(jnp.where(hit, ids, float(w.shape[0])), axis=0, keepdims=True)
                hit = ids == first
            if want_rank:
                rank = jnp.where(hit, float(r), rank)
            w = jnp.where(hit, neg, w)
        return rank, tops, w

    def candidates(ga_ref, gb_ref):
        bvals = gb_ref[...]
        for x in range(k):
            cand_ref[x * k:(x + 1) * k, :] = ga_ref[x:x + 1, :] + bvals
        return cand_ref[...]

    def staircase_candidates(ga_ref, gb_ref):
        a = lambda lo, hi: ga_ref[lo:hi, :]
        b = lambda lo, hi: gb_ref[lo:hi, :]
        return jnp.concatenate([
            a(0, 1) + b(0, 8), a(0, 1) + b(8, 16), a(1, 2) + b(0, 8), a(2, 3) + b(0, 8), a(3, 4) + b(0, 8),
            a(0, 8) + b(0, 1), a(8, 16) + b(0, 1), a(0, 8) + b(1, 2), a(0, 8) + b(2, 3)], axis=0)

    def twin_bf16(v):
        bits = lax.bitcast_convert_type(v.astype(BF16).astype(F32), jnp.uint32)
        return bits | (bits >> 16)

    def emit(cols, s1, s2, rank2, n1, a0, b0, c_tops):
        zsum = jnp.zeros((1, LANES), F32)
        for c in c_tops:
            zsum = zsum + jnp.exp(c - c_tops[0])
        r2_ref[0, 0, :, cols] = pltpu.bitcast(rank2.astype(BF16), jnp.uint32)
        e2_ref[0, 0, :, cols] = pltpu.bitcast(jnp.exp(s2 - b0).astype(BF16), jnp.uint32)
        n1_ref[0, 0, :, cols] = twin_bf16(n1)
        f_ref[0, 0, :, cols] = twin_bf16(jnp.exp(s1 - a0) / zsum)

    count = lambda m: jnp.sum(jnp.where(m, 1.0, 0.0), axis=0, keepdims=True)

    def scores(cols):
        return (_nt_dot(sk_ref[0], q_ref[cols, 0:PEER_HALF], HIGHEST),
                _nt_dot(sk_ref[1], q_ref[cols, PEER_HALF:2 * PEER_HALF], HIGHEST))

    groups = [slice(g * LANES, (g + 1) * LANES) for g in range(tt // LANES)]
    any_tied = []
    for gidx, cols in enumerate(groups):
        ga_ref, gb_ref = a_ref.at[gidx], b_ref.at[gidx]
        s1, s2 = scores(cols)
        _, a_tops, w1 = extract(s1, key_id, ga_ref, False, False)
        rank2, b_tops, _ = extract(s2, key_id, gb_ref, False, True)
        _, c_tops, _ = extract(staircase_candidates(ga_ref, gb_ref), None, None, False, False)
        tau = c_tops[k - 1]
        n1 = jnp.zeros_like(s1)
        for y in range(k):
            n1 = n1 + jnp.where((s1 + b_tops[y]) >= tau, 1.0, 0.0)
        emit(cols, s1, s2, rank2, n1, a_tops[0], b_tops[0], c_tops)
        tied = ((count(w1 == neg) != kf) | (count(rank2 < kf) != kf)
                | (jnp.sum(n1, axis=0, keepdims=True) != kf))
        any_tied.append(jnp.max(jnp.where(tied, 1.0, 0.0)) > 0.0)

    for gidx, cols in enumerate(groups):
        @pl.when(any_tied[gidx])
        def _():
            ga_ref, gb_ref = a_ref.at[gidx], b_ref.at[gidx]
            s1, s2 = scores(cols)
            rank1, a_x, _ = extract(s1, key_id, ga_ref, True, True)
            rank2x, b_x, _ = extract(s2, key_id, gb_ref, True, True)
            rank3, c_x, _ = extract(candidates(ga_ref, gb_ref), cand_id, None, True, True)
            win = rank3 < kf
            n1x = jnp.zeros_like(s1)
            for x in range(k):
                n_x = count(win[x * k:(x + 1) * k])
                n1x = n1x + jnp.where(rank1 == float(x), n_x, 0.0)
            emit(cols, s1, s2, rank2x, n1x, a_x[0], b_x[0], c_x)


def _peer_route_call(q, subkeys, *, tt):
    t = q.shape[0]
    nt = t // tt
    spec = pl.BlockSpec((1, 1, PEER_KEYS, tt), lambda i, h: (h, i, 0, 0))
    pair_spec = pl.BlockSpec((1, 1, PEER_KEYS // 2, tt), lambda i, h: (h, i, 0, 0))
    tab = lambda rows: jax.ShapeDtypeStruct((PEER_HEADS, nt, rows, tt), jnp.uint32)
    return pl.pallas_call(
        functools.partial(_peer_route_kernel, tt=tt),
        grid=(nt, PEER_HEADS),
        in_specs=[
            pl.BlockSpec((tt, 2 * PEER_HALF), lambda i, h: (i, h)),
            pl.BlockSpec(subkeys.shape, lambda i, h: (0, 0, 0)),
        ],
        out_specs=[pair_spec, pair_spec, spec, spec],
        out_shape=[tab(PEER_KEYS // 2), tab(PEER_KEYS // 2), tab(PEER_KEYS), tab(PEER_KEYS)],
        scratch_shapes=[
            pltpu.VMEM((tt // LANES, PEER_TOPK, LANES), F32),
            pltpu.VMEM((tt // LANES, PEER_TOPK, LANES), F32),
            pltpu.VMEM((PEER_TOPK * PEER_TOPK, LANES), F32),
        ],
        compiler_params=_cparams(("arbitrary", "arbitrary")),
        name="peer_route",
    )(q, subkeys)


def _gelu(x):
    return 0.5 * x * (1.0 + lax.erf(x * (1.0 / math.sqrt(2.0))))


def _peer_expert_kernel(x_ref, gate_ref, ht_ref, u_ref, vt_ref, n1_ref, f_ref, r2_ref, e2_ref,
                        o_ref, pre_a, pre_b, wa_ref, acc_ref, *, rows_per_tile):
    s = pl.program_id(1)
    te, tt = pre_a.shape
    d = acc_ref.shape[0]
    blk = 16
    rows_per_piece = 2

    @pl.when(s == 0)
    def _():
        acc_ref[...] = jnp.zeros_like(acc_ref)
        pre_b[...] = jnp.zeros_like(pre_b)

    def step(pre_new, pre_old):
        halves = [slice(m * d // 2, (m + 1) * d // 2) for m in range(2)]
        zero = jnp.zeros((blk, LANES), BF16)
        as_bf16 = lambda words: pltpu.bitcast(words, BF16)
        for il in range(rows_per_tile):
            for lt in range(tt // LANES):
                lanes = slice(lt * LANES, (lt + 1) * LANES)
                row = lambda ref, h: as_bf16(jnp.broadcast_to(ref[h, 0, il:il + 1, lanes], (blk // 2, LANES)))
                n1b = [row(n1_ref, h) for h in range(PEER_HEADS)]
                fb = [row(f_ref, h) for h in range(PEER_HEADS)]
                for jb in range(PEER_KEYS // blk):
                    words = slice(jb * blk // 2, (jb + 1) * blk // 2)
                    w = zero
                    for h in range(PEER_HEADS):
                        hit = as_bf16(r2_ref[h, 0, words, lanes]) < n1b[h]
                        w = w + jnp.where(hit, as_bf16(e2_ref[h, 0, words, lanes]), zero) * fb[h]
                    rows = slice(il * PEER_KEYS + jb * blk, il * PEER_KEYS + (jb + 1) * blk)
                    wa_ref[rows, lanes] = w
        n_pre = te // 512
        for m in range(n_pre):
            rows = slice(m * te // n_pre, (m + 1) * te // n_pre)
            pre_new[rows, :] = _dot(u_ref[rows, :], ht_ref[...])
        piece_rows = rows_per_piece * PEER_KEYS
        for p in range(te // piece_rows):
            piece = slice(p * piece_rows, (p + 1) * piece_rows)
            for r0 in range(p * piece_rows, (p + 1) * piece_rows, blk):
                rows = slice(r0, r0 + blk)
                wa_ref[rows, :] = wa_ref[rows, :] * _gelu(pre_old[rows, :]).astype(BF16)
            for rows in halves:
                acc_ref[rows, :] += _dot(vt_ref[0, rows, piece], wa_ref[piece, :])

    @pl.when(s % 2 == 0)
    def _():
        step(pre_a, pre_b)

    @pl.when(s % 2 == 1)
    def _():
        step(pre_b, pre_a)

    @pl.when(s == pl.num_programs(1) - 1)
    def _():
        o_ref[...] = x_ref[...] + gate_ref[0] * acc_ref[...].T


def _peer_expert_call(x, gate, hb, u_b, vt_b, n1, f, r2, e2, seq, *, tt, te=1024):
    t, d = x.shape
    ne = u_b.shape[0] // te
    rows_per_tile = te // PEER_KEYS
    lag = lambda s, n: jnp.clip(s - n, 0, ne - 1)
    sel = pl.BlockSpec((PEER_HEADS, 1, rows_per_tile, tt), lambda i, s: (0, i, lag(s, 1), 0))
    full = pl.BlockSpec((PEER_HEADS, 1, PEER_KEYS // 2, tt), lambda i, s: (0, i, 0, 0))
    return pl.pallas_call(
        functools.partial(_peer_expert_kernel, rows_per_tile=rows_per_tile),
        grid=(t // tt, ne + 1),
        in_specs=[
            pl.BlockSpec((tt, d), lambda i, s: (i, 0)),
            pl.BlockSpec((1, 1, d), lambda i, s: ((i * tt) // seq, 0, 0)),
            pl.BlockSpec((d, tt), lambda i, s: (0, i)),
            pl.BlockSpec((te, d), lambda i, s: (lag(s, 0), 0)),
            pl.BlockSpec((1, d, te), lambda i, s: (lag(s, 1), 0, 0)),
            sel, sel, full, full,
        ],
        out_specs=pl.BlockSpec((tt, d), lambda i, s: (i, 0)),
        out_shape=jax.ShapeDtypeStruct((t, d), F32),
        scratch_shapes=[
            pltpu.VMEM((te, tt), F32),
            pltpu.VMEM((te, tt), F32),
            pltpu.VMEM((te, tt), BF16),
            pltpu.VMEM((d, tt), F32),
        ],
        compiler_params=_cparams(("arbitrary", "arbitrary")),
        name="peer_experts",
    )(x, gate, hb, u_b, vt_b, n1, f, r2, e2)


def _final_norm_kernel(x_ref, g_ref, o_ref):
    x = x_ref[...]
    ms = jnp.mean(x * x, axis=-1, keepdims=True)
    o_ref[...] = x * lax.rsqrt(ms + NORM_EPS) * g_ref[...]


def _final_norm_call(x, g, *, tm=512):
    t, d = x.shape
    return pl.pallas_call(
        _final_norm_kernel,
        grid=(t // tm,),
        in_specs=[pl.BlockSpec((tm, d), lambda i: (i, 0)), pl.BlockSpec((1, d), lambda i: (0, 0))],
        out_specs=pl.BlockSpec((tm, d), lambda i: (i, 0)),
        out_shape=jax.ShapeDtypeStruct((t, d), F32),
        compiler_params=_cparams(("arbitrary",)),
        name="final_norm",
    )(x, g.reshape(1, d))


def _pad_rows(a, rows):
    return jnp.pad(a, ((0, rows - a.shape[0]), (0, 0)))


def _pad_cols(a, cols):
    return jnp.pad(a, ((0, 0), (0, cols - a.shape[1])))


def _hybrid_mixer(x, g, shift, scale, batch, seq, w_in, conv_w, conv_b, conv_ln_g, conv_ln_b, mu, w0, w2,
                  a0, a2, g2, k_k, k_a, r_k, ln_g, ln_b):
    rw = RWKV_WIDTH
    c0 = 2 * CONV_WIDTH
    lora0 = c0 + 3 * rw
    l1, l2 = lora0 + DECAY_LORA, lora0 + DECAY_LORA + AAA_LORA
    w_conv = w_in[:, :c0].astype(BF16)
    w_rkv = w_in[:, c0:lora0].astype(BF16)
    w_lora = jnp.concatenate([
        _pad_cols(w_in[:, lora0:l1], LANES), _pad_cols(w_in[:, l1:l2], LANES),
        _pad_cols(w_in[:, l2:], 2 * LANES)], axis=1).astype(BF16)
    zc, zr, zl = _norm_matmul_call(x, g, shift, scale, [w_conv, w_rkv, w_lora], [F32, F32, F32], seq,
                                   name="hybrid_in_proj")
    ya = _conv_call(zc, conv_w, conv_b, conv_ln_g, conv_ln_b, batch, seq)
    mu_l = jnp.concatenate([
        _pad_cols(mu[None, lora0 - c0:l1 - c0], LANES), _pad_cols(mu[None, l1 - c0:l2 - c0], LANES),
        _pad_cols(mu[None, l2 - c0:], 2 * LANES)], axis=1)
    vec = lambda a: a.reshape(1, rw)
    head_id = jnp.arange(RWKV_GROUP * RWKV_HEAD) // RWKV_HEAD
    params = dict(
        mu_r=mu[None, :3 * rw], mu_l=mu_l, w0=vec(w0), a0=vec(a0), k_k=vec(k_k), k_a=vec(k_a),
        r_k=vec(r_k), ln_g=vec(ln_g), ln_b=vec(ln_b),
        w2=_pad_rows(w2, LANES).astype(BF16), a2=_pad_rows(a2, LANES).astype(BF16),
        g2=_pad_rows(g2, 2 * LANES).astype(BF16),
        seg=(head_id[:, None] == head_id[None, :]).astype(BF16))
    yb = _rwkv_call(zr, zl, params, batch, seq)
    return ya, yb


def kernel(x, c, positions, ada_w, ada_b, norm_mix_g, norm_ffn_g, hyb_w_in, conv_w, conv_b, conv_ln_g, conv_ln_b, rwkv_mu, rwkv_w0, rwkv_w2, rwkv_a0, rwkv_a2, rwkv_g2, rwkv_k_k, rwkv_k_a, rwkv_r_k, rwkv_ln_g, rwkv_ln_b, hyb_w_out, diff_w_qkv, diff_lq1, diff_lk1, diff_lq2, diff_lk2, diff_subln_g, diff_w_out, peer_w_q, peer_subkeys, peer_u, peer_v, final_g):
    batch, seq, d = x.shape
    depth = ada_w.shape[0]
    t = batch * seq
    xt = x.reshape(t, d)
    pos = positions.reshape(t, 1)
    mod = _ada_call(c, ada_w, ada_b).reshape(depth, batch, 6, 1, d)

    for l in range(depth):
        sh1, sc1, g1, sh2, sc2, g2 = (mod[l, :, j] for j in range(6))
        gmix = norm_mix_g[l].reshape(1, d)
        if l % 2 == 0:
            e = l // 2
            ya, yb = _hybrid_mixer(
                xt, gmix, sh1, sc1, batch, seq, hyb_w_in[e], conv_w[e], conv_b[e], conv_ln_g[e],
                conv_ln_b[e], rwkv_mu[e], rwkv_w0[e], rwkv_w2[e], rwkv_a0[e], rwkv_a2[e], rwkv_g2[e],
                rwkv_k_k[e], rwkv_k_a[e], rwkv_r_k[e], rwkv_ln_g[e], rwkv_ln_b[e])
            w_out = hyb_w_out[e].astype(BF16)
            xt = _out_proj_call(xt, g1, [ya.astype(BF16), yb.astype(BF16)],
                                [w_out[:CONV_WIDTH], w_out[CONV_WIDTH:]], seq)
        else:
            o = l // 2
            lambda_init = 0.8 - 0.6 * math.exp(-0.3 * l)
            wqkv = diff_w_qkv[o].astype(BF16)
            n = wqkv.shape[1] // 3
            q, k, v = _qkv_rope_call(xt, gmix, sh1, sc1, pos, wqkv[:, :n], wqkv[:, n:2 * n], wqkv[:, 2 * n:],
                                     seq)
            y = _attn_call(q, k, v, diff_lq1[o], diff_lk1[o], diff_lq2[o], diff_lk2[o], diff_subln_g[o],
                           lambda_init, batch, seq)
            xt = _out_proj_call(xt, g1, [y], [diff_w_out[o].astype(BF16)], seq)

        q, hb = _norm_matmul_call(xt, norm_ffn_g[l].reshape(1, d), sh2, sc2, [peer_w_q[l]], [F32], seq,
                                  split=True, emit_h=True, name="peer_query")
        r2, e2, n1, f = _peer_route_call(q, peer_subkeys[l], tt=PEER_TOKEN_TILE)
        v_tiles = peer_v[l].astype(BF16).reshape(-1, PEER_EXPERT_TILE, d).transpose(0, 2, 1)
        xt = _peer_expert_call(xt, g2, hb, peer_u[l].astype(BF16), v_tiles,
                               n1, f, r2, e2, seq, tt=PEER_TOKEN_TILE, te=PEER_EXPERT_TILE)

    return _final_norm_call(xt, final_g).reshape(batch, seq, d)
```

```python
import functools
import math

import jax
import jax.numpy as jnp
from jax import lax
from jax.experimental import pallas as pl
from jax.experimental.pallas import tpu as pltpu

F32 = jnp.float32
BF16 = jnp.bfloat16
HIGHEST = lax.Precision.HIGHEST

CONV_WIDTH = 512
CONV_KERNEL = 31
RWKV_WIDTH = 512
RWKV_HEAD = 64
RWKV_HEADS = 8
DECAY_LORA = 64
AAA_LORA = 64
GATE_LORA = 160
DIFF_HEADS = 8
DIFF_HEAD_DIM = 64
DIFF_V_DIM = 128
ROPE_DIM = 16
ROPE_THETA = 500000.0
ATTN_CHUNK = 64
PEER_HEADS = 8
PEER_KEYS = 128
PEER_HALF = 128
PEER_TOPK = 16
NORM_EPS = 1e-6
LN_EPS = 1e-5
RWKV_GN_EPS = 64e-5

LANES = 128
SCAN_CHUNK = 64
CONV_HALO = 32
PEER_EXPERT_TILE = 1024
PEER_TOKEN_TILE = 256
VMEM_LIMIT = 48 * 1024 * 1024


def _cparams(semantics, flags=None):
    return pltpu.CompilerParams(dimension_semantics=semantics, vmem_limit_bytes=VMEM_LIMIT, flags=flags)


def _nt_dot(a, b, precision=None):
    return lax.dot_general(a, b, (((1,), (1,)), ((), ())), precision=precision,
                           preferred_element_type=F32)


def _dot(a, b, precision=None):
    return jnp.dot(a, b, precision=precision, preferred_element_type=F32)


def _sigmoid(x):
    return 1.0 / (1.0 + jnp.exp(-x))


def _ada_kernel(c_ref, w_ref, b_ref, o_ref):
    c = c_ref[...]
    cond = c * _sigmoid(c)
    o_ref[0] = _dot(cond, w_ref[0], HIGHEST) + b_ref[0]


def _ada_call(c, ada_w, ada_b):
    depth, d, n = ada_w.shape
    b = c.shape[0]
    tn = 1536
    return pl.pallas_call(
        _ada_kernel,
        grid=(depth, n // tn),
        in_specs=[
            pl.BlockSpec((b, d), lambda l, j: (0, 0)),
            pl.BlockSpec((1, d, tn), lambda l, j: (l, 0, j)),
            pl.BlockSpec((1, 1, tn), lambda l, j: (l, 0, j)),
        ],
        out_specs=pl.BlockSpec((1, b, tn), lambda l, j: (l, 0, j)),
        out_shape=jax.ShapeDtypeStruct((depth, b, n), F32),
        compiler_params=_cparams(("arbitrary", "arbitrary")),
        name="ada_mod",
    )(c, ada_w, ada_b.reshape(depth, 1, n))


def _modulate(x, g, shift, scale):
    ms = jnp.mean(x * x, axis=-1, keepdims=True)
    y = x * lax.rsqrt(ms + NORM_EPS)
    return (y * g) * (1.0 + scale) + shift


def _split_bf16(a):
    hi = a.astype(BF16)
    return hi, (a - hi.astype(F32)).astype(BF16)


def _norm_matmul_kernel(*refs, n_w, split, emit_h):
    x_ref, g_ref, sh_ref, sc_ref = refs[:4]
    w_refs = refs[4:4 + n_w]
    o_refs = refs[4 + n_w:]
    h = _modulate(x_ref[...], g_ref[...], sh_ref[0], sc_ref[0])
    hb = h.astype(BF16)
    if split:
        h_lo = (h - hb.astype(F32)).astype(BF16)
        for k in range(n_w // 2):
            w_hi, w_lo = w_refs[2 * k][...], w_refs[2 * k + 1][...]
            o_refs[k][...] = (_dot(hb, w_hi) + _dot(h_lo, w_hi) + _dot(hb, w_lo)).astype(o_refs[k].dtype)
    else:
        for w_ref, o_ref in zip(w_refs, o_refs[:n_w]):
            o_ref[...] = _dot(hb, w_ref[...]).astype(o_ref.dtype)
    if emit_h:
        o_refs[-1][...] = h.T.astype(BF16)


def _norm_matmul_call(x, g, shift, scale, weights, out_dtypes, seq, *, split=False, emit_h=False,
                      tm=512, name="norm_matmul"):
    t, d = x.shape
    if split:
        weights = [part for w in weights for part in _split_bf16(w)]
    in_specs = [
        pl.BlockSpec((tm, d), lambda i: (i, 0)),
        pl.BlockSpec((1, d), lambda i: (0, 0)),
        pl.BlockSpec((1, 1, d), lambda i: ((i * tm) // seq, 0, 0)),
        pl.BlockSpec((1, 1, d), lambda i: ((i * tm) // seq, 0, 0)),
    ]
    out_specs, out_shape = [], []
    for w in weights:
        in_specs.append(pl.BlockSpec(w.shape, lambda i: (0, 0)))
    for w, dt in zip(weights[::2] if split else weights, out_dtypes):
        n = w.shape[1]
        out_specs.append(pl.BlockSpec((tm, n), lambda i: (i, 0)))
        out_shape.append(jax.ShapeDtypeStruct((t, n), dt))
    if emit_h:
        out_specs.append(pl.BlockSpec((d, tm), lambda i: (0, i)))
        out_shape.append(jax.ShapeDtypeStruct((d, t), BF16))
    return pl.pallas_call(
        functools.partial(_norm_matmul_kernel, n_w=len(weights), split=split, emit_h=emit_h),
        grid=(t // tm,),
        in_specs=in_specs,
        out_specs=out_specs,
        out_shape=out_shape,
        compiler_params=_cparams(("arbitrary",)),
        name=name,
    )(x, g, shift, scale, *weights)


def _out_proj_kernel(*refs, n_y):
    x_ref, gate_ref = refs[:2]
    y_refs = refs[2:2 + n_y]
    w_refs = refs[2 + n_y:2 + 2 * n_y]
    o_ref = refs[2 + 2 * n_y]
    acc = _dot(y_refs[0][...], w_refs[0][...])
    for y_ref, w_ref in zip(y_refs[1:], w_refs[1:]):
        acc = acc + _dot(y_ref[...], w_ref[...])
    o_ref[...] = x_ref[...] + gate_ref[0] * acc


def _out_proj_call(x, gate, ys, ws, seq, *, tm=512):
    t, d = x.shape
    in_specs = [
        pl.BlockSpec((tm, d), lambda i: (i, 0)),
        pl.BlockSpec((1, 1, d), lambda i: ((i * tm) // seq, 0, 0)),
    ]
    for y in ys:
        in_specs.append(pl.BlockSpec((tm, y.shape[1]), lambda i: (i, 0)))
    for w in ws:
        in_specs.append(pl.BlockSpec(w.shape, lambda i: (0, 0)))
    return pl.pallas_call(
        functools.partial(_out_proj_kernel, n_y=len(ys)),
        grid=(t // tm,),
        in_specs=in_specs,
        out_specs=pl.BlockSpec((tm, d), lambda i: (i, 0)),
        out_shape=jax.ShapeDtypeStruct((t, d), F32),
        compiler_params=_cparams(("arbitrary",)),
        name="out_proj",
    )(x, gate, *ys, *ws)


def _conv_kernel(z_ref, w_ref, b_ref, g_ref, beta_ref, o_ref, ext_ref, *, ts):
    width = CONV_WIDTH

    @pl.when(pl.program_id(1) == 0)
    def _():
        ext_ref[0:CONV_HALO, :] = jnp.zeros((CONV_HALO, width), F32)

    z = z_ref[...]
    u = z[:, :width] * _sigmoid(z[:, width:])
    ext_ref[CONV_HALO:CONV_HALO + ts, :] = u
    base = CONV_HALO - (CONV_KERNEL - 1)
    rows = 64
    for cb in range(width // LANES):
        cs = slice(cb * LANES, (cb + 1) * LANES)
        for rb in range(ts // rows):
            acc = jnp.zeros((rows, LANES), F32)
            for j in range(CONV_KERNEL):
                start = rb * rows + base + j
                acc = acc + w_ref[j:j + 1, cs] * ext_ref[start:start + rows, cs]
            o_ref[rb * rows:(rb + 1) * rows, cs] = acc
    conv = o_ref[...] + b_ref[...]
    mu = jnp.mean(conv, axis=-1, keepdims=True)
    dlt = conv - mu
    var = jnp.mean(dlt * dlt, axis=-1, keepdims=True)
    y = dlt * lax.rsqrt(var + LN_EPS) * g_ref[...] + beta_ref[...]
    o_ref[...] = y * _sigmoid(y)
    ext_ref[0:CONV_HALO, :] = ext_ref[ts:ts + CONV_HALO, :]


def _conv_call(z, conv_w, conv_b, ln_g, ln_b, batch, seq, *, ts=256):
    t = z.shape[0]
    width = CONV_WIDTH
    nts = seq // ts
    vec = lambda a: a.reshape(1, width)
    return pl.pallas_call(
        functools.partial(_conv_kernel, ts=ts),
        grid=(batch, nts),
        in_specs=[
            pl.BlockSpec((ts, 2 * width), lambda b, i: (b * nts + i, 0)),
            pl.BlockSpec((CONV_KERNEL, width), lambda b, i: (0, 0)),
            pl.BlockSpec((1, width), lambda b, i: (0, 0)),
            pl.BlockSpec((1, width), lambda b, i: (0, 0)),
            pl.BlockSpec((1, width), lambda b, i: (0, 0)),
        ],
        out_specs=pl.BlockSpec((ts, width), lambda b, i: (b * nts + i, 0)),
        out_shape=jax.ShapeDtypeStruct((t, width), F32),
        scratch_shapes=[pltpu.VMEM((ts + CONV_HALO, width), F32)],
        compiler_params=_cparams(("arbitrary", "arbitrary")),
        name="conformer_conv",
    )(z, conv_w, vec(conv_b), vec(ln_g), vec(ln_b))


def _softplus(x):
    return jnp.maximum(x, 0.0) + jnp.log(1.0 + jnp.exp(-jnp.abs(x)))


def _unit_lower_inverses(mats, row, col, size):
    eye = (row == col).astype(F32)
    blk = lambda m: (row // m) == (col // m)
    bdot = lambda p, q: _dot(p.astype(BF16), q.astype(BF16))
    n1 = [jnp.where(blk(8), a, 0.0) for a in mats]
    t = [eye + x for x in n1]
    n2 = [bdot(x, x) for x in n1]
    t = [x + bdot(x, y) for x, y in zip(t, n2)]
    n4 = [bdot(x, x) for x in n2]
    t = [x + bdot(x, y) for x, y in zip(t, n4)]
    m = 8
    while m < size:
        new = blk(2 * m) & jnp.logical_not(blk(m))
        et = [bdot(jnp.where(new, a, 0.0), x) for a, x in zip(mats, t)]
        t = [x + bdot(x, y) for x, y in zip(t, et)]
        m *= 2
    return t


def _split_dot(x, w_bf16, parts):
    acc = None
    for _ in range(parts):
        piece = x.astype(BF16)
        term = _dot(piece, w_bf16)
        acc = term if acc is None else acc + term
        x = x - piece.astype(F32)
    return acc


RWKV_GROUP = 4
RWKV_BATCH_TILE = 2


def _rwkv_kernel(zr_ref, zl_ref, mur_ref, mul_ref, w0_ref, a0_ref, kk_ref, ka_ref, rk_ref,
                 lng_ref, lnb_ref, w2_ref, a2_ref, g2_ref, seg_ref, o_ref,
                 extr_ref, extl_ref, state_ref):
    c = SCAN_CHUNK
    rw = RWKV_WIDTH
    n = RWKV_HEAD
    nb = RWKV_BATCH_TILE
    gw = RWKV_GROUP * n
    groups = rw // gw

    @pl.when(pl.program_id(1) == 0)
    def _():
        extr_ref[:, 0:8, :] = jnp.zeros((nb, 8, 3 * rw), F32)
        extl_ref[:, 0:8, :] = jnp.zeros((nb, 8, rw), F32)
        state_ref[...] = jnp.zeros_like(state_ref)

    zr_rows, zl_rows = [], []
    for b in range(nb):
        zr = zr_ref[b]
        zl = zl_ref[b]
        extr_ref[b, 8:8 + c, :] = zr
        extl_ref[b, 8:8 + c, :] = zl
        zr_rows.append(zr + (extr_ref[b, 7:7 + c, :] - zr) * mur_ref[...])
        zl_rows.append(zl + (extl_ref[b, 7:7 + c, :] - zl) * mul_ref[...])
        extr_ref[b, 0:8, :] = extr_ref[b, c:c + 8, :]
        extl_ref[b, 0:8, :] = extl_ref[b, c:c + 8, :]
    zr = jnp.concatenate(zr_rows, axis=0)
    zl = jnp.concatenate(zl_rows, axis=0)

    r = zr[:, 0:rw]
    k = zr[:, rw:2 * rw]
    v = zr[:, 2 * rw:3 * rw]
    wd = zl[:, 0:LANES]
    ad = zl[:, LANES:2 * LANES]
    gd = zl[:, 2 * LANES:4 * LANES]
    seg = seg_ref[...]

    def segsum(x, parts):
        return jnp.concatenate([_split_dot(x[:, g * gw:(g + 1) * gw], seg, parts) for g in range(groups)],
                               axis=1)

    w_raw = -_softplus(-(w0_ref[...] + _dot(jnp.tanh(wd).astype(BF16), w2_ref[...]))) - 0.5
    lw = -jnp.exp(w_raw)
    alpha = _sigmoid(a0_ref[...] + _dot(ad.astype(BF16), a2_ref[...]))
    gate = _dot(_sigmoid(gd).astype(BF16), g2_ref[...])
    kkf = k * kk_ref[...]
    kk = kkf / jnp.maximum(jnp.sqrt(segsum(kkf * kkf, 3)), 1e-12)
    kp = k * (1.0 + (alpha - 1.0) * ka_ref[...])
    bonus = segsum(r * kp * rk_ref[...], 3) * v
    av = -kk
    bv = kk * alpha

    trow = lax.broadcasted_iota(jnp.int32, (nb * c, nb * c), 0)
    tcol = lax.broadcasted_iota(jnp.int32, (nb * c, nb * c), 1)
    same_seq_incl = ((trow // c) == (tcol // c)) & (tcol <= trow)
    cum = _split_dot_lhs_exact(jnp.where(same_seq_incl, 1.0, 0.0).astype(BF16), lw)
    tot = jnp.concatenate([jnp.broadcast_to(cum[(b + 1) * c - 1:(b + 1) * c, :], (c, rw)) for b in range(nb)],
                          axis=0)
    g_in = jnp.exp(cum)
    g_inp = jnp.exp(cum - lw)
    g_out = jnp.exp(-cum)
    g_end = jnp.exp(tot - cum)
    g_tot = jnp.exp(tot)
    a_in = av * g_inp
    r_in = r * g_in
    b_out = bv * g_out
    k_out = kp * g_out
    b_end = bv * g_end
    k_end = kp * g_end

    row = lax.broadcasted_iota(jnp.int32, (gw, gw), 0)
    col = lax.broadcasted_iota(jnp.int32, (gw, gw), 1)
    same_head = (row // n) == (col // n)
    strict = same_head & ((col % n) < (row % n))
    incl = same_head & ((col % n) <= (row % n))

    def expand(x):
        return jnp.where(same_head, jnp.concatenate([x] * RWKV_GROUP, axis=0), 0.0)

    bf = lambda x: x.astype(BF16)
    chains = [(b, g) for b in range(nb) for g in range(groups)]
    pick = lambda x: [expand(x[b * c:(b + 1) * c, g * gw:(g + 1) * gw]) for b, g in chains]
    each = lambda fn, *lists: [fn(*args) for args in zip(*lists)]
    a_bd, r_bd = each(bf, pick(a_in)), each(bf, pick(r_in))
    bo_bd, ko_bd = each(bf, pick(b_out)), each(bf, pick(k_out))
    v_f = pick(v)
    v_bd = each(bf, v_f)
    st = [state_ref[b, g] for b, g in chains]
    st_b = each(bf, st)
    a_ab = each(lambda p, q: jnp.where(strict, _nt_dot(p, q), 0.0), a_bd, bo_bd)
    a_ak = each(lambda p, q: bf(jnp.where(strict, _nt_dot(p, q), 0.0)), a_bd, ko_bd)
    a_rb = each(lambda p, q: bf(jnp.where(incl, _nt_dot(p, q), 0.0)), r_bd, bo_bd)
    a_rk = each(lambda p, q: bf(jnp.where(incl, _nt_dot(p, q), 0.0)), r_bd, ko_bd)
    ph_a = each(_nt_dot, a_bd, st_b)
    ph_r = each(_nt_dot, r_bd, st_b)
    tinv = _unit_lower_inverses(a_ab, row, col, n)
    rhs = each(lambda p, m, w: bf(p + _dot(m, w)), ph_a, a_ak, v_bd)
    u = each(lambda t, x: _dot(bf(t), x), tinv, rhs)
    y_bd = each(lambda p, m1, uu, m2, w: p + _dot(m1, bf(uu)) + _dot(m2, w), ph_r, a_rb, u, a_rk, v_bd)
    be_bd, ke_bd = each(bf, pick(b_end)), each(bf, pick(k_end))
    for (b, g), s0, uu, vf, be, ke in zip(chains, st, u, v_f, be_bd, ke_bd):
        state_ref[b, g] = (s0 * g_tot[b * c:b * c + 1, g * gw:(g + 1) * gw]
                           + _dot(bf(uu.T), be) + _dot(bf(vf.T), ke))
    y_g = [sum(m[i * c:(i + 1) * c] for i in range(RWKV_GROUP)) for m in y_bd]
    y = jnp.concatenate([jnp.concatenate(y_g[b * groups:(b + 1) * groups], axis=1) for b in range(nb)],
                        axis=0)

    inv_n = 1.0 / n
    mu_y = segsum(y, 3) * inv_n
    dy = y - mu_y
    var_y = segsum(dy * dy, 3) * inv_n
    yn = dy * lax.rsqrt(var_y + RWKV_GN_EPS)
    out = (yn * lng_ref[...] + lnb_ref[...] + bonus) * gate
    for b in range(nb):
        o_ref[b] = out[b * c:(b + 1) * c]


def _split_dot_lhs_exact(w_bf16, x):
    acc = None
    for _ in range(3):
        piece = x.astype(BF16)
        term = _dot(w_bf16, piece)
        acc = term if acc is None else acc + term
        x = x - piece.astype(F32)
    return acc


def _rwkv_call(zr, zl, p, batch, seq):
    c = SCAN_CHUNK
    rw = RWKV_WIDTH
    nb = RWKV_BATCH_TILE
    nc = seq // c
    gw = RWKV_GROUP * RWKV_HEAD
    const = lambda shape: pl.BlockSpec(shape, lambda b, i: (0,) * len(shape))
    vec = const((1, rw))
    out = pl.pallas_call(
        _rwkv_kernel,
        grid=(batch // nb, nc),
        in_specs=[
            pl.BlockSpec((nb, c, 3 * rw), lambda b, i: (b, i, 0)),
            pl.BlockSpec((nb, c, rw), lambda b, i: (b, i, 0)),
            const((1, 3 * rw)), vec,
            vec, vec, vec, vec, vec, vec, vec,
            const((LANES, rw)), const((LANES, rw)), const((2 * LANES, rw)),
            const((gw, gw)),
        ],
        out_specs=pl.BlockSpec((nb, c, rw), lambda b, i: (b, i, 0)),
        out_shape=jax.ShapeDtypeStruct((batch, seq, rw), F32),
        scratch_shapes=[
            pltpu.VMEM((nb, c + 8, 3 * rw), F32),
            pltpu.VMEM((nb, c + 8, rw), F32),
            pltpu.VMEM((nb, rw // gw, gw, gw), F32),
        ],
        compiler_params=_cparams(("arbitrary", "arbitrary")),
        name="rwkv7_mix",
    )(zr.reshape(batch, seq, 3 * rw), zl.reshape(batch, seq, rw), p["mu_r"], p["mu_l"], p["w0"], p["a0"],
      p["k_k"], p["k_a"], p["r_k"], p["ln_g"], p["ln_b"], p["w2"], p["a2"], p["g2"], p["seg"])
    return out.reshape(batch * seq, rw)


def _qkv_rope_kernel(x_ref, g_ref, sh_ref, sc_ref, pos_ref, inv_ref, sgn_ref, wq_ref, wk_ref, wv_ref,
                     q_ref, k_ref, v_ref):
    h = _modulate(x_ref[...], g_ref[...], sh_ref[0], sc_ref[0]).astype(BF16)
    ang = pos_ref[...].astype(F32) * inv_ref[...]
    cosf = jnp.cos(ang)
    sinf = jnp.sin(ang) * sgn_ref[...]
    lane = lax.broadcasted_iota(jnp.int32, ang.shape, 1)
    low = (lane % DIFF_HEAD_DIM) < (ROPE_DIM // 2)
    half = ROPE_DIM // 2

    def rope(w_ref, o_ref, scale):
        z = _dot(h, w_ref[...])
        for cb in range(z.shape[1] // LANES):
            zc = z[:, cb * LANES:(cb + 1) * LANES]
            partner = jnp.where(low, pltpu.roll(zc, LANES - half, axis=1), pltpu.roll(zc, half, axis=1))
            o_ref[:, cb * LANES:(cb + 1) * LANES] = ((zc * cosf + partner * sinf) * scale).astype(o_ref.dtype)

    rope(wq_ref, q_ref, DIFF_HEAD_DIM ** -0.5)
    rope(wk_ref, k_ref, 1.0)
    v_ref[...] = _dot(h, wv_ref[...]).astype(v_ref.dtype)


def _qkv_rope_call(x, g, shift, scale, pos, wq, wk, wv, seq, *, tm=512):
    t, d = x.shape
    half = ROPE_DIM // 2
    inv = 1.0 / (ROPE_THETA ** (jnp.arange(0, ROPE_DIM, 2, dtype=F32) / ROPE_DIM))
    dl = jnp.arange(LANES) % DIFF_HEAD_DIM
    inv_pat = jnp.where(dl < ROPE_DIM, inv[dl % half], 0.0).reshape(1, LANES).astype(F32)
    sgn_pat = jnp.where(dl < half, -1.0, 1.0).reshape(1, LANES).astype(F32)
    row = lambda n: pl.BlockSpec((tm, n), lambda i: (i, 0))
    full = lambda a: pl.BlockSpec(a.shape, lambda i: (0, 0))
    mod = pl.BlockSpec((1, 1, d), lambda i: ((i * tm) // seq, 0, 0))
    n = wq.shape[1]
    return pl.pallas_call(
        _qkv_rope_kernel,
        grid=(t // tm,),
        in_specs=[row(d), full(g), mod, mod, row(1), full(inv_pat), full(sgn_pat),
                  full(wq), full(wk), full(wv)],
        out_specs=[row(n), row(n), row(n)],
        out_shape=[jax.ShapeDtypeStruct((t, n), BF16)] * 3,
        compiler_params=_cparams(("arbitrary",)),
        name="qkv_rope",
    )(x, g, shift, scale, pos, inv_pat, sgn_pat, wq, wk, wv)


def _attn_kernel(q_ref, k_ref, v_ref, lq1_ref, lk1_ref, lq2_ref, lk2_ref, sg_ref, o_ref,
                 *, tq, seq, lambda_init):
    i = pl.program_id(2)
    kb = 2 * tq
    q = q_ref[...]
    lane = lax.broadcasted_iota(jnp.int32, q.shape, 1)
    zero = jnp.zeros_like(q)
    qm = (jnp.where(lane < DIFF_HEAD_DIM, q, zero), jnp.where(lane >= DIFF_HEAD_DIM, q, zero))
    f32sum = lambda a, b: jnp.sum(a[...] * b[...], axis=-1, keepdims=True)
    lam = jnp.exp(f32sum(lq1_ref, lk1_ref)) - jnp.exp(f32sum(lq2_ref, lk2_ref)) + lambda_init
    row_chunk = (i * tq + lax.broadcasted_iota(jnp.int32, (tq, kb), 0)) // ATTN_CHUNK
    col_in_block = lax.broadcasted_iota(jnp.int32, (tq, kb), 1)

    def attend(nblk):
        outs = []
        for mp in range(2):
            s = [_nt_dot(qm[mp], k_ref[j * kb:(j + 1) * kb, :]) for j in range(nblk)]
            col_chunk = ((nblk - 1) * kb + col_in_block) // ATTN_CHUNK
            s[-1] = jnp.where(col_chunk <= row_chunk, s[-1], -jnp.inf)
            m = jnp.max(s[0], axis=-1, keepdims=True)
            for sj in s[1:]:
                m = jnp.maximum(m, jnp.max(sj, axis=-1, keepdims=True))
            l = jnp.zeros_like(m)
            acc = jnp.zeros((tq, DIFF_V_DIM), F32)
            for j, sj in enumerate(s):
                p = jnp.exp(sj - m)
                l = l + jnp.sum(p, axis=-1, keepdims=True)
                acc = acc + _dot(p.astype(BF16), v_ref[j * kb:(j + 1) * kb, :])
            outs.append(acc / l)
        o = outs[0] - lam * outs[1]
        ms = jnp.mean(o * o, axis=-1, keepdims=True)
        o = o * lax.rsqrt(ms + NORM_EPS) * sg_ref[...] * (1.0 - lambda_init)
        o_ref[...] = o.astype(o_ref.dtype)

    for nblk in range(1, seq // kb + 1):
        pl.when(i // 2 == nblk - 1)(functools.partial(attend, nblk))


def _attn_call(q, k, v, lq1, lk1, lq2, lk2, subln_g, lambda_init, batch, seq, *, tq=256):
    t = q.shape[0]
    nq = seq // tq
    assert seq % (2 * tq) == 0
    vec = lambda a: a.reshape(1, -1)
    small = lambda a: pl.BlockSpec(a.shape, lambda b, h, i: (0, 0))
    kv = pl.BlockSpec((seq, DIFF_V_DIM), lambda b, h, i: (b, h))
    args = [vec(lq1), vec(lk1), vec(lq2), vec(lk2), vec(subln_g)]
    return pl.pallas_call(
        functools.partial(_attn_kernel, tq=tq, seq=seq, lambda_init=lambda_init),
        grid=(batch, DIFF_HEADS, nq),
        in_specs=[pl.BlockSpec((tq, DIFF_V_DIM), lambda b, h, i: (b * nq + i, h)), kv, kv]
                 + [small(a) for a in args],
        out_specs=pl.BlockSpec((tq, DIFF_V_DIM), lambda b, h, i: (b * nq + i, h)),
        out_shape=jax.ShapeDtypeStruct((t, DIFF_HEADS * DIFF_V_DIM), BF16),
        compiler_params=_cparams(("arbitrary", "arbitrary", "arbitrary")),
        name="diff_attn",
    )(q, k, v, *args)


def _peer_route_kernel(q_ref, sk_ref, r2_ref, e2_ref, n1_ref, f_ref, a_ref, b_ref, cand_ref, *, tt):
    neg = -jnp.inf
    k = PEER_TOPK
    kf = float(k)
    key_id = lax.broadcasted_iota(jnp.int32, (PEER_KEYS, LANES), 0).astype(F32)
    cand_id = lax.broadcasted_iota(jnp.int32, (k * k, LANES), 0).astype(F32)

    def extract(w, ids, dst_ref, by_index, want_rank):
        rank = jnp.full(w.shape, kf, F32) if want_rank else None
        tops = []
        for r in range(k):
            mx = jnp.max(w, axis=0, keepdims=True)
            tops.append(mx)
            if dst_ref is not None:
                dst_ref[r:r + 1, :] = mx
            hit = w == mx
            if by_index:
                first = jnp.min(jnp.where(hit, ids, float(w.shape[0])), axis=0, keepdims=True)
                hit = ids == first
            if want_rank:
                rank = jnp.where(hit, float(r), rank)
            w = jnp.where(hit, neg, w)
        return rank, tops, w

    def candidates(ga_ref, gb_ref):
        bvals = gb_ref[...]
        for x in range(k):
            cand_ref[x * k:(x + 1) * k, :] = ga_ref[x:x + 1, :] + bvals
        return cand_ref[...]

    def staircase_candidates(ga_ref, gb_ref):
        a = lambda lo, hi: ga_ref[lo:hi, :]
        b = lambda lo, hi: gb_ref[lo:hi, :]
        return jnp.concatenate([
            a(0, 1) + b(0, 8), a(0, 1) + b(8, 16), a(1, 2) + b(0, 8), a(2, 3) + b(0, 8), a(3, 4) + b(0, 8),
            a(0, 8) + b(0, 1), a(8, 16) + b(0, 1), a(0, 8) + b(1, 2), a(0, 8) + b(2, 3)], axis=0)

    def twin_bf16(v):
        bits = lax.bitcast_convert_type(v.astype(BF16).astype(F32), jnp.uint32)
        return bits | (bits >> 16)

    def emit(cols, s1, s2, rank2, n1, a0, b0, c_tops):
        zsum = jnp.zeros((1, LANES), F32)
        for c in c_tops:
            zsum = zsum + jnp.exp(c - c_tops[0])
        r2_ref[0, 0, :, cols] = pltpu.bitcast(rank2.astype(BF16), jnp.uint32)
        e2_ref[0, 0, :, cols] = pltpu.bitcast(jnp.exp(s2 - b0).astype(BF16), jnp.uint32)
        n1_ref[0, 0, :, cols] = twin_bf16(n1)
        f_ref[0, 0, :, cols] = twin_bf16(jnp.exp(s1 - a0) / zsum)

    count = lambda m: jnp.sum(jnp.where(m, 1.0, 0.0), axis=0, keepdims=True)

    def scores(cols):
        return (_nt_dot(sk_ref[0], q_ref[cols, 0:PEER_HALF], HIGHEST),
                _nt_dot(sk_ref[1], q_ref[cols, PEER_HALF:2 * PEER_HALF], HIGHEST))

    groups = [slice(g * LANES, (g + 1) * LANES) for g in range(tt // LANES)]
    any_tied = []
    for gidx, cols in enumerate(groups):
        ga_ref, gb_ref = a_ref.at[gidx], b_ref.at[gidx]
        s1, s2 = scores(cols)
        _, a_tops, w1 = extract(s1, key_id, ga_ref, False, False)
        rank2, b_tops, _ = extract(s2, key_id, gb_ref, False, True)
        _, c_tops, _ = extract(staircase_candidates(ga_ref, gb_ref), None, None, False, False)
        tau = c_tops[k - 1]
        n1 = jnp.zeros_like(s1)
        for y in range(k):
            n1 = n1 + jnp.where((s1 + b_tops[y]) >= tau, 1.0, 0.0)
        emit(cols, s1, s2, rank2, n1, a_tops[0], b_tops[0], c_tops)
        tied = ((count(w1 == neg) != kf) | (count(rank2 < kf) != kf)
                | (jnp.sum(n1, axis=0, keepdims=True) != kf))
        any_tied.append(jnp.max(jnp.where(tied, 1.0, 0.0)) > 0.0)

    for gidx, cols in enumerate(groups):
        @pl.when(any_tied[gidx])
        def _():
            ga_ref, gb_ref = a_ref.at[gidx], b_ref.at[gidx]
            s1, s2 = scores(cols)
            rank1, a_x, _ = extract(s1, key_id, ga_ref, True, True)
            rank2x, b_x, _ = extract(s2, key_id, gb_ref, True, True)
            rank3, c_x, _ = extract(candidates(ga_ref, gb_ref), cand_id, None, True, True)
            win = rank3 < kf
            n1x = jnp.zeros_like(s1)
            for x in range(k):
                n_x = count(win[x * k:(x + 1) * k])
                n1x = n1x + jnp.where(rank1 == float(x), n_x, 0.0)
            emit(cols, s1, s2, rank2x, n1x, a_x[0], b_x[0], c_x)


def _peer_route_call(q, subkeys, *, tt):
    t = q.shape[0]
    nt = t // tt
    spec = pl.BlockSpec((1, 1, PEER_KEYS, tt), lambda i, h: (h, i, 0, 0))
    pair_spec = pl.BlockSpec((1, 1, PEER_KEYS // 2, tt), lambda i, h: (h, i, 0, 0))
    tab = lambda rows: jax.ShapeDtypeStruct((PEER_HEADS, nt, rows, tt), jnp.uint32)
    return pl.pallas_call(
        functools.partial(_peer_route_kernel, tt=tt),
        grid=(nt, PEER_HEADS),
        in_specs=[
            pl.BlockSpec((tt, 2 * PEER_HALF), lambda i, h: (i, h)),
            pl.BlockSpec(subkeys.shape, lambda i, h: (0, 0, 0)),
        ],
        out_specs=[pair_spec, pair_spec, spec, spec],
        out_shape=[tab(PEER_KEYS // 2), tab(PEER_KEYS // 2), tab(PEER_KEYS), tab(PEER_KEYS)],
        scratch_shapes=[
            pltpu.VMEM((tt // LANES, PEER_TOPK, LANES), F32),
            pltpu.VMEM((tt // LANES, PEER_TOPK, LANES), F32),
            pltpu.VMEM((PEER_TOPK * PEER_TOPK, LANES), F32),
        ],
        compiler_params=_cparams(("arbitrary", "arbitrary")),
        name="peer_route",
    )(q, subkeys)


def _gelu(x):
    return 0.5 * x * (1.0 + lax.erf(x * (1.0 / math.sqrt(2.0))))


def _peer_expert_kernel(x_ref, gate_ref, ht_ref, u0_ref, u1_ref, vt0_ref, vt1_ref, n1_ref, f_ref, r2_ref,
                        e2_ref, o_ref, pre_a, pre_b, wa_ref, acc_ref, *, rows_per_tile):
    u_halves, vt_halves = (u0_ref, u1_ref), (vt0_ref, vt1_ref)
    s = pl.program_id(1)
    te, tt = pre_a.shape
    d = acc_ref.shape[0]
    blk = 16
    rows_per_piece = 2

    @pl.when(s == 0)
    def _():
        acc_ref[...] = jnp.zeros_like(acc_ref)
        pre_b[...] = jnp.zeros_like(pre_b)

    def step(pre_new, pre_old):
        halves = [slice(m * d // 2, (m + 1) * d // 2) for m in range(2)]
        zero = jnp.zeros((blk, LANES), BF16)
        as_bf16 = lambda words: pltpu.bitcast(words, BF16)
        for il in range(rows_per_tile):
            for lt in range(tt // LANES):
                lanes = slice(lt * LANES, (lt + 1) * LANES)
                row = lambda ref, h: as_bf16(jnp.broadcast_to(ref[h, 0, il:il + 1, lanes], (blk // 2, LANES)))
                n1b = [row(n1_ref, h) for h in range(PEER_HEADS)]
                fb = [row(f_ref, h) for h in range(PEER_HEADS)]
                for jb in range(PEER_KEYS // blk):
                    words = slice(jb * blk // 2, (jb + 1) * blk // 2)
                    w = zero
                    for h in range(PEER_HEADS):
                        hit = as_bf16(r2_ref[h, 0, words, lanes]) < n1b[h]
                        w = w + jnp.where(hit, as_bf16(e2_ref[h, 0, words, lanes]), zero) * fb[h]
                    rows = slice(il * PEER_KEYS + jb * blk, il * PEER_KEYS + (jb + 1) * blk)
                    wa_ref[rows, lanes] = w
        for m, u_ref in enumerate(u_halves):
            rows = slice(m * te // 2, (m + 1) * te // 2)
            pre_new[rows, :] = _dot(u_ref[...], ht_ref[...])
        piece_rows = rows_per_piece * PEER_KEYS
        for p in range(te // piece_rows):
            piece = slice(p * piece_rows, (p + 1) * piece_rows)
            for r0 in range(p * piece_rows, (p + 1) * piece_rows, blk):
                rows = slice(r0, r0 + blk)
                wa_ref[rows, :] = wa_ref[rows, :] * _gelu(pre_old[rows, :]).astype(BF16)
            for rows, vt_ref in zip(halves, vt_halves):
                acc_ref[rows, :] += _dot(vt_ref[0, :, piece], wa_ref[piece, :])

    @pl.when(s % 2 == 0)
    def _():
        step(pre_a, pre_b)

    @pl.when(s % 2 == 1)
    def _():
        step(pre_b, pre_a)

    @pl.when(s == pl.num_programs(1) - 1)
    def _():
        o_ref[...] = x_ref[...] + gate_ref[0] * acc_ref[...].T


def _peer_expert_call(x, gate, hb, u_b, vt_b, n1, f, r2, e2, seq, *, tt, te=1024):
    t, d = x.shape
    ne = u_b.shape[0] // te
    rows_per_tile = te // PEER_KEYS
    lag = lambda s, n: jnp.clip(s - n, 0, ne - 1)
    sel = pl.BlockSpec((PEER_HEADS, 1, rows_per_tile, tt), lambda i, s: (0, i, lag(s, 1), 0))
    full = pl.BlockSpec((PEER_HEADS, 1, PEER_KEYS // 2, tt), lambda i, s: (0, i, 0, 0))
    return pl.pallas_call(
        functools.partial(_peer_expert_kernel, rows_per_tile=rows_per_tile),
        grid=(t // tt, ne + 1),
        in_specs=[
            pl.BlockSpec((tt, d), lambda i, s: (i, 0)),
            pl.BlockSpec((1, 1, d), lambda i, s: ((i * tt) // seq, 0, 0)),
            pl.BlockSpec((d, tt), lambda i, s: (0, i)),
            pl.BlockSpec((te // 2, d), lambda i, s: (2 * lag(s, 0), 0)),
            pl.BlockSpec((te // 2, d), lambda i, s: (2 * lag(s, 0) + 1, 0)),
            pl.BlockSpec((1, d // 2, te), lambda i, s: (lag(s, 1), 0, 0)),
            pl.BlockSpec((1, d // 2, te), lambda i, s: (lag(s, 1), 1, 0)),
            sel, sel, full, full,
        ],
        out_specs=pl.BlockSpec((tt, d), lambda i, s: (i, 0)),
        out_shape=jax.ShapeDtypeStruct((t, d), F32),
        scratch_shapes=[
            pltpu.VMEM((te, tt), F32),
            pltpu.VMEM((te, tt), F32),
            pltpu.VMEM((te, tt), BF16),
            pltpu.VMEM((d, tt), F32),
        ],
        compiler_params=_cparams(("arbitrary", "arbitrary")),
        name="peer_experts",
    )(x, gate, hb, u_b, u_b, vt_b, vt_b, n1, f, r2, e2)


def _final_norm_kernel(x_ref, g_ref, o_ref):
    x = x_ref[...]
    ms = jnp.mean(x * x, axis=-1, keepdims=True)
    o_ref[...] = x * lax.rsqrt(ms + NORM_EPS) * g_ref[...]


def _final_norm_call(x, g, *, tm=512):
    t, d = x.shape
    return pl.pallas_call(
        _final_norm_kernel,
        grid=(t // tm,),
        in_specs=[pl.BlockSpec((tm, d), lambda i: (i, 0)), pl.BlockSpec((1, d), lambda i: (0, 0))],
        out_specs=pl.BlockSpec((tm, d), lambda i: (i, 0)),
        out_shape=jax.ShapeDtypeStruct((t, d), F32),
        compiler_params=_cparams(("arbitrary",)),
        name="final_norm",
    )(x, g.reshape(1, d))


def _pad_rows(a, rows):
    return jnp.pad(a, ((0, rows - a.shape[0]), (0, 0)))


def _pad_cols(a, cols):
    return jnp.pad(a, ((0, 0), (0, cols - a.shape[1])))


def _hybrid_mixer(x, g, shift, scale, batch, seq, w_in, conv_w, conv_b, conv_ln_g, conv_ln_b, mu, w0, w2,
                  a0, a2, g2, k_k, k_a, r_k, ln_g, ln_b):
    rw = RWKV_WIDTH
    c0 = 2 * CONV_WIDTH
    lora0 = c0 + 3 * rw
    l1, l2 = lora0 + DECAY_LORA, lora0 + DECAY_LORA + AAA_LORA
    w_conv = w_in[:, :c0].astype(BF16)
    w_rkv = w_in[:, c0:lora0].astype(BF16)
    w_lora = jnp.concatenate([
        _pad_cols(w_in[:, lora0:l1], LANES), _pad_cols(w_in[:, l1:l2], LANES),
        _pad_cols(w_in[:, l2:], 2 * LANES)], axis=1).astype(BF16)
    zc, zr, zl = _norm_matmul_call(x, g, shift, scale, [w_conv, w_rkv, w_lora], [F32, F32, F32], seq,
                                   name="hybrid_in_proj")
    ya = _conv_call(zc, conv_w, conv_b, conv_ln_g, conv_ln_b, batch, seq)
    mu_l = jnp.concatenate([
        _pad_cols(mu[None, lora0 - c0:l1 - c0], LANES), _pad_cols(mu[None, l1 - c0:l2 - c0], LANES),
        _pad_cols(mu[None, l2 - c0:], 2 * LANES)], axis=1)
    vec = lambda a: a.reshape(1, rw)
    head_id = jnp.arange(RWKV_GROUP * RWKV_HEAD) // RWKV_HEAD
    params = dict(
        mu_r=mu[None, :3 * rw], mu_l=mu_l, w0=vec(w0), a0=vec(a0), k_k=vec(k_k), k_a=vec(k_a),
        r_k=vec(r_k), ln_g=vec(ln_g), ln_b=vec(ln_b),
        w2=_pad_rows(w2, LANES).astype(BF16), a2=_pad_rows(a2, LANES).astype(BF16),
        g2=_pad_rows(g2, 2 * LANES).astype(BF16),
        seg=(head_id[:, None] == head_id[None, :]).astype(BF16))
    yb = _rwkv_call(zr, zl, params, batch, seq)
    return ya, yb


def kernel(x, c, positions, ada_w, ada_b, norm_mix_g, norm_ffn_g, hyb_w_in, conv_w, conv_b, conv_ln_g, conv_ln_b, rwkv_mu, rwkv_w0, rwkv_w2, rwkv_a0, rwkv_a2, rwkv_g2, rwkv_k_k, rwkv_k_a, rwkv_r_k, rwkv_ln_g, rwkv_ln_b, hyb_w_out, diff_w_qkv, diff_lq1, diff_lk1, diff_lq2, diff_lk2, diff_subln_g, diff_w_out, peer_w_q, peer_subkeys, peer_u, peer_v, final_g):
    batch, seq, d = x.shape
    depth = ada_w.shape[0]
    t = batch * seq
    xt = x.reshape(t, d)
    pos = positions.reshape(t, 1)
    mod = _ada_call(c, ada_w, ada_b).reshape(depth, batch, 6, 1, d)

    for l in range(depth):
        sh1, sc1, g1, sh2, sc2, g2 = (mod[l, :, j] for j in range(6))
        gmix = norm_mix_g[l].reshape(1, d)
        if l % 2 == 0:
            e = l // 2
            ya, yb = _hybrid_mixer(
                xt, gmix, sh1, sc1, batch, seq, hyb_w_in[e], conv_w[e], conv_b[e], conv_ln_g[e],
                conv_ln_b[e], rwkv_mu[e], rwkv_w0[e], rwkv_w2[e], rwkv_a0[e], rwkv_a2[e], rwkv_g2[e],
                rwkv_k_k[e], rwkv_k_a[e], rwkv_r_k[e], rwkv_ln_g[e], rwkv_ln_b[e])
            w_out = hyb_w_out[e].astype(BF16)
            xt = _out_proj_call(xt, g1, [ya.astype(BF16), yb.astype(BF16)],
                                [w_out[:CONV_WIDTH], w_out[CONV_WIDTH:]], seq)
        else:
            o = l // 2
            lambda_init = 0.8 - 0.6 * math.exp(-0.3 * l)
            wqkv = diff_w_qkv[o].astype(BF16)
            n = wqkv.shape[1] // 3
            q, k, v = _qkv_rope_call(xt, gmix, sh1, sc1, pos, wqkv[:, :n], wqkv[:, n:2 * n], wqkv[:, 2 * n:],
                                     seq)
            y = _attn_call(q, k, v, diff_lq1[o], diff_lk1[o], diff_lq2[o], diff_lk2[o], diff_subln_g[o],
                           lambda_init, batch, seq)
            xt = _out_proj_call(xt, g1, [y], [diff_w_out[o].astype(BF16)], seq)

        q, hb = _norm_matmul_call(xt, norm_ffn_g[l].reshape(1, d), sh2, sc2, [peer_w_q[l]], [F32], seq,
                                  split=True, emit_h=True, name="peer_query")
        r2, e2, n1, f = _peer_route_call(q, peer_subkeys[l], tt=PEER_TOKEN_TILE)
        v_tiles = peer_v[l].astype(BF16).reshape(-1, PEER_EXPERT_TILE, d).transpose(0, 2, 1)
        xt = _peer_expert_call(xt, g2, hb, peer_u[l].astype(BF16), v_tiles,
                               n1, f, r2, e2, seq, tt=PEER_TOKEN_TILE, te=PEER_EXPERT_TILE)

    return _final_norm_call(xt, final_g).reshape(batch, seq, d)
```

```python
import functools
import math

import jax
import jax.numpy as jnp
from jax import lax
from jax.experimental import pallas as pl
from jax.experimental.pallas import tpu as pltpu

F32 = jnp.float32
BF16 = jnp.bfloat16
HIGHEST = lax.Precision.HIGHEST

CONV_WIDTH = 512
CONV_KERNEL = 31
RWKV_WIDTH = 512
RWKV_HEAD = 64
RWKV_HEADS = 8
DECAY_LORA = 64
AAA_LORA = 64
GATE_LORA = 160
DIFF_HEADS = 8
DIFF_HEAD_DIM = 64
DIFF_V_DIM = 128
ROPE_DIM = 16
ROPE_THETA = 500000.0
ATTN_CHUNK = 64
PEER_HEADS = 8
PEER_KEYS = 128
PEER_HALF = 128
PEER_TOPK = 16
NORM_EPS = 1e-6
LN_EPS = 1e-5
RWKV_GN_EPS = 64e-5

LANES = 128
SCAN_CHUNK = 64
CONV_HALO = 32
PEER_EXPERT_TILE = 2048
PEER_ROUTE_TILES = 2
PEER_TOKEN_TILE = 256
VMEM_LIMIT = 48 * 1024 * 1024


def _cparams(semantics, flags=None):
    return pltpu.CompilerParams(dimension_semantics=semantics, vmem_limit_bytes=VMEM_LIMIT, flags=flags)


def _nt_dot(a, b, precision=None):
    return lax.dot_general(a, b, (((1,), (1,)), ((), ())), precision=precision,
                           preferred_element_type=F32)


def _dot(a, b, precision=None):
    return jnp.dot(a, b, precision=precision, preferred_element_type=F32)


def _sigmoid(x):
    return 1.0 / (1.0 + jnp.exp(-x))


def _ada_kernel(c_ref, w_ref, b_ref, o_ref):
    c = c_ref[...]
    cond = c * _sigmoid(c)
    o_ref[0] = _dot(cond, w_ref[0], HIGHEST) + b_ref[0]


def _ada_call(c, ada_w, ada_b):
    depth, d, n = ada_w.shape
    b = c.shape[0]
    tn = 1536
    return pl.pallas_call(
        _ada_kernel,
        grid=(depth, n // tn),
        in_specs=[
            pl.BlockSpec((b, d), lambda l, j: (0, 0)),
            pl.BlockSpec((1, d, tn), lambda l, j: (l, 0, j)),
            pl.BlockSpec((1, 1, tn), lambda l, j: (l, 0, j)),
        ],
        out_specs=pl.BlockSpec((1, b, tn), lambda l, j: (l, 0, j)),
        out_shape=jax.ShapeDtypeStruct((depth, b, n), F32),
        compiler_params=_cparams(("arbitrary", "arbitrary")),
        name="ada_mod",
    )(c, ada_w, ada_b.reshape(depth, 1, n))


def _modulate(x, g, shift, scale):
    ms = jnp.mean(x * x, axis=-1, keepdims=True)
    y = x * lax.rsqrt(ms + NORM_EPS)
    return (y * g) * (1.0 + scale) + shift


def _split_bf16(a):
    hi = a.astype(BF16)
    return hi, (a - hi.astype(F32)).astype(BF16)


def _norm_matmul_kernel(*refs, n_w, split, emit_h):
    x_ref, g_ref, sh_ref, sc_ref = refs[:4]
    w_refs = refs[4:4 + n_w]
    o_refs = refs[4 + n_w:]
    h = _modulate(x_ref[...], g_ref[...], sh_ref[0], sc_ref[0])
    hb = h.astype(BF16)
    if split:
        h_lo = (h - hb.astype(F32)).astype(BF16)
        for k in range(n_w // 2):
            w_hi, w_lo = w_refs[2 * k][...], w_refs[2 * k + 1][...]
            o_refs[k][...] = (_dot(hb, w_hi) + _dot(h_lo, w_hi) + _dot(hb, w_lo)).astype(o_refs[k].dtype)
    else:
        for w_ref, o_ref in zip(w_refs, o_refs[:n_w]):
            o_ref[...] = _dot(hb, w_ref[...]).astype(o_ref.dtype)
    if emit_h:
        o_refs[-1][...] = h.T.astype(BF16)


def _norm_matmul_call(x, g, shift, scale, weights, out_dtypes, seq, *, split=False, emit_h=False,
                      tm=512, name="norm_matmul"):
    t, d = x.shape
    if split:
        weights = [part for w in weights for part in _split_bf16(w)]
    in_specs = [
        pl.BlockSpec((tm, d), lambda i: (i, 0)),
        pl.BlockSpec((1, d), lambda i: (0, 0)),
        pl.BlockSpec((1, 1, d), lambda i: ((i * tm) // seq, 0, 0)),
        pl.BlockSpec((1, 1, d), lambda i: ((i * tm) // seq, 0, 0)),
    ]
    out_specs, out_shape = [], []
    for w in weights:
        in_specs.append(pl.BlockSpec(w.shape, lambda i: (0, 0)))
    for w, dt in zip(weights[::2] if split else weights, out_dtypes):
        n = w.shape[1]
        out_specs.append(pl.BlockSpec((tm, n), lambda i: (i, 0)))
        out_shape.append(jax.ShapeDtypeStruct((t, n), dt))
    if emit_h:
        out_specs.append(pl.BlockSpec((d, tm), lambda i: (0, i)))
        out_shape.append(jax.ShapeDtypeStruct((d, t), BF16))
    return pl.pallas_call(
        functools.partial(_norm_matmul_kernel, n_w=len(weights), split=split, emit_h=emit_h),
        grid=(t // tm,),
        in_specs=in_specs,
        out_specs=out_specs,
        out_shape=out_shape,
        compiler_params=_cparams(("arbitrary",)),
        name=name,
    )(x, g, shift, scale, *weights)


def _out_proj_kernel(*refs, n_y):
    x_ref, gate_ref = refs[:2]
    y_refs = refs[2:2 + n_y]
    w_refs = refs[2 + n_y:2 + 2 * n_y]
    o_ref = refs[2 + 2 * n_y]
    acc = _dot(y_refs[0][...], w_refs[0][...])
    for y_ref, w_ref in zip(y_refs[1:], w_refs[1:]):
        acc = acc + _dot(y_ref[...], w_ref[...])
    o_ref[...] = x_ref[...] + gate_ref[0] * acc


def _out_proj_call(x, gate, ys, ws, seq, *, tm=512):
    t, d = x.shape
    in_specs = [
        pl.BlockSpec((tm, d), lambda i: (i, 0)),
        pl.BlockSpec((1, 1, d), lambda i: ((i * tm) // seq, 0, 0)),
    ]
    for y in ys:
        in_specs.append(pl.BlockSpec((tm, y.shape[1]), lambda i: (i, 0)))
    for w in ws:
        in_specs.append(pl.BlockSpec(w.shape, lambda i: (0, 0)))
    return pl.pallas_call(
        functools.partial(_out_proj_kernel, n_y=len(ys)),
        grid=(t // tm,),
        in_specs=in_specs,
        out_specs=pl.BlockSpec((tm, d), lambda i: (i, 0)),
        out_shape=jax.ShapeDtypeStruct((t, d), F32),
        compiler_params=_cparams(("arbitrary",)),
        name="out_proj",
    )(x, gate, *ys, *ws)


def _conv_kernel(z_ref, w_ref, b_ref, g_ref, beta_ref, o_ref, ext_ref, *, ts):
    width = CONV_WIDTH

    @pl.when(pl.program_id(1) == 0)
    def _():
        ext_ref[0:CONV_HALO, :] = jnp.zeros((CONV_HALO, width), F32)

    z = z_ref[...]
    u = z[:, :width] * _sigmoid(z[:, width:])
    ext_ref[CONV_HALO:CONV_HALO + ts, :] = u
    base = CONV_HALO - (CONV_KERNEL - 1)
    rows = 64
    for cb in range(width // LANES):
        cs = slice(cb * LANES, (cb + 1) * LANES)
        for rb in range(ts // rows):
            acc = jnp.zeros((rows, LANES), F32)
            for j in range(CONV_KERNEL):
                start = rb * rows + base + j
                acc = acc + w_ref[j:j + 1, cs] * ext_ref[start:start + rows, cs]
            o_ref[rb * rows:(rb + 1) * rows, cs] = acc
    conv = o_ref[...] + b_ref[...]
    mu = jnp.mean(conv, axis=-1, keepdims=True)
    dlt = conv - mu
    var = jnp.mean(dlt * dlt, axis=-1, keepdims=True)
    y = dlt * lax.rsqrt(var + LN_EPS) * g_ref[...] + beta_ref[...]
    o_ref[...] = y * _sigmoid(y)
    ext_ref[0:CONV_HALO, :] = ext_ref[ts:ts + CONV_HALO, :]


def _conv_call(z, conv_w, conv_b, ln_g, ln_b, batch, seq, *, ts=256):
    t = z.shape[0]
    width = CONV_WIDTH
    nts = seq // ts
    vec = lambda a: a.reshape(1, width)
    return pl.pallas_call(
        functools.partial(_conv_kernel, ts=ts),
        grid=(batch, nts),
        in_specs=[
            pl.BlockSpec((ts, 2 * width), lambda b, i: (b * nts + i, 0)),
            pl.BlockSpec((CONV_KERNEL, width), lambda b, i: (0, 0)),
            pl.BlockSpec((1, width), lambda b, i: (0, 0)),
            pl.BlockSpec((1, width), lambda b, i: (0, 0)),
            pl.BlockSpec((1, width), lambda b, i: (0, 0)),
        ],
        out_specs=pl.BlockSpec((ts, width), lambda b, i: (b * nts + i, 0)),
        out_shape=jax.ShapeDtypeStruct((t, width), F32),
        scratch_shapes=[pltpu.VMEM((ts + CONV_HALO, width), F32)],
        compiler_params=_cparams(("arbitrary", "arbitrary")),
        name="conformer_conv",
    )(z, conv_w, vec(conv_b), vec(ln_g), vec(ln_b))


def _softplus(x):
    return jnp.maximum(x, 0.0) + jnp.log(1.0 + jnp.exp(-jnp.abs(x)))


def _unit_lower_inverses(mats, row, col, size):
    eye = (row == col).astype(F32)
    blk = lambda m: (row // m) == (col // m)
    bdot = lambda p, q: _dot(p.astype(BF16), q.astype(BF16))
    n1 = [jnp.where(blk(8), a, 0.0) for a in mats]
    t = [eye + x for x in n1]
    n2 = [bdot(x, x) for x in n1]
    t = [x + bdot(x, y) for x, y in zip(t, n2)]
    n4 = [bdot(x, x) for x in n2]
    t = [x + bdot(x, y) for x, y in zip(t, n4)]
    m = 8
    while m < size:
        new = blk(2 * m) & jnp.logical_not(blk(m))
        et = [bdot(jnp.where(new, a, 0.0), x) for a, x in zip(mats, t)]
        t = [x + bdot(x, y) for x, y in zip(t, et)]
        m *= 2
    return t


def _split_dot(x, w_bf16, parts):
    acc = None
    for _ in range(parts):
        piece = x.astype(BF16)
        term = _dot(piece, w_bf16)
        acc = term if acc is None else acc + term
        x = x - piece.astype(F32)
    return acc


RWKV_GROUP = 4
RWKV_BATCH_TILE = 2


def _rwkv_kernel(zr_ref, zl_ref, mur_ref, mul_ref, w0_ref, a0_ref, kk_ref, ka_ref, rk_ref,
                 lng_ref, lnb_ref, w2_ref, a2_ref, g2_ref, seg_ref, o_ref,
                 extr_ref, extl_ref, state_ref):
    c = SCAN_CHUNK
    rw = RWKV_WIDTH
    n = RWKV_HEAD
    nb = RWKV_BATCH_TILE
    gw = RWKV_GROUP * n
    groups = rw // gw

    @pl.when(pl.program_id(1) == 0)
    def _():
        extr_ref[:, 0:8, :] = jnp.zeros((nb, 8, 3 * rw), F32)
        extl_ref[:, 0:8, :] = jnp.zeros((nb, 8, rw), F32)
        state_ref[...] = jnp.zeros_like(state_ref)

    zr_rows, zl_rows = [], []
    for b in range(nb):
        zr = zr_ref[b]
        zl = zl_ref[b]
        extr_ref[b, 8:8 + c, :] = zr
        extl_ref[b, 8:8 + c, :] = zl
        zr_rows.append(zr + (extr_ref[b, 7:7 + c, :] - zr) * mur_ref[...])
        zl_rows.append(zl + (extl_ref[b, 7:7 + c, :] - zl) * mul_ref[...])
        extr_ref[b, 0:8, :] = extr_ref[b, c:c + 8, :]
        extl_ref[b, 0:8, :] = extl_ref[b, c:c + 8, :]
    zr = jnp.concatenate(zr_rows, axis=0)
    zl = jnp.concatenate(zl_rows, axis=0)

    r = zr[:, 0:rw]
    k = zr[:, rw:2 * rw]
    v = zr[:, 2 * rw:3 * rw]
    wd = zl[:, 0:LANES]
    ad = zl[:, LANES:2 * LANES]
    gd = zl[:, 2 * LANES:4 * LANES]
    seg = seg_ref[...]

    def segsum(x, parts):
        return jnp.concatenate([_split_dot(x[:, g * gw:(g + 1) * gw], seg, parts) for g in range(groups)],
                               axis=1)

    w_raw = -_softplus(-(w0_ref[...] + _dot(jnp.tanh(wd).astype(BF16), w2_ref[...]))) - 0.5
    lw = -jnp.exp(w_raw)
    alpha = _sigmoid(a0_ref[...] + _dot(ad.astype(BF16), a2_ref[...]))
    gate = _dot(_sigmoid(gd).astype(BF16), g2_ref[...])
    kkf = k * kk_ref[...]
    kk = kkf / jnp.maximum(jnp.sqrt(segsum(kkf * kkf, 3)), 1e-12)
    kp = k * (1.0 + (alpha - 1.0) * ka_ref[...])
    bonus = segsum(r * kp * rk_ref[...], 3) * v
    av = -kk
    bv = kk * alpha

    trow = lax.broadcasted_iota(jnp.int32, (nb * c, nb * c), 0)
    tcol = lax.broadcasted_iota(jnp.int32, (nb * c, nb * c), 1)
    same_seq_incl = ((trow // c) == (tcol // c)) & (tcol <= trow)
    cum = _split_dot_lhs_exact(jnp.where(same_seq_incl, 1.0, 0.0).astype(BF16), lw)
    tot = jnp.concatenate([jnp.broadcast_to(cum[(b + 1) * c - 1:(b + 1) * c, :], (c, rw)) for b in range(nb)],
                          axis=0)
    g_in = jnp.exp(cum)
    g_inp = jnp.exp(cum - lw)
    g_out = jnp.exp(-cum)
    g_end = jnp.exp(tot - cum)
    g_tot = jnp.exp(tot)
    a_in = av * g_inp
    r_in = r * g_in
    b_out = bv * g_out
    k_out = kp * g_out
    b_end = bv * g_end
    k_end = kp * g_end

    row = lax.broadcasted_iota(jnp.int32, (gw, gw), 0)
    col = lax.broadcasted_iota(jnp.int32, (gw, gw), 1)
    same_head = (row // n) == (col // n)
    strict = same_head & ((col % n) < (row % n))
    incl = same_head & ((col % n) <= (row % n))

    def expand(x):
        return jnp.where(same_head, jnp.concatenate([x] * RWKV_GROUP, axis=0), 0.0)

    bf = lambda x: x.astype(BF16)
    chains = [(b, g) for b in range(nb) for g in range(groups)]
    pick = lambda x: [expand(x[b * c:(b + 1) * c, g * gw:(g + 1) * gw]) for b, g in chains]
    each = lambda fn, *lists: [fn(*args) for args in zip(*lists)]
    a_bd, r_bd = each(bf, pick(a_in)), each(bf, pick(r_in))
    bo_bd, ko_bd = each(bf, pick(b_out)), each(bf, pick(k_out))
    v_f = pick(v)
    v_bd = each(bf, v_f)
    st = [state_ref[b, g] for b, g in chains]
    st_b = each(bf, st)
    a_ab = each(lambda p, q: jnp.where(strict, _nt_dot(p, q), 0.0), a_bd, bo_bd)
    a_ak = each(lambda p, q: bf(jnp.where(strict, _nt_dot(p, q), 0.0)), a_bd, ko_bd)
    a_rb = each(lambda p, q: bf(jnp.where(incl, _nt_dot(p, q), 0.0)), r_bd, bo_bd)
    a_rk = each(lambda p, q: bf(jnp.where(incl, _nt_dot(p, q), 0.0)), r_bd, ko_bd)
    ph_a = each(_nt_dot, a_bd, st_b)
    ph_r = each(_nt_dot, r_bd, st_b)
    tinv = _unit_lower_inverses(a_ab, row, col, n)
    rhs = each(lambda p, m, w: bf(p + _dot(m, w)), ph_a, a_ak, v_bd)
    u = each(lambda t, x: _dot(bf(t), x), tinv, rhs)
    y_bd = each(lambda p, m1, uu, m2, w: p + _dot(m1, bf(uu)) + _dot(m2, w), ph_r, a_rb, u, a_rk, v_bd)
    be_bd, ke_bd = each(bf, pick(b_end)), each(bf, pick(k_end))
    for (b, g), s0, uu, vf, be, ke in zip(chains, st, u, v_f, be_bd, ke_bd):
        state_ref[b, g] = (s0 * g_tot[b * c:b * c + 1, g * gw:(g + 1) * gw]
                           + _dot(bf(uu.T), be) + _dot(bf(vf.T), ke))
    y_g = [sum(m[i * c:(i + 1) * c] for i in range(RWKV_GROUP)) for m in y_bd]
    y = jnp.concatenate([jnp.concatenate(y_g[b * groups:(b + 1) * groups], axis=1) for b in range(nb)],
                        axis=0)

    inv_n = 1.0 / n
    mu_y = segsum(y, 3) * inv_n
    dy = y - mu_y
    var_y = segsum(dy * dy, 3) * inv_n
    yn = dy * lax.rsqrt(var_y + RWKV_GN_EPS)
    out = (yn * lng_ref[...] + lnb_ref[...] + bonus) * gate
    for b in range(nb):
        o_ref[b] = out[b * c:(b + 1) * c]


def _split_dot_lhs_exact(w_bf16, x):
    acc = None
    for _ in range(3):
        piece = x.astype(BF16)
        term = _dot(w_bf16, piece)
        acc = term if acc is None else acc + term
        x = x - piece.astype(F32)
    return acc


def _rwkv_call(zr, zl, p, batch, seq):
    c = SCAN_CHUNK
    rw = RWKV_WIDTH
    nb = RWKV_BATCH_TILE
    nc = seq // c
    gw = RWKV_GROUP * RWKV_HEAD
    const = lambda shape: pl.BlockSpec(shape, lambda b, i: (0,) * len(shape))
    vec = const((1, rw))
    out = pl.pallas_call(
        _rwkv_kernel,
        grid=(batch // nb, nc),
        in_specs=[
            pl.BlockSpec((nb, c, 3 * rw), lambda b, i: (b, i, 0)),
            pl.BlockSpec((nb, c, rw), lambda b, i: (b, i, 0)),
            const((1, 3 * rw)), vec,
            vec, vec, vec, vec, vec, vec, vec,
            const((LANES, rw)), const((LANES, rw)), const((2 * LANES, rw)),
            const((gw, gw)),
        ],
        out_specs=pl.BlockSpec((nb, c, rw), lambda b, i: (b, i, 0)),
        out_shape=jax.ShapeDtypeStruct((batch, seq, rw), F32),
        scratch_shapes=[
            pltpu.VMEM((nb, c + 8, 3 * rw), F32),
            pltpu.VMEM((nb, c + 8, rw), F32),
            pltpu.VMEM((nb, rw // gw, gw, gw), F32),
        ],
        compiler_params=_cparams(("arbitrary", "arbitrary")),
        name="rwkv7_mix",
    )(zr.reshape(batch, seq, 3 * rw), zl.reshape(batch, seq, rw), p["mu_r"], p["mu_l"], p["w0"], p["a0"],
      p["k_k"], p["k_a"], p["r_k"], p["ln_g"], p["ln_b"], p["w2"], p["a2"], p["g2"], p["seg"])
    return out.reshape(batch * seq, rw)


def _qkv_rope_kernel(x_ref, g_ref, sh_ref, sc_ref, pos_ref, inv_ref, sgn_ref, wq_ref, wk_ref, wv_ref,
                     q_ref, k_ref, v_ref):
    h = _modulate(x_ref[...], g_ref[...], sh_ref[0], sc_ref[0]).astype(BF16)
    ang = pos_ref[...].astype(F32) * inv_ref[...]
    cosf = jnp.cos(ang)
    sinf = jnp.sin(ang) * sgn_ref[...]
    lane = lax.broadcasted_iota(jnp.int32, ang.shape, 1)
    low = (lane % DIFF_HEAD_DIM) < (ROPE_DIM // 2)
    half = ROPE_DIM // 2

    def rope(w_ref, o_ref, scale):
        z = _dot(h, w_ref[...])
        for cb in range(z.shape[1] // LANES):
            zc = z[:, cb * LANES:(cb + 1) * LANES]
            partner = jnp.where(low, pltpu.roll(zc, LANES - half, axis=1), pltpu.roll(zc, half, axis=1))
            o_ref[:, cb * LANES:(cb + 1) * LANES] = ((zc * cosf + partner * sinf) * scale).astype(o_ref.dtype)

    rope(wq_ref, q_ref, DIFF_HEAD_DIM ** -0.5)
    rope(wk_ref, k_ref, 1.0)
    v_ref[...] = _dot(h, wv_ref[...]).astype(v_ref.dtype)


def _qkv_rope_call(x, g, shift, scale, pos, wq, wk, wv, seq, *, tm=512):
    t, d = x.shape
    half = ROPE_DIM // 2
    inv = 1.0 / (ROPE_THETA ** (jnp.arange(0, ROPE_DIM, 2, dtype=F32) / ROPE_DIM))
    dl = jnp.arange(LANES) % DIFF_HEAD_DIM
    inv_pat = jnp.where(dl < ROPE_DIM, inv[dl % half], 0.0).reshape(1, LANES).astype(F32)
    sgn_pat = jnp.where(dl < half, -1.0, 1.0).reshape(1, LANES).astype(F32)
    row = lambda n: pl.BlockSpec((tm, n), lambda i: (i, 0))
    full = lambda a: pl.BlockSpec(a.shape, lambda i: (0, 0))
    mod = pl.BlockSpec((1, 1, d), lambda i: ((i * tm) // seq, 0, 0))
    n = wq.shape[1]
    return pl.pallas_call(
        _qkv_rope_kernel,
        grid=(t // tm,),
        in_specs=[row(d), full(g), mod, mod, row(1), full(inv_pat), full(sgn_pat),
                  full(wq), full(wk), full(wv)],
        out_specs=[row(n), row(n), row(n)],
        out_shape=[jax.ShapeDtypeStruct((t, n), BF16)] * 3,
        compiler_params=_cparams(("arbitrary",)),
        name="qkv_rope",
    )(x, g, shift, scale, pos, inv_pat, sgn_pat, wq, wk, wv)


def _attn_kernel(q_ref, k_ref, v_ref, lq1_ref, lk1_ref, lq2_ref, lk2_ref, sg_ref, o_ref,
                 *, tq, seq, lambda_init):
    i = pl.program_id(2)
    kb = 2 * tq
    q = q_ref[...]
    lane = lax.broadcasted_iota(jnp.int32, q.shape, 1)
    zero = jnp.zeros_like(q)
    qm = (jnp.where(lane < DIFF_HEAD_DIM, q, zero), jnp.where(lane >= DIFF_HEAD_DIM, q, zero))
    f32sum = lambda a, b: jnp.sum(a[...] * b[...], axis=-1, keepdims=True)
    lam = jnp.exp(f32sum(lq1_ref, lk1_ref)) - jnp.exp(f32sum(lq2_ref, lk2_ref)) + lambda_init
    row_chunk = (i * tq + lax.broadcasted_iota(jnp.int32, (tq, kb), 0)) // ATTN_CHUNK
    col_in_block = lax.broadcasted_iota(jnp.int32, (tq, kb), 1)

    def attend(nblk):
        outs = []
        for mp in range(2):
            s = [_nt_dot(qm[mp], k_ref[j * kb:(j + 1) * kb, :]) for j in range(nblk)]
            col_chunk = ((nblk - 1) * kb + col_in_block) // ATTN_CHUNK
            s[-1] = jnp.where(col_chunk <= row_chunk, s[-1], -jnp.inf)
            m = jnp.max(s[0], axis=-1, keepdims=True)
            for sj in s[1:]:
                m = jnp.maximum(m, jnp.max(sj, axis=-1, keepdims=True))
            l = jnp.zeros_like(m)
            acc = jnp.zeros((tq, DIFF_V_DIM), F32)
            for j, sj in enumerate(s):
                p = jnp.exp(sj - m)
                l = l + jnp.sum(p, axis=-1, keepdims=True)
                acc = acc + _dot(p.astype(BF16), v_ref[j * kb:(j + 1) * kb, :])
            outs.append(acc / l)
        o = outs[0] - lam * outs[1]
        ms = jnp.mean(o * o, axis=-1, keepdims=True)
        o = o * lax.rsqrt(ms + NORM_EPS) * sg_ref[...] * (1.0 - lambda_init)
        o_ref[...] = o.astype(o_ref.dtype)

    for nblk in range(1, seq // kb + 1):
        pl.when(i // 2 == nblk - 1)(functools.partial(attend, nblk))


def _attn_call(q, k, v, lq1, lk1, lq2, lk2, subln_g, lambda_init, batch, seq, *, tq=256):
    t = q.shape[0]
    nq = seq // tq
    assert seq % (2 * tq) == 0
    vec = lambda a: a.reshape(1, -1)
    small = lambda a: pl.BlockSpec(a.shape, lambda b, h, i: (0, 0))
    kv = pl.BlockSpec((seq, DIFF_V_DIM), lambda b, h, i: (b, h))
    args = [vec(lq1), vec(lk1), vec(lq2), vec(lk2), vec(subln_g)]
    return pl.pallas_call(
        functools.partial(_attn_kernel, tq=tq, seq=seq, lambda_init=lambda_init),
        grid=(batch, DIFF_HEADS, nq),
        in_specs=[pl.BlockSpec((tq, DIFF_V_DIM), lambda b, h, i: (b * nq + i, h)), kv, kv]
                 + [small(a) for a in args],
        out_specs=pl.BlockSpec((tq, DIFF_V_DIM), lambda b, h, i: (b * nq + i, h)),
        out_shape=jax.ShapeDtypeStruct((t, DIFF_HEADS * DIFF_V_DIM), BF16),
        compiler_params=_cparams(("arbitrary", "arbitrary", "arbitrary")),
        name="diff_attn",
    )(q, k, v, *args)


def _peer_route_kernel(q_ref, sk_ref, r2_ref, e2_ref, n1_ref, f_ref, a_ref, b_ref, cand_ref, *, tt, tiles):
    neg = -jnp.inf
    k = PEER_TOPK
    kf = float(k)
    key_id = lax.broadcasted_iota(jnp.int32, (PEER_KEYS, LANES), 0).astype(F32)
    cand_id = lax.broadcasted_iota(jnp.int32, (k * k, LANES), 0).astype(F32)

    def extract(w, ids, dst_ref, by_index, want_rank):
        rank = jnp.full(w.shape, kf, F32) if want_rank else None
        tops = []
        for r in range(k):
            mx = jnp.max(w, axis=0, keepdims=True)
            tops.append(mx)
            if dst_ref is not None:
                dst_ref[r:r + 1, :] = mx
            hit = w == mx
            if by_index:
                first = jnp.min(jnp.where(hit, ids, float(w.shape[0])), axis=0, keepdims=True)
                hit = ids == first
            if want_rank:
                rank = jnp.where(hit, float(r), rank)
            w = jnp.where(hit, neg, w)
        return rank, tops, w

    def candidates(ga_ref, gb_ref):
        bvals = gb_ref[...]
        for x in range(k):
            cand_ref[x * k:(x + 1) * k, :] = ga_ref[x:x + 1, :] + bvals
        return cand_ref[...]

    def staircase_candidates(ga_ref, gb_ref):
        a = lambda lo, hi: ga_ref[lo:hi, :]
        b = lambda lo, hi: gb_ref[lo:hi, :]
        return jnp.concatenate([
            a(0, 1) + b(0, 8), a(0, 1) + b(8, 16), a(1, 2) + b(0, 8), a(2, 3) + b(0, 8), a(3, 4) + b(0, 8),
            a(0, 8) + b(0, 1), a(8, 16) + b(0, 1), a(0, 8) + b(1, 2), a(0, 8) + b(2, 3)], axis=0)

    def twin_bf16(v):
        bits = lax.bitcast_convert_type(v.astype(BF16).astype(F32), jnp.uint32)
        return bits | (bits >> 16)

    def emit(dst, s1, s2, rank2, n1, a0, b0, c_tops):
        tile, cols = dst
        zsum = jnp.zeros((1, LANES), F32)
        for c in c_tops:
            zsum = zsum + jnp.exp(c - c_tops[0])
        r2_ref[0, tile, :, cols] = pltpu.bitcast(rank2.astype(BF16), jnp.uint32)
        e2_ref[0, tile, :, cols] = pltpu.bitcast(jnp.exp(s2 - b0).astype(BF16), jnp.uint32)
        n1_ref[0, tile, :, cols] = twin_bf16(n1)
        f_ref[0, tile, :, cols] = twin_bf16(jnp.exp(s1 - a0) / zsum)

    count = lambda m: jnp.sum(jnp.where(m, 1.0, 0.0), axis=0, keepdims=True)

    def scores(rows):
        return (_nt_dot(sk_ref[0], q_ref[rows, 0:PEER_HALF], HIGHEST),
                _nt_dot(sk_ref[1], q_ref[rows, PEER_HALF:2 * PEER_HALF], HIGHEST))

    per_tile = tt // LANES
    groups = [(slice(g * LANES, (g + 1) * LANES),
               (g // per_tile, slice((g % per_tile) * LANES, (g % per_tile + 1) * LANES)))
              for g in range(tiles * per_tile)]
    any_tied = []
    for gidx, (rows, dst) in enumerate(groups):
        ga_ref, gb_ref = a_ref.at[gidx], b_ref.at[gidx]
        s1, s2 = scores(rows)
        _, a_tops, w1 = extract(s1, key_id, ga_ref, False, False)
        rank2, b_tops, _ = extract(s2, key_id, gb_ref, False, True)
        _, c_tops, _ = extract(staircase_candidates(ga_ref, gb_ref), None, None, False, False)
        tau = c_tops[k - 1]
        n1 = jnp.zeros_like(s1)
        for y in range(k):
            n1 = n1 + jnp.where((s1 + b_tops[y]) >= tau, 1.0, 0.0)
        emit(dst, s1, s2, rank2, n1, a_tops[0], b_tops[0], c_tops)
        tied = ((count(w1 == neg) != kf) | (count(rank2 < kf) != kf)
                | (jnp.sum(n1, axis=0, keepdims=True) != kf))
        any_tied.append(jnp.max(jnp.where(tied, 1.0, 0.0)) > 0.0)

    for gidx, (rows, dst) in enumerate(groups):
        @pl.when(any_tied[gidx])
        def _():
            ga_ref, gb_ref = a_ref.at[gidx], b_ref.at[gidx]
            s1, s2 = scores(rows)
            rank1, a_x, _ = extract(s1, key_id, ga_ref, True, True)
            rank2x, b_x, _ = extract(s2, key_id, gb_ref, True, True)
            rank3, c_x, _ = extract(candidates(ga_ref, gb_ref), cand_id, None, True, True)
            win = rank3 < kf
            n1x = jnp.zeros_like(s1)
            for x in range(k):
                n_x = count(win[x * k:(x + 1) * k])
                n1x = n1x + jnp.where(rank1 == float(x), n_x, 0.0)
            emit(dst, s1, s2, rank2x, n1x, a_x[0], b_x[0], c_x)


def _peer_route_call(q, subkeys, *, tt, tiles=PEER_ROUTE_TILES):
    t = q.shape[0]
    nt = t // tt
    spec = pl.BlockSpec((1, tiles, PEER_KEYS, tt), lambda i, h: (h, i, 0, 0))
    pair_spec = pl.BlockSpec((1, tiles, PEER_KEYS // 2, tt), lambda i, h: (h, i, 0, 0))
    tab = lambda rows: jax.ShapeDtypeStruct((PEER_HEADS, nt, rows, tt), jnp.uint32)
    return pl.pallas_call(
        functools.partial(_peer_route_kernel, tt=tt, tiles=tiles),
        grid=(nt // tiles, PEER_HEADS),
        in_specs=[
            pl.BlockSpec((tt * tiles, 2 * PEER_HALF), lambda i, h: (i, h)),
            pl.BlockSpec(subkeys.shape, lambda i, h: (0, 0, 0)),
        ],
        out_specs=[pair_spec, pair_spec, spec, spec],
        out_shape=[tab(PEER_KEYS // 2), tab(PEER_KEYS // 2), tab(PEER_KEYS), tab(PEER_KEYS)],
        scratch_shapes=[
            pltpu.VMEM((tiles * tt // LANES, PEER_TOPK, LANES), F32),
            pltpu.VMEM((tiles * tt // LANES, PEER_TOPK, LANES), F32),
            pltpu.VMEM((PEER_TOPK * PEER_TOPK, LANES), F32),
        ],
        compiler_params=_cparams(("arbitrary", "arbitrary")),
        name="peer_route",
    )(q, subkeys)


def _gelu(x):
    return 0.5 * x * (1.0 + lax.erf(x * (1.0 / math.sqrt(2.0))))


def _peer_expert_kernel(x_ref, gate_ref, ht_ref, u0_ref, u1_ref, vt0_ref, vt1_ref, n1_ref, f_ref, r2_ref,
                        e2_ref, o_ref, pre_a, pre_b, wa_ref, acc_ref, *, rows_per_tile):
    u_halves, vt_halves = (u0_ref, u1_ref), (vt0_ref, vt1_ref)
    s = pl.program_id(1)
    te, tt = pre_a.shape
    d = acc_ref.shape[0]
    blk = 16
    rows_per_piece = 2

    @pl.when(s == 0)
    def _():
        acc_ref[...] = jnp.zeros_like(acc_ref)
        pre_b[...] = jnp.zeros_like(pre_b)

    def step(pre_new, pre_old):
        halves = [slice(m * d // 2, (m + 1) * d // 2) for m in range(2)]
        zero = jnp.zeros((blk, LANES), BF16)
        as_bf16 = lambda words: pltpu.bitcast(words, BF16)
        for il in range(rows_per_tile):
            for lt in range(tt // LANES):
                lanes = slice(lt * LANES, (lt + 1) * LANES)
                row = lambda ref, h: as_bf16(jnp.broadcast_to(ref[h, 0, il:il + 1, lanes], (blk // 2, LANES)))
                n1b = [row(n1_ref, h) for h in range(PEER_HEADS)]
                fb = [row(f_ref, h) for h in range(PEER_HEADS)]
                for jb in range(PEER_KEYS // blk):
                    words = slice(jb * blk // 2, (jb + 1) * blk // 2)
                    w = zero
                    for h in range(PEER_HEADS):
                        hit = as_bf16(r2_ref[h, 0, words, lanes]) < n1b[h]
                        w = w + jnp.where(hit, as_bf16(e2_ref[h, 0, words, lanes]), zero) * fb[h]
                    rows = slice(il * PEER_KEYS + jb * blk, il * PEER_KEYS + (jb + 1) * blk)
                    wa_ref[rows, lanes] = w
        pre_rows = 512
        for m, u_ref in enumerate(u_halves):
            for r0 in range(0, te // 2, pre_rows):
                pre_new[m * te // 2 + r0:m * te // 2 + r0 + pre_rows, :] = _dot(u_ref[r0:r0 + pre_rows, :],
                                                                               ht_ref[...])
        piece_rows = rows_per_piece * PEER_KEYS
        for p in range(te // piece_rows):
            piece = slice(p * piece_rows, (p + 1) * piece_rows)
            for r0 in range(p * piece_rows, (p + 1) * piece_rows, blk):
                rows = slice(r0, r0 + blk)
                wa_ref[rows, :] = wa_ref[rows, :] * _gelu(pre_old[rows, :]).astype(BF16)
            for rows, vt_ref in zip(halves, vt_halves):
                acc_ref[rows, :] += _dot(vt_ref[0, :, piece], wa_ref[piece, :])

    @pl.when(s % 2 == 0)
    def _():
        step(pre_a, pre_b)

    @pl.when(s % 2 == 1)
    def _():
        step(pre_b, pre_a)

    @pl.when(s == pl.num_programs(1) - 1)
    def _():
        o_ref[...] = x_ref[...] + gate_ref[0] * acc_ref[...].T


def _peer_expert_call(x, gate, hb, u_b, vt_b, n1, f, r2, e2, seq, *, tt, te=1024):
    t, d = x.shape
    ne = u_b.shape[0] // te
    rows_per_tile = te // PEER_KEYS
    lag = lambda s, n: jnp.clip(s - n, 0, ne - 1)
    sel = pl.BlockSpec((PEER_HEADS, 1, rows_per_tile, tt), lambda i, s: (0, i, lag(s, 1), 0))
    full = pl.BlockSpec((PEER_HEADS, 1, PEER_KEYS // 2, tt), lambda i, s: (0, i, 0, 0))
    return pl.pallas_call(
        functools.partial(_peer_expert_kernel, rows_per_tile=rows_per_tile),
        grid=(t // tt, ne + 1),
        in_specs=[
            pl.BlockSpec((tt, d), lambda i, s: (i, 0)),
            pl.BlockSpec((1, 1, d), lambda i, s: ((i * tt) // seq, 0, 0)),
            pl.BlockSpec((d, tt), lambda i, s: (0, i)),
            pl.BlockSpec((te // 2, d), lambda i, s: (2 * lag(s, 0), 0)),
            pl.BlockSpec((te // 2, d), lambda i, s: (2 * lag(s, 0) + 1, 0)),
            pl.BlockSpec((1, d // 2, te), lambda i, s: (lag(s, 1), 0, 0)),
            pl.BlockSpec((1, d // 2, te), lambda i, s: (lag(s, 1), 1, 0)),
            sel, sel, full, full,
        ],
        out_specs=pl.BlockSpec((tt, d), lambda i, s: (i, 0)),
        out_shape=jax.ShapeDtypeStruct((t, d), F32),
        scratch_shapes=[
            pltpu.VMEM((te, tt), F32),
            pltpu.VMEM((te, tt), F32),
            pltpu.VMEM((te, tt), BF16),
            pltpu.VMEM((d, tt), F32),
        ],
        compiler_params=_cparams(("arbitrary", "arbitrary")),
        name="peer_experts",
    )(x, gate, hb, u_b, u_b, vt_b, vt_b, n1, f, r2, e2)


def _final_norm_kernel(x_ref, g_ref, o_ref):
    x = x_ref[...]
    ms = jnp.mean(x * x, axis=-1, keepdims=True)
    o_ref[...] = x * lax.rsqrt(ms + NORM_EPS) * g_ref[...]


def _final_norm_call(x, g, *, tm=512):
    t, d = x.shape
    return pl.pallas_call(
        _final_norm_kernel,
        grid=(t // tm,),
        in_specs=[pl.BlockSpec((tm, d), lambda i: (i, 0)), pl.BlockSpec((1, d), lambda i: (0, 0))],
        out_specs=pl.BlockSpec((tm, d), lambda i: (i, 0)),
        out_shape=jax.ShapeDtypeStruct((t, d), F32),
        compiler_params=_cparams(("arbitrary",)),
        name="final_norm",
    )(x, g.reshape(1, d))


def _pad_rows(a, rows):
    return jnp.pad(a, ((0, rows - a.shape[0]), (0, 0)))


def _pad_cols(a, cols):
    return jnp.pad(a, ((0, 0), (0, cols - a.shape[1])))


def _hybrid_mixer(x, g, shift, scale, batch, seq, w_in, conv_w, conv_b, conv_ln_g, conv_ln_b, mu, w0, w2,
                  a0, a2, g2, k_k, k_a, r_k, ln_g, ln_b):
    rw = RWKV_WIDTH
    c0 = 2 * CONV_WIDTH
    lora0 = c0 + 3 * rw
    l1, l2 = lora0 + DECAY_LORA, lora0 + DECAY_LORA + AAA_LORA
    w_conv = w_in[:, :c0].astype(BF16)
    w_rkv = w_in[:, c0:lora0].astype(BF16)
    w_lora = jnp.concatenate([
        _pad_cols(w_in[:, lora0:l1], LANES), _pad_cols(w_in[:, l1:l2], LANES),
        _pad_cols(w_in[:, l2:], 2 * LANES)], axis=1).astype(BF16)
    zc, zr, zl = _norm_matmul_call(x, g, shift, scale, [w_conv, w_rkv, w_lora], [F32, F32, F32], seq,
                                   name="hybrid_in_proj")
    ya = _conv_call(zc, conv_w, conv_b, conv_ln_g, conv_ln_b, batch, seq)
    mu_l = jnp.concatenate([
        _pad_cols(mu[None, lora0 - c0:l1 - c0], LANES), _pad_cols(mu[None, l1 - c0:l2 - c0], LANES),
        _pad_cols(mu[None, l2 - c0:], 2 * LANES)], axis=1)
    vec = lambda a: a.reshape(1, rw)
    head_id = jnp.arange(RWKV_GROUP * RWKV_HEAD) // RWKV_HEAD
    params = dict(
        mu_r=mu[None, :3 * rw], mu_l=mu_l, w0=vec(w0), a0=vec(a0), k_k=vec(k_k), k_a=vec(k_a),
        r_k=vec(r_k), ln_g=vec(ln_g), ln_b=vec(ln_b),
        w2=_pad_rows(w2, LANES).astype(BF16), a2=_pad_rows(a2, LANES).astype(BF16),
        g2=_pad_rows(g2, 2 * LANES).astype(BF16),
        seg=(head_id[:, None] == head_id[None, :]).astype(BF16))
    yb = _rwkv_call(zr, zl, params, batch, seq)
    return ya, yb


def kernel(x, c, positions, ada_w, ada_b, norm_mix_g, norm_ffn_g, hyb_w_in, conv_w, conv_b, conv_ln_g, conv_ln_b, rwkv_mu, rwkv_w0, rwkv_w2, rwkv_a0, rwkv_a2, rwkv_g2, rwkv_k_k, rwkv_k_a, rwkv_r_k, rwkv_ln_g, rwkv_ln_b, hyb_w_out, diff_w_qkv, diff_lq1, diff_lk1, diff_lq2, diff_lk2, diff_subln_g, diff_w_out, peer_w_q, peer_subkeys, peer_u, peer_v, final_g):
    batch, seq, d = x.shape
    depth = ada_w.shape[0]
    t = batch * seq
    xt = x.reshape(t, d)
    pos = positions.reshape(t, 1)
    mod = _ada_call(c, ada_w, ada_b).reshape(depth, batch, 6, 1, d)

    for l in range(depth):
        sh1, sc1, g1, sh2, sc2, g2 = (mod[l, :, j] for j in range(6))
        gmix = norm_mix_g[l].reshape(1, d)
        if l % 2 == 0:
            e = l // 2
            ya, yb = _hybrid_mixer(
                xt, gmix, sh1, sc1, batch, seq, hyb_w_in[e], conv_w[e], conv_b[e], conv_ln_g[e],
                conv_ln_b[e], rwkv_mu[e], rwkv_w0[e], rwkv_w2[e], rwkv_a0[e], rwkv_a2[e], rwkv_g2[e],
                rwkv_k_k[e], rwkv_k_a[e], rwkv_r_k[e], rwkv_ln_g[e], rwkv_ln_b[e])
            w_out = hyb_w_out[e].astype(BF16)
            xt = _out_proj_call(xt, g1, [ya.astype(BF16), yb.astype(BF16)],
                                [w_out[:CONV_WIDTH], w_out[CONV_WIDTH:]], seq)
        else:
            o = l // 2
            lambda_init = 0.8 - 0.6 * math.exp(-0.3 * l)
            wqkv = diff_w_qkv[o].astype(BF16)
            n = wqkv.shape[1] // 3
            q, k, v = _qkv_rope_call(xt, gmix, sh1, sc1, pos, wqkv[:, :n], wqkv[:, n:2 * n], wqkv[:, 2 * n:],
                                     seq)
            y = _attn_call(q, k, v, diff_lq1[o], diff_lk1[o], diff_lq2[o], diff_lk2[o], diff_subln_g[o],
                           lambda_init, batch, seq)
            xt = _out_proj_call(xt, g1, [y], [diff_w_out[o].astype(BF16)], seq)

        q, hb = _norm_matmul_call(xt, norm_ffn_g[l].reshape(1, d), sh2, sc2, [peer_w_q[l]], [F32], seq,
                                  split=True, emit_h=True, name="peer_query")
        r2, e2, n1, f = _peer_route_call(q, peer_subkeys[l], tt=PEER_TOKEN_TILE)
        v_tiles = peer_v[l].astype(BF16).reshape(-1, PEER_EXPERT_TILE, d).transpose(0, 2, 1)
        xt = _peer_expert_call(xt, g2, hb, peer_u[l].astype(BF16), v_tiles,
                               n1, f, r2, e2, seq, tt=PEER_TOKEN_TILE, te=PEER_EXPERT_TILE)

    return _final_norm_call(xt, final_g).reshape(batch, seq, d)
```

```python
import functools
import math

import jax
import jax.numpy as jnp
from jax import lax
from jax.experimental import pallas as pl
from jax.experimental.pallas import tpu as pltpu

F32 = jnp.float32
BF16 = jnp.bfloat16
HIGHEST = lax.Precision.HIGHEST

CONV_WIDTH = 512
CONV_KERNEL = 31
RWKV_WIDTH = 512
RWKV_HEAD = 64
RWKV_HEADS = 8
DECAY_LORA = 64
AAA_LORA = 64
GATE_LORA = 160
DIFF_HEADS = 8
DIFF_HEAD_DIM = 64
DIFF_V_DIM = 128
ROPE_DIM = 16
ROPE_THETA = 500000.0
ATTN_CHUNK = 64
PEER_HEADS = 8
PEER_KEYS = 128
PEER_HALF = 128
PEER_TOPK = 16
NORM_EPS = 1e-6
LN_EPS = 1e-5
RWKV_GN_EPS = 64e-5

LANES = 128
SCAN_CHUNK = 64
CONV_HALO = 32
PEER_EXPERT_TILE = 2048
PEER_ROUTE_TILES = 2
PEER_TOKEN_TILE = 256
VMEM_LIMIT = 48 * 1024 * 1024


def _cparams(semantics, flags=None):
    return pltpu.CompilerParams(dimension_semantics=semantics, vmem_limit_bytes=VMEM_LIMIT, flags=flags)


def _nt_dot(a, b, precision=None):
    return lax.dot_general(a, b, (((1,), (1,)), ((), ())), precision=precision,
                           preferred_element_type=F32)


def _dot(a, b, precision=None):
    return jnp.dot(a, b, precision=precision, preferred_element_type=F32)


def _sigmoid(x):
    return 1.0 / (1.0 + jnp.exp(-x))


def _ada_kernel(c_ref, w_ref, b_ref, o_ref):
    c = c_ref[...]
    cond = c * _sigmoid(c)
    o_ref[0] = _dot(cond, w_ref[0], HIGHEST) + b_ref[0]


def _ada_call(c, ada_w, ada_b):
    depth, d, n = ada_w.shape
    b = c.shape[0]
    tn = 1536
    return pl.pallas_call(
        _ada_kernel,
        grid=(depth, n // tn),
        in_specs=[
            pl.BlockSpec((b, d), lambda l, j: (0, 0)),
            pl.BlockSpec((1, d, tn), lambda l, j: (l, 0, j)),
            pl.BlockSpec((1, 1, tn), lambda l, j: (l, 0, j)),
        ],
        out_specs=pl.BlockSpec((1, b, tn), lambda l, j: (l, 0, j)),
        out_shape=jax.ShapeDtypeStruct((depth, b, n), F32),
        compiler_params=_cparams(("arbitrary", "arbitrary")),
        name="ada_mod",
    )(c, ada_w, ada_b.reshape(depth, 1, n))


def _modulate(x, g, shift, scale):
    ms = jnp.mean(x * x, axis=-1, keepdims=True)
    y = x * lax.rsqrt(ms + NORM_EPS)
    return (y * g) * (1.0 + scale) + shift


def _split_bf16(a):
    hi = a.astype(BF16)
    return hi, (a - hi.astype(F32)).astype(BF16)


def _norm_matmul_kernel(*refs, n_w, split, emit_h):
    x_ref, g_ref, sh_ref, sc_ref = refs[:4]
    w_refs = refs[4:4 + n_w]
    o_refs = refs[4 + n_w:]
    h = _modulate(x_ref[...], g_ref[...], sh_ref[0], sc_ref[0])
    hb = h.astype(BF16)
    if split:
        h_lo = (h - hb.astype(F32)).astype(BF16)
        for k in range(n_w // 2):
            w_hi, w_lo = w_refs[2 * k][...], w_refs[2 * k + 1][...]
            o_refs[k][...] = (_dot(hb, w_hi) + _dot(h_lo, w_hi) + _dot(hb, w_lo)).astype(o_refs[k].dtype)
    else:
        for w_ref, o_ref in zip(w_refs, o_refs[:n_w]):
            o_ref[...] = _dot(hb, w_ref[...]).astype(o_ref.dtype)
    if emit_h:
        o_refs[-1][...] = h.T.astype(BF16)


def _norm_matmul_call(x, g, shift, scale, weights, out_dtypes, seq, *, split=False, emit_h=False,
                      tm=512, name="norm_matmul"):
    t, d = x.shape
    if split:
        weights = [part for w in weights for part in _split_bf16(w)]
    in_specs = [
        pl.BlockSpec((tm, d), lambda i: (i, 0)),
        pl.BlockSpec((1, d), lambda i: (0, 0)),
        pl.BlockSpec((1, 1, d), lambda i: ((i * tm) // seq, 0, 0)),
        pl.BlockSpec((1, 1, d), lambda i: ((i * tm) // seq, 0, 0)),
    ]
    out_specs, out_shape = [], []
    for w in weights:
        in_specs.append(pl.BlockSpec(w.shape, lambda i: (0, 0)))
    for w, dt in zip(weights[::2] if split else weights, out_dtypes):
        n = w.shape[1]
        out_specs.append(pl.BlockSpec((tm, n), lambda i: (i, 0)))
        out_shape.append(jax.ShapeDtypeStruct((t, n), dt))
    if emit_h:
        out_specs.append(pl.BlockSpec((d, tm), lambda i: (0, i)))
        out_shape.append(jax.ShapeDtypeStruct((d, t), BF16))
    return pl.pallas_call(
        functools.partial(_norm_matmul_kernel, n_w=len(weights), split=split, emit_h=emit_h),
        grid=(t // tm,),
        in_specs=in_specs,
        out_specs=out_specs,
        out_shape=out_shape,
        compiler_params=_cparams(("arbitrary",)),
        name=name,
    )(x, g, shift, scale, *weights)


def _out_proj_kernel(*refs, n_y):
    x_ref, gate_ref = refs[:2]
    y_refs = refs[2:2 + n_y]
    w_refs = refs[2 + n_y:2 + 2 * n_y]
    o_ref = refs[2 + 2 * n_y]
    acc = _dot(y_refs[0][...], w_refs[0][...])
    for y_ref, w_ref in zip(y_refs[1:], w_refs[1:]):
        acc = acc + _dot(y_ref[...], w_ref[...])
    o_ref[...] = x_ref[...] + gate_ref[0] * acc


def _out_proj_call(x, gate, ys, ws, seq, *, tm=512):
    t, d = x.shape
    in_specs = [
        pl.BlockSpec((tm, d), lambda i: (i, 0)),
        pl.BlockSpec((1, 1, d), lambda i: ((i * tm) // seq, 0, 0)),
    ]
    for y in ys:
        in_specs.append(pl.BlockSpec((tm, y.shape[1]), lambda i: (i, 0)))
    for w in ws:
        in_specs.append(pl.BlockSpec(w.shape, lambda i: (0, 0)))
    return pl.pallas_call(
        functools.partial(_out_proj_kernel, n_y=len(ys)),
        grid=(t // tm,),
        in_specs=in_specs,
        out_specs=pl.BlockSpec((tm, d), lambda i: (i, 0)),
        out_shape=jax.ShapeDtypeStruct((t, d), F32),
        compiler_params=_cparams(("arbitrary",)),
        name="out_proj",
    )(x, gate, *ys, *ws)


def _conv_kernel(z_ref, w_ref, b_ref, g_ref, beta_ref, o_ref, ext_ref, *, ts):
    width = CONV_WIDTH

    @pl.when(pl.program_id(1) == 0)
    def _():
        ext_ref[0:CONV_HALO, :] = jnp.zeros((CONV_HALO, width), F32)

    z = z_ref[...]
    u = z[:, :width] * _sigmoid(z[:, width:])
    ext_ref[CONV_HALO:CONV_HALO + ts, :] = u
    base = CONV_HALO - (CONV_KERNEL - 1)
    rows = 64
    for cb in range(width // LANES):
        cs = slice(cb * LANES, (cb + 1) * LANES)
        for rb in range(ts // rows):
            acc = jnp.zeros((rows, LANES), F32)
            for j in range(CONV_KERNEL):
                start = rb * rows + base + j
                acc = acc + w_ref[j:j + 1, cs] * ext_ref[start:start + rows, cs]
            o_ref[rb * rows:(rb + 1) * rows, cs] = acc
    conv = o_ref[...] + b_ref[...]
    mu = jnp.mean(conv, axis=-1, keepdims=True)
    dlt = conv - mu
    var = jnp.mean(dlt * dlt, axis=-1, keepdims=True)
    y = dlt * lax.rsqrt(var + LN_EPS) * g_ref[...] + beta_ref[...]
    o_ref[...] = y * _sigmoid(y)
    ext_ref[0:CONV_HALO, :] = ext_ref[ts:ts + CONV_HALO, :]


def _conv_call(z, conv_w, conv_b, ln_g, ln_b, batch, seq, *, ts=256):
    t = z.shape[0]
    width = CONV_WIDTH
    nts = seq // ts
    vec = lambda a: a.reshape(1, width)
    return pl.pallas_call(
        functools.partial(_conv_kernel, ts=ts),
        grid=(batch, nts),
        in_specs=[
            pl.BlockSpec((ts, 2 * width), lambda b, i: (b * nts + i, 0)),
            pl.BlockSpec((CONV_KERNEL, width), lambda b, i: (0, 0)),
            pl.BlockSpec((1, width), lambda b, i: (0, 0)),
            pl.BlockSpec((1, width), lambda b, i: (0, 0)),
            pl.BlockSpec((1, width), lambda b, i: (0, 0)),
        ],
        out_specs=pl.BlockSpec((ts, width), lambda b, i: (b * nts + i, 0)),
        out_shape=jax.ShapeDtypeStruct((t, width), F32),
        scratch_shapes=[pltpu.VMEM((ts + CONV_HALO, width), F32)],
        compiler_params=_cparams(("arbitrary", "arbitrary")),
        name="conformer_conv",
    )(z, conv_w, vec(conv_b), vec(ln_g), vec(ln_b))


def _softplus(x):
    return jnp.maximum(x, 0.0) + jnp.log(1.0 + jnp.exp(-jnp.abs(x)))


def _unit_lower_inverses(mats, row, col, size):
    eye = (row == col).astype(F32)
    blk = lambda m: (row // m) == (col // m)
    bdot = lambda p, q: _dot(p.astype(BF16), q.astype(BF16))
    n1 = [jnp.where(blk(8), a, 0.0) for a in mats]
    t = [eye + x for x in n1]
    n2 = [bdot(x, x) for x in n1]
    t = [x + bdot(x, y) for x, y in zip(t, n2)]
    n4 = [bdot(x, x) for x in n2]
    t = [x + bdot(x, y) for x, y in zip(t, n4)]
    m = 8
    while m < size:
        new = blk(2 * m) & jnp.logical_not(blk(m))
        et = [bdot(jnp.where(new, a, 0.0), x) for a, x in zip(mats, t)]
        t = [x + bdot(x, y) for x, y in zip(t, et)]
        m *= 2
    return t


def _split_dot(x, w_bf16, parts):
    acc = None
    for _ in range(parts):
        piece = x.astype(BF16)
        term = _dot(piece, w_bf16)
        acc = term if acc is None else acc + term
        x = x - piece.astype(F32)
    return acc


RWKV_GROUP = 4
RWKV_BATCH_TILE = 2


def _rwkv_kernel(zr_ref, zl_ref, mur_ref, mul_ref, w0_ref, a0_ref, kk_ref, ka_ref, rk_ref,
                 lng_ref, lnb_ref, w2_ref, a2_ref, g2_ref, seg_ref, o_ref,
                 extr_ref, extl_ref, state_ref):
    c = SCAN_CHUNK
    rw = RWKV_WIDTH
    n = RWKV_HEAD
    nb = RWKV_BATCH_TILE
    gw = RWKV_GROUP * n
    groups = rw // gw

    @pl.when(pl.program_id(1) == 0)
    def _():
        extr_ref[:, 0:8, :] = jnp.zeros((nb, 8, 3 * rw), F32)
        extl_ref[:, 0:8, :] = jnp.zeros((nb, 8, rw), F32)
        state_ref[...] = jnp.zeros_like(state_ref)

    zr_rows, zl_rows = [], []
    for b in range(nb):
        zr = zr_ref[b]
        zl = zl_ref[b]
        extr_ref[b, 8:8 + c, :] = zr
        extl_ref[b, 8:8 + c, :] = zl
        zr_rows.append(zr + (extr_ref[b, 7:7 + c, :] - zr) * mur_ref[...])
        zl_rows.append(zl + (extl_ref[b, 7:7 + c, :] - zl) * mul_ref[...])
        extr_ref[b, 0:8, :] = extr_ref[b, c:c + 8, :]
        extl_ref[b, 0:8, :] = extl_ref[b, c:c + 8, :]
    zr = jnp.concatenate(zr_rows, axis=0)
    zl = jnp.concatenate(zl_rows, axis=0)

    r = zr[:, 0:rw]
    k = zr[:, rw:2 * rw]
    v = zr[:, 2 * rw:3 * rw]
    wd = zl[:, 0:LANES]
    ad = zl[:, LANES:2 * LANES]
    gd = zl[:, 2 * LANES:4 * LANES]
    seg = seg_ref[...]

    def segsum(x, parts):
        return jnp.concatenate([_split_dot(x[:, g * gw:(g + 1) * gw], seg, parts) for g in range(groups)],
                               axis=1)

    w_raw = -_softplus(-(w0_ref[...] + _dot(jnp.tanh(wd).astype(BF16), w2_ref[...]))) - 0.5
    lw = -jnp.exp(w_raw)
    alpha = _sigmoid(a0_ref[...] + _dot(ad.astype(BF16), a2_ref[...]))
    gate = _dot(_sigmoid(gd).astype(BF16), g2_ref[...])
    kkf = k * kk_ref[...]
    kk = kkf / jnp.maximum(jnp.sqrt(segsum(kkf * kkf, 3)), 1e-12)
    kp = k * (1.0 + (alpha - 1.0) * ka_ref[...])
    bonus = segsum(r * kp * rk_ref[...], 3) * v
    av = -kk
    bv = kk * alpha

    trow = lax.broadcasted_iota(jnp.int32, (nb * c, nb * c), 0)
    tcol = lax.broadcasted_iota(jnp.int32, (nb * c, nb * c), 1)
    same_seq_incl = ((trow // c) == (tcol // c)) & (tcol <= trow)
    cum = _split_dot_lhs_exact(jnp.where(same_seq_incl, 1.0, 0.0).astype(BF16), lw)
    tot = jnp.concatenate([jnp.broadcast_to(cum[(b + 1) * c - 1:(b + 1) * c, :], (c, rw)) for b in range(nb)],
                          axis=0)
    g_in = jnp.exp(cum)
    g_inp = jnp.exp(cum - lw)
    g_out = jnp.exp(-cum)
    g_end = jnp.exp(tot - cum)
    g_tot = jnp.exp(tot)
    a_in = av * g_inp
    r_in = r * g_in
    b_out = bv * g_out
    k_out = kp * g_out
    b_end = bv * g_end
    k_end = kp * g_end

    row = lax.broadcasted_iota(jnp.int32, (gw, gw), 0)
    col = lax.broadcasted_iota(jnp.int32, (gw, gw), 1)
    same_head = (row // n) == (col // n)
    strict = same_head & ((col % n) < (row % n))
    incl = same_head & ((col % n) <= (row % n))

    def expand(x):
        return jnp.where(same_head, jnp.concatenate([x] * RWKV_GROUP, axis=0), 0.0)

    bf = lambda x: x.astype(BF16)
    chains = [(b, g) for b in range(nb) for g in range(groups)]
    pick = lambda x: [expand(x[b * c:(b + 1) * c, g * gw:(g + 1) * gw]) for b, g in chains]
    each = lambda fn, *lists: [fn(*args) for args in zip(*lists)]
    a_bd, r_bd = each(bf, pick(a_in)), each(bf, pick(r_in))
    bo_bd, ko_bd = each(bf, pick(b_out)), each(bf, pick(k_out))
    v_f = pick(v)
    v_bd = each(bf, v_f)
    st = [state_ref[b, g] for b, g in chains]
    st_b = each(bf, st)
    a_ab = each(lambda p, q: jnp.where(strict, _nt_dot(p, q), 0.0), a_bd, bo_bd)
    a_ak = each(lambda p, q: bf(jnp.where(strict, _nt_dot(p, q), 0.0)), a_bd, ko_bd)
    a_rb = each(lambda p, q: bf(jnp.where(incl, _nt_dot(p, q), 0.0)), r_bd, bo_bd)
    a_rk = each(lambda p, q: bf(jnp.where(incl, _nt_dot(p, q), 0.0)), r_bd, ko_bd)
    ph_a = each(_nt_dot, a_bd, st_b)
    ph_r = each(_nt_dot, r_bd, st_b)
    tinv = _unit_lower_inverses(a_ab, row, col, n)
    rhs = each(lambda p, m, w: bf(p + _dot(m, w)), ph_a, a_ak, v_bd)
    u = each(lambda t, x: _dot(bf(t), x), tinv, rhs)
    y_bd = each(lambda p, m1, uu, m2, w: p + _dot(m1, bf(uu)) + _dot(m2, w), ph_r, a_rb, u, a_rk, v_bd)
    be_bd, ke_bd = each(bf, pick(b_end)), each(bf, pick(k_end))
    for (b, g), s0, uu, vf, be, ke in zip(chains, st, u, v_f, be_bd, ke_bd):
        state_ref[b, g] = (s0 * g_tot[b * c:b * c + 1, g * gw:(g + 1) * gw]
                           + _dot(bf(uu.T), be) + _dot(bf(vf.T), ke))
    y_g = [sum(m[i * c:(i + 1) * c] for i in range(RWKV_GROUP)) for m in y_bd]
    y = jnp.concatenate([jnp.concatenate(y_g[b * groups:(b + 1) * groups], axis=1) for b in range(nb)],
                        axis=0)

    inv_n = 1.0 / n
    mu_y = segsum(y, 3) * inv_n
    dy = y - mu_y
    var_y = segsum(dy * dy, 3) * inv_n
    yn = dy * lax.rsqrt(var_y + RWKV_GN_EPS)
    out = (yn * lng_ref[...] + lnb_ref[...] + bonus) * gate
    for b in range(nb):
        o_ref[b] = out[b * c:(b + 1) * c]


def _split_dot_lhs_exact(w_bf16, x):
    acc = None
    for _ in range(3):
        piece = x.astype(BF16)
        term = _dot(w_bf16, piece)
        acc = term if acc is None else acc + term
        x = x - piece.astype(F32)
    return acc


def _rwkv_call(zr, zl, p, batch, seq):
    c = SCAN_CHUNK
    rw = RWKV_WIDTH
    nb = RWKV_BATCH_TILE
    nc = seq // c
    gw = RWKV_GROUP * RWKV_HEAD
    const = lambda shape: pl.BlockSpec(shape, lambda b, i: (0,) * len(shape))
    vec = const((1, rw))
    out = pl.pallas_call(
        _rwkv_kernel,
        grid=(batch // nb, nc),
        in_specs=[
            pl.BlockSpec((nb, c, 3 * rw), lambda b, i: (b, i, 0)),
            pl.BlockSpec((nb, c, rw), lambda b, i: (b, i, 0)),
            const((1, 3 * rw)), vec,
            vec, vec, vec, vec, vec, vec, vec,
            const((LANES, rw)), const((LANES, rw)), const((2 * LANES, rw)),
            const((gw, gw)),
        ],
        out_specs=pl.BlockSpec((nb, c, rw), lambda b, i: (b, i, 0)),
        out_shape=jax.ShapeDtypeStruct((batch, seq, rw), F32),
        scratch_shapes=[
            pltpu.VMEM((nb, c + 8, 3 * rw), F32),
            pltpu.VMEM((nb, c + 8, rw), F32),
            pltpu.VMEM((nb, rw // gw, gw, gw), F32),
        ],
        compiler_params=_cparams(("arbitrary", "arbitrary")),
        name="rwkv7_mix",
    )(zr.reshape(batch, seq, 3 * rw), zl.reshape(batch, seq, rw), p["mu_r"], p["mu_l"], p["w0"], p["a0"],
      p["k_k"], p["k_a"], p["r_k"], p["ln_g"], p["ln_b"], p["w2"], p["a2"], p["g2"], p["seg"])
    return out.reshape(batch * seq, rw)


def _qkv_rope_kernel(x_ref, g_ref, sh_ref, sc_ref, pos_ref, inv_ref, sgn_ref, wq_ref, wk_ref, wv_ref,
                     q_ref, k_ref, v_ref):
    h = _modulate(x_ref[...], g_ref[...], sh_ref[0], sc_ref[0]).astype(BF16)
    ang = pos_ref[...].astype(F32) * inv_ref[...]
    cosf = jnp.cos(ang)
    sinf = jnp.sin(ang) * sgn_ref[...]
    lane = lax.broadcasted_iota(jnp.int32, ang.shape, 1)
    low = (lane % DIFF_HEAD_DIM) < (ROPE_DIM // 2)
    half = ROPE_DIM // 2

    def rope(w_ref, o_ref, scale):
        z = _dot(h, w_ref[...])
        for cb in range(z.shape[1] // LANES):
            zc = z[:, cb * LANES:(cb + 1) * LANES]
            partner = jnp.where(low, pltpu.roll(zc, LANES - half, axis=1), pltpu.roll(zc, half, axis=1))
            o_ref[:, cb * LANES:(cb + 1) * LANES] = ((zc * cosf + partner * sinf) * scale).astype(o_ref.dtype)

    rope(wq_ref, q_ref, DIFF_HEAD_DIM ** -0.5)
    rope(wk_ref, k_ref, 1.0)
    v_ref[...] = _dot(h, wv_ref[...]).astype(v_ref.dtype)


def _qkv_rope_call(x, g, shift, scale, pos, wq, wk, wv, seq, *, tm=512):
    t, d = x.shape
    half = ROPE_DIM // 2
    inv = 1.0 / (ROPE_THETA ** (jnp.arange(0, ROPE_DIM, 2, dtype=F32) / ROPE_DIM))
    dl = jnp.arange(LANES) % DIFF_HEAD_DIM
    inv_pat = jnp.where(dl < ROPE_DIM, inv[dl % half], 0.0).reshape(1, LANES).astype(F32)
    sgn_pat = jnp.where(dl < half, -1.0, 1.0).reshape(1, LANES).astype(F32)
    row = lambda n: pl.BlockSpec((tm, n), lambda i: (i, 0))
    full = lambda a: pl.BlockSpec(a.shape, lambda i: (0, 0))
    mod = pl.BlockSpec((1, 1, d), lambda i: ((i * tm) // seq, 0, 0))
    n = wq.shape[1]
    return pl.pallas_call(
        _qkv_rope_kernel,
        grid=(t // tm,),
        in_specs=[row(d), full(g), mod, mod, row(1), full(inv_pat), full(sgn_pat),
                  full(wq), full(wk), full(wv)],
        out_specs=[row(n), row(n), row(n)],
        out_shape=[jax.ShapeDtypeStruct((t, n), BF16)] * 3,
        compiler_params=_cparams(("arbitrary",)),
        name="qkv_rope",
    )(x, g, shift, scale, pos, inv_pat, sgn_pat, wq, wk, wv)


def _attn_kernel(q_ref, k_ref, v_ref, lq1_ref, lk1_ref, lq2_ref, lk2_ref, sg_ref, o_ref,
                 *, tq, seq, lambda_init):
    i = pl.program_id(2)
    kb = 2 * tq
    q = q_ref[...]
    lane = lax.broadcasted_iota(jnp.int32, q.shape, 1)
    zero = jnp.zeros_like(q)
    qm = (jnp.where(lane < DIFF_HEAD_DIM, q, zero), jnp.where(lane >= DIFF_HEAD_DIM, q, zero))
    f32sum = lambda a, b: jnp.sum(a[...] * b[...], axis=-1, keepdims=True)
    lam = jnp.exp(f32sum(lq1_ref, lk1_ref)) - jnp.exp(f32sum(lq2_ref, lk2_ref)) + lambda_init
    row_chunk = (i * tq + lax.broadcasted_iota(jnp.int32, (tq, kb), 0)) // ATTN_CHUNK
    col_in_block = lax.broadcasted_iota(jnp.int32, (tq, kb), 1)

    def attend(nblk):
        outs = []
        for mp in range(2):
            s = [_nt_dot(qm[mp], k_ref[j * kb:(j + 1) * kb, :]) for j in range(nblk)]
            col_chunk = ((nblk - 1) * kb + col_in_block) // ATTN_CHUNK
            s[-1] = jnp.where(col_chunk <= row_chunk, s[-1], -jnp.inf)
            m = jnp.max(s[0], axis=-1, keepdims=True)
            for sj in s[1:]:
                m = jnp.maximum(m, jnp.max(sj, axis=-1, keepdims=True))
            l = jnp.zeros_like(m)
            acc = jnp.zeros((tq, DIFF_V_DIM), F32)
            for j, sj in enumerate(s):
                p = jnp.exp(sj - m)
                l = l + jnp.sum(p, axis=-1, keepdims=True)
                acc = acc + _dot(p.astype(BF16), v_ref[j * kb:(j + 1) * kb, :])
            outs.append(acc / l)
        o = outs[0] - lam * outs[1]
        ms = jnp.mean(o * o, axis=-1, keepdims=True)
        o = o * lax.rsqrt(ms + NORM_EPS) * sg_ref[...] * (1.0 - lambda_init)
        o_ref[...] = o.astype(o_ref.dtype)

    for nblk in range(1, seq // kb + 1):
        pl.when(i // 2 == nblk - 1)(functools.partial(attend, nblk))


def _attn_call(q, k, v, lq1, lk1, lq2, lk2, subln_g, lambda_init, batch, seq, *, tq=256):
    t = q.shape[0]
    nq = seq // tq
    assert seq % (2 * tq) == 0
    vec = lambda a: a.reshape(1, -1)
    small = lambda a: pl.BlockSpec(a.shape, lambda b, h, i: (0, 0))
    kv = pl.BlockSpec((seq, DIFF_V_DIM), lambda b, h, i: (b, h))
    args = [vec(lq1), vec(lk1), vec(lq2), vec(lk2), vec(subln_g)]
    return pl.pallas_call(
        functools.partial(_attn_kernel, tq=tq, seq=seq, lambda_init=lambda_init),
        grid=(batch, DIFF_HEADS, nq),
        in_specs=[pl.BlockSpec((tq, DIFF_V_DIM), lambda b, h, i: (b * nq + i, h)), kv, kv]
                 + [small(a) for a in args],
        out_specs=pl.BlockSpec((tq, DIFF_V_DIM), lambda b, h, i: (b * nq + i, h)),
        out_shape=jax.ShapeDtypeStruct((t, DIFF_HEADS * DIFF_V_DIM), BF16),
        compiler_params=_cparams(("arbitrary", "arbitrary", "arbitrary")),
        name="diff_attn",
    )(q, k, v, *args)


def _peer_route_kernel(q_ref, sk_ref, r2_ref, e2_ref, n1_ref, f_ref, a_ref, b_ref, cand_ref, *, tt, tiles):
    neg = -jnp.inf
    k = PEER_TOPK
    kf = float(k)
    key_id = lax.broadcasted_iota(jnp.int32, (PEER_KEYS, LANES), 0).astype(F32)
    cand_id = lax.broadcasted_iota(jnp.int32, (k * k, LANES), 0).astype(F32)

    def extract(w, ids, dst_ref, by_index, want_rank):
        rank = jnp.full(w.shape, kf, F32) if want_rank else None
        tops = []
        for r in range(k):
            mx = jnp.max(w, axis=0, keepdims=True)
            tops.append(mx)
            if dst_ref is not None:
                dst_ref[r:r + 1, :] = mx
            hit = w == mx
            if by_index:
                first = jnp.min(jnp.where(hit, ids, float(w.shape[0])), axis=0, keepdims=True)
                hit = ids == first
            if want_rank:
                rank = jnp.where(hit, float(r), rank)
            w = jnp.where(hit, neg, w)
        return rank, tops, w

    def candidates(ga_ref, gb_ref):
        bvals = gb_ref[...]
        for x in range(k):
            cand_ref[x * k:(x + 1) * k, :] = ga_ref[x:x + 1, :] + bvals
        return cand_ref[...]

    def staircase_candidates(ga_ref, gb_ref):
        a = lambda lo, hi: ga_ref[lo:hi, :]
        b = lambda lo, hi: gb_ref[lo:hi, :]
        return jnp.concatenate([
            a(0, 1) + b(0, 8), a(0, 1) + b(8, 16), a(1, 2) + b(0, 8), a(2, 3) + b(0, 8), a(3, 4) + b(0, 8),
            a(0, 8) + b(0, 1), a(8, 16) + b(0, 1), a(0, 8) + b(1, 2), a(0, 8) + b(2, 3)], axis=0)

    def twin_bf16(v):
        bits = lax.bitcast_convert_type(v.astype(BF16).astype(F32), jnp.uint32)
        return bits | (bits >> 16)

    def emit(dst, s1, s2, rank2, n1, a0, b0, c_tops):
        tile, cols = dst
        zsum = jnp.zeros((1, LANES), F32)
        for c in c_tops:
            zsum = zsum + jnp.exp(c - c_tops[0])
        r2_ref[0, tile, :, cols] = pltpu.bitcast(rank2.astype(BF16), jnp.uint32)
        e2_ref[0, tile, :, cols] = pltpu.bitcast(jnp.exp(s2 - b0).astype(BF16), jnp.uint32)
        n1_ref[0, tile, :, cols] = twin_bf16(n1)
        f_ref[0, tile, :, cols] = twin_bf16(jnp.exp(s1 - a0) / zsum)

    count = lambda m: jnp.sum(jnp.where(m, 1.0, 0.0), axis=0, keepdims=True)

    def scores(rows):
        return (_nt_dot(sk_ref[0], q_ref[rows, 0:PEER_HALF], HIGHEST),
                _nt_dot(sk_ref[1], q_ref[rows, PEER_HALF:2 * PEER_HALF], HIGHEST))

    per_tile = tt // LANES
    groups = [(slice(g * LANES, (g + 1) * LANES),
               (g // per_tile, slice((g % per_tile) * LANES, (g % per_tile + 1) * LANES)))
              for g in range(tiles * per_tile)]
    any_tied = []
    for gidx, (rows, dst) in enumerate(groups):
        ga_ref, gb_ref = a_ref.at[gidx], b_ref.at[gidx]
        s1, s2 = scores(rows)
        _, a_tops, w1 = extract(s1, key_id, ga_ref, False, False)
        rank2, b_tops, _ = extract(s2, key_id, gb_ref, False, True)
        _, c_tops, _ = extract(staircase_candidates(ga_ref, gb_ref), None, None, False, False)
        tau = c_tops[k - 1]
        n1 = jnp.zeros_like(s1)
        for y in range(k):
            n1 = n1 + jnp.where((s1 + b_tops[y]) >= tau, 1.0, 0.0)
        emit(dst, s1, s2, rank2, n1, a_tops[0], b_tops[0], c_tops)
        tied = ((count(w1 == neg) != kf) | (count(rank2 < kf) != kf)
                | (jnp.sum(n1, axis=0, keepdims=True) != kf))
        any_tied.append(jnp.max(jnp.where(tied, 1.0, 0.0)) > 0.0)

    for gidx, (rows, dst) in enumerate(groups):
        @pl.when(any_tied[gidx])
        def _():
            ga_ref, gb_ref = a_ref.at[gidx], b_ref.at[gidx]
            s1, s2 = scores(rows)
            rank1, a_x, _ = extract(s1, key_id, ga_ref, True, True)
            rank2x, b_x, _ = extract(s2, key_id, gb_ref, True, True)
            rank3, c_x, _ = extract(candidates(ga_ref, gb_ref), cand_id, None, True, True)
            win = rank3 < kf
            n1x = jnp.zeros_like(s1)
            for x in range(k):
                n_x = count(win[x * k:(x + 1) * k])
                n1x = n1x + jnp.where(rank1 == float(x), n_x, 0.0)
            emit(dst, s1, s2, rank2x, n1x, a_x[0], b_x[0], c_x)


def _peer_route_call(q, subkeys, *, tt, tiles=PEER_ROUTE_TILES):
    t = q.shape[0]
    nt = t // tt
    spec = pl.BlockSpec((1, tiles, PEER_KEYS, tt), lambda i, h: (h, i, 0, 0))
    pair_spec = pl.BlockSpec((1, tiles, PEER_KEYS // 2, tt), lambda i, h: (h, i, 0, 0))
    tab = lambda rows: jax.ShapeDtypeStruct((PEER_HEADS, nt, rows, tt), jnp.uint32)
    return pl.pallas_call(
        functools.partial(_peer_route_kernel, tt=tt, tiles=tiles),
        grid=(nt // tiles, PEER_HEADS),
        in_specs=[
            pl.BlockSpec((tt * tiles, 2 * PEER_HALF), lambda i, h: (i, h)),
            pl.BlockSpec(subkeys.shape, lambda i, h: (0, 0, 0)),
        ],
        out_specs=[pair_spec, pair_spec, spec, spec],
        out_shape=[tab(PEER_KEYS // 2), tab(PEER_KEYS // 2), tab(PEER_KEYS), tab(PEER_KEYS)],
        scratch_shapes=[
            pltpu.VMEM((tiles * tt // LANES, PEER_TOPK, LANES), F32),
            pltpu.VMEM((tiles * tt // LANES, PEER_TOPK, LANES), F32),
            pltpu.VMEM((PEER_TOPK * PEER_TOPK, LANES), F32),
        ],
        compiler_params=_cparams(("arbitrary", "arbitrary")),
        name="peer_route",
    )(q, subkeys)


def _gelu(x):
    return 0.5 * x * (1.0 + lax.erf(x * (1.0 / math.sqrt(2.0))))


def _peer_expert_kernel(x_ref, gate_ref, ht_ref, u0_ref, u1_ref, vt0_ref, vt1_ref, n1_ref, f_ref, r2_ref,
                        e2_ref, o_ref, pre_a, pre_b, wa_ref, acc_ref, *, rows_per_tile):
    u_halves, vt_halves = (u0_ref, u1_ref), (vt0_ref, vt1_ref)
    s = pl.program_id(1)
    te, tt = pre_a.shape
    d = acc_ref.shape[0]
    blk = 16
    rows_per_piece = 2

    def fill_pre(pre_new):
        pre_rows = 512
        for m, u_ref in enumerate(u_halves):
            for r0 in range(0, te // 2, pre_rows):
                pre_new[m * te // 2 + r0:m * te // 2 + r0 + pre_rows, :] = _dot(u_ref[r0:r0 + pre_rows, :],
                                                                               ht_ref[...])

    def step(pre_new, pre_old):
        halves = [slice(m * d // 2, (m + 1) * d // 2) for m in range(2)]
        if pre_old is None:
            acc_ref[...] = jnp.zeros_like(acc_ref)
            fill_pre(pre_new)
            return
        zero = jnp.zeros((blk, LANES), BF16)
        as_bf16 = lambda words: pltpu.bitcast(words, BF16)
        for il in range(rows_per_tile):
            for lt in range(tt // LANES):
                lanes = slice(lt * LANES, (lt + 1) * LANES)
                row = lambda ref, h: as_bf16(jnp.broadcast_to(ref[h, 0, il:il + 1, lanes], (blk // 2, LANES)))
                n1b = [row(n1_ref, h) for h in range(PEER_HEADS)]
                fb = [row(f_ref, h) for h in range(PEER_HEADS)]
                for jb in range(PEER_KEYS // blk):
                    words = slice(jb * blk // 2, (jb + 1) * blk // 2)
                    w = zero
                    for h in range(PEER_HEADS):
                        hit = as_bf16(r2_ref[h, 0, words, lanes]) < n1b[h]
                        w = w + jnp.where(hit, as_bf16(e2_ref[h, 0, words, lanes]), zero) * fb[h]
                    rows = slice(il * PEER_KEYS + jb * blk, il * PEER_KEYS + (jb + 1) * blk)
                    wa_ref[rows, lanes] = w
        if pre_new is not None:
            fill_pre(pre_new)
        piece_rows = rows_per_piece * PEER_KEYS
        for p in range(te // piece_rows):
            piece = slice(p * piece_rows, (p + 1) * piece_rows)
            for r0 in range(p * piece_rows, (p + 1) * piece_rows, blk):
                rows = slice(r0, r0 + blk)
                wa_ref[rows, :] = wa_ref[rows, :] * _gelu(pre_old[rows, :]).astype(BF16)
            for rows, vt_ref in zip(halves, vt_halves):
                acc_ref[rows, :] += _dot(vt_ref[0, :, piece], wa_ref[piece, :])

    last = pl.num_programs(1) - 1
    inner = (s > 0) & (s < last)
    pl.when(s == 0)(lambda: step(pre_a, None))
    pl.when(inner & (s % 2 == 0))(lambda: step(pre_a, pre_b))
    pl.when(inner & (s % 2 == 1))(lambda: step(pre_b, pre_a))

    @pl.when(s == last)
    def _():
        step(None, pre_b)
        o_ref[...] = x_ref[...] + gate_ref[0] * acc_ref[...].T


def _peer_expert_call(x, gate, hb, u_b, vt_b, n1, f, r2, e2, seq, *, tt, te=1024):
    t, d = x.shape
    ne = u_b.shape[0] // te
    assert ne % 2 == 0, "the drain step reads the second pre-activation buffer"
    rows_per_tile = te // PEER_KEYS
    lag = lambda s, n: jnp.clip(s - n, 0, ne - 1)
    sel = pl.BlockSpec((PEER_HEADS, 1, rows_per_tile, tt), lambda i, s: (0, i, lag(s, 1), 0))
    full = pl.BlockSpec((PEER_HEADS, 1, PEER_KEYS // 2, tt), lambda i, s: (0, i, 0, 0))
    return pl.pallas_call(
        functools.partial(_peer_expert_kernel, rows_per_tile=rows_per_tile),
        grid=(t // tt, ne + 1),
        in_specs=[
            pl.BlockSpec((tt, d), lambda i, s: (i, 0)),
            pl.BlockSpec((1, 1, d), lambda i, s: ((i * tt) // seq, 0, 0)),
            pl.BlockSpec((d, tt), lambda i, s: (0, i)),
            pl.BlockSpec((te // 2, d), lambda i, s: (2 * lag(s, 0), 0)),
            pl.BlockSpec((te // 2, d), lambda i, s: (2 * lag(s, 0) + 1, 0)),
            pl.BlockSpec((1, d // 2, te), lambda i, s: (lag(s, 1), 0, 0)),
            pl.BlockSpec((1, d // 2, te), lambda i, s: (lag(s, 1), 1, 0)),
            sel, sel, full, full,
        ],
        out_specs=pl.BlockSpec((tt, d), lambda i, s: (i, 0)),
        out_shape=jax.ShapeDtypeStruct((t, d), F32),
        scratch_shapes=[
            pltpu.VMEM((te, tt), F32),
            pltpu.VMEM((te, tt), F32),
            pltpu.VMEM((te, tt), BF16),
            pltpu.VMEM((d, tt), F32),
        ],
        compiler_params=_cparams(("arbitrary", "arbitrary")),
        name="peer_experts",
    )(x, gate, hb, u_b, u_b, vt_b, vt_b, n1, f, r2, e2)


def _final_norm_kernel(x_ref, g_ref, o_ref):
    x = x_ref[...]
    ms = jnp.mean(x * x, axis=-1, keepdims=True)
    o_ref[...] = x * lax.rsqrt(ms + NORM_EPS) * g_ref[...]


def _final_norm_call(x, g, *, tm=512):
    t, d = x.shape
    return pl.pallas_call(
        _final_norm_kernel,
        grid=(t // tm,),
        in_specs=[pl.BlockSpec((tm, d), lambda i: (i, 0)), pl.BlockSpec((1, d), lambda i: (0, 0))],
        out_specs=pl.BlockSpec((tm, d), lambda i: (i, 0)),
        out_shape=jax.ShapeDtypeStruct((t, d), F32),
        compiler_params=_cparams(("arbitrary",)),
        name="final_norm",
    )(x, g.reshape(1, d))


def _pad_rows(a, rows):
    return jnp.pad(a, ((0, rows - a.shape[0]), (0, 0)))


def _pad_cols(a, cols):
    return jnp.pad(a, ((0, 0), (0, cols - a.shape[1])))


def _hybrid_mixer(x, g, shift, scale, batch, seq, w_in, conv_w, conv_b, conv_ln_g, conv_ln_b, mu, w0, w2,
                  a0, a2, g2, k_k, k_a, r_k, ln_g, ln_b):
    rw = RWKV_WIDTH
    c0 = 2 * CONV_WIDTH
    lora0 = c0 + 3 * rw
    l1, l2 = lora0 + DECAY_LORA, lora0 + DECAY_LORA + AAA_LORA
    w_conv = w_in[:, :c0].astype(BF16)
    w_rkv = w_in[:, c0:lora0].astype(BF16)
    w_lora = jnp.concatenate([
        _pad_cols(w_in[:, lora0:l1], LANES), _pad_cols(w_in[:, l1:l2], LANES),
        _pad_cols(w_in[:, l2:], 2 * LANES)], axis=1).astype(BF16)
    zc, zr, zl = _norm_matmul_call(x, g, shift, scale, [w_conv, w_rkv, w_lora], [F32, F32, F32], seq,
                                   name="hybrid_in_proj")
    ya = _conv_call(zc, conv_w, conv_b, conv_ln_g, conv_ln_b, batch, seq)
    mu_l = jnp.concatenate([
        _pad_cols(mu[None, lora0 - c0:l1 - c0], LANES), _pad_cols(mu[None, l1 - c0:l2 - c0], LANES),
        _pad_cols(mu[None, l2 - c0:], 2 * LANES)], axis=1)
    vec = lambda a: a.reshape(1, rw)
    head_id = jnp.arange(RWKV_GROUP * RWKV_HEAD) // RWKV_HEAD
    params = dict(
        mu_r=mu[None, :3 * rw], mu_l=mu_l, w0=vec(w0), a0=vec(a0), k_k=vec(k_k), k_a=vec(k_a),
        r_k=vec(r_k), ln_g=vec(ln_g), ln_b=vec(ln_b),
        w2=_pad_rows(w2, LANES).astype(BF16), a2=_pad_rows(a2, LANES).astype(BF16),
        g2=_pad_rows(g2, 2 * LANES).astype(BF16),
        seg=(head_id[:, None] == head_id[None, :]).astype(BF16))
    yb = _rwkv_call(zr, zl, params, batch, seq)
    return ya, yb


def kernel(x, c, positions, ada_w, ada_b, norm_mix_g, norm_ffn_g, hyb_w_in, conv_w, conv_b, conv_ln_g, conv_ln_b, rwkv_mu, rwkv_w0, rwkv_w2, rwkv_a0, rwkv_a2, rwkv_g2, rwkv_k_k, rwkv_k_a, rwkv_r_k, rwkv_ln_g, rwkv_ln_b, hyb_w_out, diff_w_qkv, diff_lq1, diff_lk1, diff_lq2, diff_lk2, diff_subln_g, diff_w_out, peer_w_q, peer_subkeys, peer_u, peer_v, final_g):
    batch, seq, d = x.shape
    depth = ada_w.shape[0]
    t = batch * seq
    xt = x.reshape(t, d)
    pos = positions.reshape(t, 1)
    mod = _ada_call(c, ada_w, ada_b).reshape(depth, batch, 6, 1, d)

    for l in range(depth):
        sh1, sc1, g1, sh2, sc2, g2 = (mod[l, :, j] for j in range(6))
        gmix = norm_mix_g[l].reshape(1, d)
        if l % 2 == 0:
            e = l // 2
            ya, yb = _hybrid_mixer(
                xt, gmix, sh1, sc1, batch, seq, hyb_w_in[e], conv_w[e], conv_b[e], conv_ln_g[e],
                conv_ln_b[e], rwkv_mu[e], rwkv_w0[e], rwkv_w2[e], rwkv_a0[e], rwkv_a2[e], rwkv_g2[e],
                rwkv_k_k[e], rwkv_k_a[e], rwkv_r_k[e], rwkv_ln_g[e], rwkv_ln_b[e])
            w_out = hyb_w_out[e].astype(BF16)
            xt = _out_proj_call(xt, g1, [ya.astype(BF16), yb.astype(BF16)],
                                [w_out[:CONV_WIDTH], w_out[CONV_WIDTH:]], seq)
        else:
            o = l // 2
            lambda_init = 0.8 - 0.6 * math.exp(-0.3 * l)
            wqkv = diff_w_qkv[o].astype(BF16)
            n = wqkv.shape[1] // 3
            q, k, v = _qkv_rope_call(xt, gmix, sh1, sc1, pos, wqkv[:, :n], wqkv[:, n:2 * n], wqkv[:, 2 * n:],
                                     seq)
            y = _attn_call(q, k, v, diff_lq1[o], diff_lk1[o], diff_lq2[o], diff_lk2[o], diff_subln_g[o],
                           lambda_init, batch, seq)
            xt = _out_proj_call(xt, g1, [y], [diff_w_out[o].astype(BF16)], seq)

        q, hb = _norm_matmul_call(xt, norm_ffn_g[l].reshape(1, d), sh2, sc2, [peer_w_q[l]], [F32], seq,
                                  split=True, emit_h=True, name="peer_query")
        r2, e2, n1, f = _peer_route_call(q, peer_subkeys[l], tt=PEER_TOKEN_TILE)
        v_tiles = peer_v[l].astype(BF16).reshape(-1, PEER_EXPERT_TILE, d).transpose(0, 2, 1)
        xt = _peer_expert_call(xt, g2, hb, peer_u[l].astype(BF16), v_tiles,
                               n1, f, r2, e2, seq, tt=PEER_TOKEN_TILE, te=PEER_EXPERT_TILE)

    return _final_norm_call(xt, final_g).reshape(batch, seq, d)
```

```python
import functools
import math

import jax
import jax.numpy as jnp
from jax import lax
from jax.experimental import pallas as pl
from jax.experimental.pallas import tpu as pltpu

F32 = jnp.float32
BF16 = jnp.bfloat16
HIGHEST = lax.Precision.HIGHEST

CONV_WIDTH = 512
CONV_KERNEL = 31
RWKV_WIDTH = 512
RWKV_HEAD = 64
RWKV_HEADS = 8
DECAY_LORA = 64
AAA_LORA = 64
GATE_LORA = 160
DIFF_HEADS = 8
DIFF_HEAD_DIM = 64
DIFF_V_DIM = 128
ROPE_DIM = 16
ROPE_THETA = 500000.0
ATTN_CHUNK = 64
PEER_HEADS = 8
PEER_KEYS = 128
PEER_HALF = 128
PEER_TOPK = 16
NORM_EPS = 1e-6
LN_EPS = 1e-5
RWKV_GN_EPS = 64e-5

LANES = 128
SCAN_CHUNK = 64
CONV_HALO = 32
PEER_EXPERT_TILE = 2048
PEER_ROUTE_TILES = 4
PEER_TOKEN_TILE = 256
VMEM_LIMIT = 48 * 1024 * 1024


def _cparams(semantics, flags=None):
    return pltpu.CompilerParams(dimension_semantics=semantics, vmem_limit_bytes=VMEM_LIMIT, flags=flags)


def _nt_dot(a, b, precision=None):
    return lax.dot_general(a, b, (((1,), (1,)), ((), ())), precision=precision,
                           preferred_element_type=F32)


def _dot(a, b, precision=None):
    return jnp.dot(a, b, precision=precision, preferred_element_type=F32)


def _sigmoid(x):
    return 1.0 / (1.0 + jnp.exp(-x))


def _ada_kernel(c_ref, w_ref, b_ref, o_ref):
    c = c_ref[...]
    cond = c * _sigmoid(c)
    o_ref[0] = _dot(cond, w_ref[0], HIGHEST) + b_ref[0]


def _ada_call(c, ada_w, ada_b):
    depth, d, n = ada_w.shape
    b = c.shape[0]
    tn = 1536
    return pl.pallas_call(
        _ada_kernel,
        grid=(depth, n // tn),
        in_specs=[
            pl.BlockSpec((b, d), lambda l, j: (0, 0)),
            pl.BlockSpec((1, d, tn), lambda l, j: (l, 0, j)),
            pl.BlockSpec((1, 1, tn), lambda l, j: (l, 0, j)),
        ],
        out_specs=pl.BlockSpec((1, b, tn), lambda l, j: (l, 0, j)),
        out_shape=jax.ShapeDtypeStruct((depth, b, n), F32),
        compiler_params=_cparams(("arbitrary", "arbitrary")),
        name="ada_mod",
    )(c, ada_w, ada_b.reshape(depth, 1, n))


def _modulate(x, g, shift, scale):
    ms = jnp.mean(x * x, axis=-1, keepdims=True)
    y = x * lax.rsqrt(ms + NORM_EPS)
    return (y * g) * (1.0 + scale) + shift


def _split_bf16(a):
    hi = a.astype(BF16)
    return hi, (a - hi.astype(F32)).astype(BF16)


def _norm_matmul_kernel(*refs, n_w, split, emit_h):
    x_ref, g_ref, sh_ref, sc_ref = refs[:4]
    w_refs = refs[4:4 + n_w]
    o_refs = refs[4 + n_w:]
    h = _modulate(x_ref[...], g_ref[...], sh_ref[0], sc_ref[0])
    hb = h.astype(BF16)
    if split:
        h_lo = (h - hb.astype(F32)).astype(BF16)
        for k in range(n_w // 2):
            w_hi, w_lo = w_refs[2 * k][...], w_refs[2 * k + 1][...]
            o_refs[k][...] = (_dot(hb, w_hi) + _dot(h_lo, w_hi) + _dot(hb, w_lo)).astype(o_refs[k].dtype)
    else:
        for w_ref, o_ref in zip(w_refs, o_refs[:n_w]):
            o_ref[...] = _dot(hb, w_ref[...]).astype(o_ref.dtype)
    if emit_h:
        o_refs[-1][...] = h.T.astype(BF16)


def _norm_matmul_call(x, g, shift, scale, weights, out_dtypes, seq, *, split=False, emit_h=False,
                      tm=512, name="norm_matmul"):
    t, d = x.shape
    if split:
        weights = [part for w in weights for part in _split_bf16(w)]
    in_specs = [
        pl.BlockSpec((tm, d), lambda i: (i, 0)),
        pl.BlockSpec((1, d), lambda i: (0, 0)),
        pl.BlockSpec((1, 1, d), lambda i: ((i * tm) // seq, 0, 0)),
        pl.BlockSpec((1, 1, d), lambda i: ((i * tm) // seq, 0, 0)),
    ]
    out_specs, out_shape = [], []
    for w in weights:
        in_specs.append(pl.BlockSpec(w.shape, lambda i: (0, 0)))
    for w, dt in zip(weights[::2] if split else weights, out_dtypes):
        n = w.shape[1]
        out_specs.append(pl.BlockSpec((tm, n), lambda i: (i, 0)))
        out_shape.append(jax.ShapeDtypeStruct((t, n), dt))
    if emit_h:
        out_specs.append(pl.BlockSpec((d, tm), lambda i: (0, i)))
        out_shape.append(jax.ShapeDtypeStruct((d, t), BF16))
    return pl.pallas_call(
        functools.partial(_norm_matmul_kernel, n_w=len(weights), split=split, emit_h=emit_h),
        grid=(t // tm,),
        in_specs=in_specs,
        out_specs=out_specs,
        out_shape=out_shape,
        compiler_params=_cparams(("arbitrary",)),
        name=name,
    )(x, g, shift, scale, *weights)


def _out_proj_kernel(*refs, n_y):
    x_ref, gate_ref = refs[:2]
    y_refs = refs[2:2 + n_y]
    w_refs = refs[2 + n_y:2 + 2 * n_y]
    o_ref = refs[2 + 2 * n_y]
    acc = _dot(y_refs[0][...], w_refs[0][...])
    for y_ref, w_ref in zip(y_refs[1:], w_refs[1:]):
        acc = acc + _dot(y_ref[...], w_ref[...])
    o_ref[...] = x_ref[...] + gate_ref[0] * acc


def _out_proj_call(x, gate, ys, ws, seq, *, tm=512):
    t, d = x.shape
    in_specs = [
        pl.BlockSpec((tm, d), lambda i: (i, 0)),
        pl.BlockSpec((1, 1, d), lambda i: ((i * tm) // seq, 0, 0)),
    ]
    for y in ys:
        in_specs.append(pl.BlockSpec((tm, y.shape[1]), lambda i: (i, 0)))
    for w in ws:
        in_specs.append(pl.BlockSpec(w.shape, lambda i: (0, 0)))
    return pl.pallas_call(
        functools.partial(_out_proj_kernel, n_y=len(ys)),
        grid=(t // tm,),
        in_specs=in_specs,
        out_specs=pl.BlockSpec((tm, d), lambda i: (i, 0)),
        out_shape=jax.ShapeDtypeStruct((t, d), F32),
        compiler_params=_cparams(("arbitrary",)),
        name="out_proj",
    )(x, gate, *ys, *ws)


def _conv_kernel(z_ref, w_ref, b_ref, g_ref, beta_ref, o_ref, ext_ref, *, ts):
    width = CONV_WIDTH

    @pl.when(pl.program_id(1) == 0)
    def _():
        ext_ref[0:CONV_HALO, :] = jnp.zeros((CONV_HALO, width), F32)

    z = z_ref[...]
    u = z[:, :width] * _sigmoid(z[:, width:])
    ext_ref[CONV_HALO:CONV_HALO + ts, :] = u
    base = CONV_HALO - (CONV_KERNEL - 1)
    rows = 64
    for cb in range(width // LANES):
        cs = slice(cb * LANES, (cb + 1) * LANES)
        for rb in range(ts // rows):
            acc = jnp.zeros((rows, LANES), F32)
            for j in range(CONV_KERNEL):
                start = rb * rows + base + j
                acc = acc + w_ref[j:j + 1, cs] * ext_ref[start:start + rows, cs]
            o_ref[rb * rows:(rb + 1) * rows, cs] = acc
    conv = o_ref[...] + b_ref[...]
    mu = jnp.mean(conv, axis=-1, keepdims=True)
    dlt = conv - mu
    var = jnp.mean(dlt * dlt, axis=-1, keepdims=True)
    y = dlt * lax.rsqrt(var + LN_EPS) * g_ref[...] + beta_ref[...]
    o_ref[...] = y * _sigmoid(y)
    ext_ref[0:CONV_HALO, :] = ext_ref[ts:ts + CONV_HALO, :]


def _conv_call(z, conv_w, conv_b, ln_g, ln_b, batch, seq, *, ts=256):
    t = z.shape[0]
    width = CONV_WIDTH
    nts = seq // ts
    vec = lambda a: a.reshape(1, width)
    return pl.pallas_call(
        functools.partial(_conv_kernel, ts=ts),
        grid=(batch, nts),
        in_specs=[
            pl.BlockSpec((ts, 2 * width), lambda b, i: (b * nts + i, 0)),
            pl.BlockSpec((CONV_KERNEL, width), lambda b, i: (0, 0)),
            pl.BlockSpec((1, width), lambda b, i: (0, 0)),
            pl.BlockSpec((1, width), lambda b, i: (0, 0)),
            pl.BlockSpec((1, width), lambda b, i: (0, 0)),
        ],
        out_specs=pl.BlockSpec((ts, width), lambda b, i: (b * nts + i, 0)),
        out_shape=jax.ShapeDtypeStruct((t, width), F32),
        scratch_shapes=[pltpu.VMEM((ts + CONV_HALO, width), F32)],
        compiler_params=_cparams(("arbitrary", "arbitrary")),
        name="conformer_conv",
    )(z, conv_w, vec(conv_b), vec(ln_g), vec(ln_b))


def _softplus(x):
    return jnp.maximum(x, 0.0) + jnp.log(1.0 + jnp.exp(-jnp.abs(x)))


def _unit_lower_inverses(mats, row, col, size):
    eye = (row == col).astype(F32)
    blk = lambda m: (row // m) == (col // m)
    bdot = lambda p, q: _dot(p.astype(BF16), q.astype(BF16))
    n1 = [jnp.where(blk(8), a, 0.0) for a in mats]
    t = [eye + x for x in n1]
    n2 = [bdot(x, x) for x in n1]
    t = [x + bdot(x, y) for x, y in zip(t, n2)]
    n4 = [bdot(x, x) for x in n2]
    t = [x + bdot(x, y) for x, y in zip(t, n4)]
    m = 8
    while m < size:
        new = blk(2 * m) & jnp.logical_not(blk(m))
        et = [bdot(jnp.where(new, a, 0.0), x) for a, x in zip(mats, t)]
        t = [x + bdot(x, y) for x, y in zip(t, et)]
        m *= 2
    return t


def _split_dot(x, w_bf16, parts):
    acc = None
    for _ in range(parts):
        piece = x.astype(BF16)
        term = _dot(piece, w_bf16)
        acc = term if acc is None else acc + term
        x = x - piece.astype(F32)
    return acc


RWKV_GROUP = 4
RWKV_BATCH_TILE = 2


def _rwkv_kernel(zr_ref, zl_ref, mur_ref, mul_ref, w0_ref, a0_ref, kk_ref, ka_ref, rk_ref,
                 lng_ref, lnb_ref, w2_ref, a2_ref, g2_ref, seg_ref, o_ref,
                 extr_ref, extl_ref, state_ref):
    c = SCAN_CHUNK
    rw = RWKV_WIDTH
    n = RWKV_HEAD
    nb = RWKV_BATCH_TILE
    gw = RWKV_GROUP * n
    groups = rw // gw

    @pl.when(pl.program_id(1) == 0)
    def _():
        extr_ref[:, 0:8, :] = jnp.zeros((nb, 8, 3 * rw), F32)
        extl_ref[:, 0:8, :] = jnp.zeros((nb, 8, rw), F32)
        state_ref[...] = jnp.zeros_like(state_ref)

    zr_rows, zl_rows = [], []
    for b in range(nb):
        zr = zr_ref[b]
        zl = zl_ref[b]
        extr_ref[b, 8:8 + c, :] = zr
        extl_ref[b, 8:8 + c, :] = zl
        zr_rows.append(zr + (extr_ref[b, 7:7 + c, :] - zr) * mur_ref[...])
        zl_rows.append(zl + (extl_ref[b, 7:7 + c, :] - zl) * mul_ref[...])
        extr_ref[b, 0:8, :] = extr_ref[b, c:c + 8, :]
        extl_ref[b, 0:8, :] = extl_ref[b, c:c + 8, :]
    zr = jnp.concatenate(zr_rows, axis=0)
    zl = jnp.concatenate(zl_rows, axis=0)

    r = zr[:, 0:rw]
    k = zr[:, rw:2 * rw]
    v = zr[:, 2 * rw:3 * rw]
    wd = zl[:, 0:LANES]
    ad = zl[:, LANES:2 * LANES]
    gd = zl[:, 2 * LANES:4 * LANES]
    seg = seg_ref[...]

    def segsum(x, parts):
        return jnp.concatenate([_split_dot(x[:, g * gw:(g + 1) * gw], seg, parts) for g in range(groups)],
                               axis=1)

    w_raw = -_softplus(-(w0_ref[...] + _dot(jnp.tanh(wd).astype(BF16), w2_ref[...]))) - 0.5
    lw = -jnp.exp(w_raw)
    alpha = _sigmoid(a0_ref[...] + _dot(ad.astype(BF16), a2_ref[...]))
    gate = _dot(_sigmoid(gd).astype(BF16), g2_ref[...])
    kkf = k * kk_ref[...]
    kk = kkf / jnp.maximum(jnp.sqrt(segsum(kkf * kkf, 3)), 1e-12)
    kp = k * (1.0 + (alpha - 1.0) * ka_ref[...])
    bonus = segsum(r * kp * rk_ref[...], 3) * v
    av = -kk
    bv = kk * alpha

    trow = lax.broadcasted_iota(jnp.int32, (nb * c, nb * c), 0)
    tcol = lax.broadcasted_iota(jnp.int32, (nb * c, nb * c), 1)
    same_seq_incl = ((trow // c) == (tcol // c)) & (tcol <= trow)
    cum = _split_dot_lhs_exact(jnp.where(same_seq_incl, 1.0, 0.0).astype(BF16), lw)
    tot = jnp.concatenate([jnp.broadcast_to(cum[(b + 1) * c - 1:(b + 1) * c, :], (c, rw)) for b in range(nb)],
                          axis=0)
    g_in = jnp.exp(cum)
    g_inp = jnp.exp(cum - lw)
    g_out = jnp.exp(-cum)
    g_end = jnp.exp(tot - cum)
    g_tot = jnp.exp(tot)
    a_in = av * g_inp
    r_in = r * g_in
    b_out = bv * g_out
    k_out = kp * g_out
    b_end = bv * g_end
    k_end = kp * g_end

    row = lax.broadcasted_iota(jnp.int32, (gw, gw), 0)
    col = lax.broadcasted_iota(jnp.int32, (gw, gw), 1)
    same_head = (row // n) == (col // n)
    strict = same_head & ((col % n) < (row % n))
    incl = same_head & ((col % n) <= (row % n))

    def expand(x):
        return jnp.where(same_head, jnp.concatenate([x] * RWKV_GROUP, axis=0), 0.0)

    bf = lambda x: x.astype(BF16)
    chains = [(b, g) for b in range(nb) for g in range(groups)]
    pick = lambda x: [expand(x[b * c:(b + 1) * c, g * gw:(g + 1) * gw]) for b, g in chains]
    each = lambda fn, *lists: [fn(*args) for args in zip(*lists)]
    a_bd, r_bd = each(bf, pick(a_in)), each(bf, pick(r_in))
    bo_bd, ko_bd = each(bf, pick(b_out)), each(bf, pick(k_out))
    v_f = pick(v)
    v_bd = each(bf, v_f)
    st = [state_ref[b, g] for b, g in chains]
    st_b = each(bf, st)
    a_ab = each(lambda p, q: jnp.where(strict, _nt_dot(p, q), 0.0), a_bd, bo_bd)
    a_ak = each(lambda p, q: bf(jnp.where(strict, _nt_dot(p, q), 0.0)), a_bd, ko_bd)
    a_rb = each(lambda p, q: bf(jnp.where(incl, _nt_dot(p, q), 0.0)), r_bd, bo_bd)
    a_rk = each(lambda p, q: bf(jnp.where(incl, _nt_dot(p, q), 0.0)), r_bd, ko_bd)
    ph_a = each(_nt_dot, a_bd, st_b)
    ph_r = each(_nt_dot, r_bd, st_b)
    tinv = _unit_lower_inverses(a_ab, row, col, n)
    rhs = each(lambda p, m, w: bf(p + _dot(m, w)), ph_a, a_ak, v_bd)
    u = each(lambda t, x: _dot(bf(t), x), tinv, rhs)
    y_bd = each(lambda p, m1, uu, m2, w: p + _dot(m1, bf(uu)) + _dot(m2, w), ph_r, a_rb, u, a_rk, v_bd)
    be_bd, ke_bd = each(bf, pick(b_end)), each(bf, pick(k_end))
    for (b, g), s0, uu, vf, be, ke in zip(chains, st, u, v_f, be_bd, ke_bd):
        state_ref[b, g] = (s0 * g_tot[b * c:b * c + 1, g * gw:(g + 1) * gw]
                           + _dot(bf(uu.T), be) + _dot(bf(vf.T), ke))
    y_g = [sum(m[i * c:(i + 1) * c] for i in range(RWKV_GROUP)) for m in y_bd]
    y = jnp.concatenate([jnp.concatenate(y_g[b * groups:(b + 1) * groups], axis=1) for b in range(nb)],
                        axis=0)

    inv_n = 1.0 / n
    mu_y = segsum(y, 3) * inv_n
    dy = y - mu_y
    var_y = segsum(dy * dy, 3) * inv_n
    yn = dy * lax.rsqrt(var_y + RWKV_GN_EPS)
    out = (yn * lng_ref[...] + lnb_ref[...] + bonus) * gate
    for b in range(nb):
        o_ref[b] = out[b * c:(b + 1) * c]


def _split_dot_lhs_exact(w_bf16, x):
    acc = None
    for _ in range(3):
        piece = x.astype(BF16)
        term = _dot(w_bf16, piece)
        acc = term if acc is None else acc + term
        x = x - piece.astype(F32)
    return acc


def _rwkv_call(zr, zl, p, batch, seq):
    c = SCAN_CHUNK
    rw = RWKV_WIDTH
    nb = RWKV_BATCH_TILE
    nc = seq // c
    gw = RWKV_GROUP * RWKV_HEAD
    const = lambda shape: pl.BlockSpec(shape, lambda b, i: (0,) * len(shape))
    vec = const((1, rw))
    out = pl.pallas_call(
        _rwkv_kernel,
        grid=(batch // nb, nc),
        in_specs=[
            pl.BlockSpec((nb, c, 3 * rw), lambda b, i: (b, i, 0)),
            pl.BlockSpec((nb, c, rw), lambda b, i: (b, i, 0)),
            const((1, 3 * rw)), vec,
            vec, vec, vec, vec, vec, vec, vec,
            const((LANES, rw)), const((LANES, rw)), const((2 * LANES, rw)),
            const((gw, gw)),
        ],
        out_specs=pl.BlockSpec((nb, c, rw), lambda b, i: (b, i, 0)),
        out_shape=jax.ShapeDtypeStruct((batch, seq, rw), F32),
        scratch_shapes=[
            pltpu.VMEM((nb, c + 8, 3 * rw), F32),
            pltpu.VMEM((nb, c + 8, rw), F32),
            pltpu.VMEM((nb, rw // gw, gw, gw), F32),
        ],
        compiler_params=_cparams(("arbitrary", "arbitrary")),
        name="rwkv7_mix",
    )(zr.reshape(batch, seq, 3 * rw), zl.reshape(batch, seq, rw), p["mu_r"], p["mu_l"], p["w0"], p["a0"],
      p["k_k"], p["k_a"], p["r_k"], p["ln_g"], p["ln_b"], p["w2"], p["a2"], p["g2"], p["seg"])
    return out.reshape(batch * seq, rw)


def _qkv_rope_kernel(x_ref, g_ref, sh_ref, sc_ref, pos_ref, inv_ref, sgn_ref, wq_ref, wk_ref, wv_ref,
                     q_ref, k_ref, v_ref):
    h = _modulate(x_ref[...], g_ref[...], sh_ref[0], sc_ref[0]).astype(BF16)
    ang = pos_ref[...].astype(F32) * inv_ref[...]
    cosf = jnp.cos(ang)
    sinf = jnp.sin(ang) * sgn_ref[...]
    lane = lax.broadcasted_iota(jnp.int32, ang.shape, 1)
    low = (lane % DIFF_HEAD_DIM) < (ROPE_DIM // 2)
    half = ROPE_DIM // 2

    def rope(w_ref, o_ref, scale):
        z = _dot(h, w_ref[...])
        for cb in range(z.shape[1] // LANES):
            zc = z[:, cb * LANES:(cb + 1) * LANES]
            partner = jnp.where(low, pltpu.roll(zc, LANES - half, axis=1), pltpu.roll(zc, half, axis=1))
            o_ref[:, cb * LANES:(cb + 1) * LANES] = ((zc * cosf + partner * sinf) * scale).astype(o_ref.dtype)

    rope(wq_ref, q_ref, DIFF_HEAD_DIM ** -0.5)
    rope(wk_ref, k_ref, 1.0)
    v_ref[...] = _dot(h, wv_ref[...]).astype(v_ref.dtype)


def _qkv_rope_call(x, g, shift, scale, pos, wq, wk, wv, seq, *, tm=512):
    t, d = x.shape
    half = ROPE_DIM // 2
    inv = 1.0 / (ROPE_THETA ** (jnp.arange(0, ROPE_DIM, 2, dtype=F32) / ROPE_DIM))
    dl = jnp.arange(LANES) % DIFF_HEAD_DIM
    inv_pat = jnp.where(dl < ROPE_DIM, inv[dl % half], 0.0).reshape(1, LANES).astype(F32)
    sgn_pat = jnp.where(dl < half, -1.0, 1.0).reshape(1, LANES).astype(F32)
    row = lambda n: pl.BlockSpec((tm, n), lambda i: (i, 0))
    full = lambda a: pl.BlockSpec(a.shape, lambda i: (0, 0))
    mod = pl.BlockSpec((1, 1, d), lambda i: ((i * tm) // seq, 0, 0))
    n = wq.shape[1]
    return pl.pallas_call(
        _qkv_rope_kernel,
        grid=(t // tm,),
        in_specs=[row(d), full(g), mod, mod, row(1), full(inv_pat), full(sgn_pat),
                  full(wq), full(wk), full(wv)],
        out_specs=[row(n), row(n), row(n)],
        out_shape=[jax.ShapeDtypeStruct((t, n), BF16)] * 3,
        compiler_params=_cparams(("arbitrary",)),
        name="qkv_rope",
    )(x, g, shift, scale, pos, inv_pat, sgn_pat, wq, wk, wv)


def _attn_kernel(q_ref, k_ref, v_ref, lq1_ref, lk1_ref, lq2_ref, lk2_ref, sg_ref, o_ref,
                 *, tq, seq, lambda_init):
    i = pl.program_id(2)
    kb = 2 * tq
    q = q_ref[...]
    lane = lax.broadcasted_iota(jnp.int32, q.shape, 1)
    zero = jnp.zeros_like(q)
    qm = (jnp.where(lane < DIFF_HEAD_DIM, q, zero), jnp.where(lane >= DIFF_HEAD_DIM, q, zero))
    f32sum = lambda a, b: jnp.sum(a[...] * b[...], axis=-1, keepdims=True)
    lam = jnp.exp(f32sum(lq1_ref, lk1_ref)) - jnp.exp(f32sum(lq2_ref, lk2_ref)) + lambda_init
    row_chunk = (i * tq + lax.broadcasted_iota(jnp.int32, (tq, kb), 0)) // ATTN_CHUNK
    col_in_block = lax.broadcasted_iota(jnp.int32, (tq, kb), 1)

    def attend(nblk):
        outs = []
        for mp in range(2):
            s = [_nt_dot(qm[mp], k_ref[j * kb:(j + 1) * kb, :]) for j in range(nblk)]
            col_chunk = ((nblk - 1) * kb + col_in_block) // ATTN_CHUNK
            s[-1] = jnp.where(col_chunk <= row_chunk, s[-1], -jnp.inf)
            m = jnp.max(s[0], axis=-1, keepdims=True)
            for sj in s[1:]:
                m = jnp.maximum(m, jnp.max(sj, axis=-1, keepdims=True))
            l = jnp.zeros_like(m)
            acc = jnp.zeros((tq, DIFF_V_DIM), F32)
            for j, sj in enumerate(s):
                p = jnp.exp(sj - m)
                l = l + jnp.sum(p, axis=-1, keepdims=True)
                acc = acc + _dot(p.astype(BF16), v_ref[j * kb:(j + 1) * kb, :])
            outs.append(acc / l)
        o = outs[0] - lam * outs[1]
        ms = jnp.mean(o * o, axis=-1, keepdims=True)
        o = o * lax.rsqrt(ms + NORM_EPS) * sg_ref[...] * (1.0 - lambda_init)
        o_ref[...] = o.astype(o_ref.dtype)

    for nblk in range(1, seq // kb + 1):
        pl.when(i // 2 == nblk - 1)(functools.partial(attend, nblk))


def _attn_call(q, k, v, lq1, lk1, lq2, lk2, subln_g, lambda_init, batch, seq, *, tq=256):
    t = q.shape[0]
    nq = seq // tq
    assert seq % (2 * tq) == 0
    vec = lambda a: a.reshape(1, -1)
    small = lambda a: pl.BlockSpec(a.shape, lambda b, h, i: (0, 0))
    kv = pl.BlockSpec((seq, DIFF_V_DIM), lambda b, h, i: (b, h))
    args = [vec(lq1), vec(lk1), vec(lq2), vec(lk2), vec(subln_g)]
    return pl.pallas_call(
        functools.partial(_attn_kernel, tq=tq, seq=seq, lambda_init=lambda_init),
        grid=(batch, DIFF_HEADS, nq),
        in_specs=[pl.BlockSpec((tq, DIFF_V_DIM), lambda b, h, i: (b * nq + i, h)), kv, kv]
                 + [small(a) for a in args],
        out_specs=pl.BlockSpec((tq, DIFF_V_DIM), lambda b, h, i: (b * nq + i, h)),
        out_shape=jax.ShapeDtypeStruct((t, DIFF_HEADS * DIFF_V_DIM), BF16),
        compiler_params=_cparams(("arbitrary", "arbitrary", "arbitrary")),
        name="diff_attn",
    )(q, k, v, *args)


def _peer_route_kernel(q_ref, sk_ref, r2_ref, e2_ref, n1_ref, f_ref, a_ref, b_ref, cand_ref, *, tt, tiles):
    neg = -jnp.inf
    k = PEER_TOPK
    kf = float(k)
    key_id = lax.broadcasted_iota(jnp.int32, (PEER_KEYS, LANES), 0).astype(F32)
    cand_id = lax.broadcasted_iota(jnp.int32, (k * k, LANES), 0).astype(F32)

    def extract(w, ids, dst_ref, by_index, want_rank):
        rank = jnp.full(w.shape, kf, F32) if want_rank else None
        tops = []
        for r in range(k):
            mx = jnp.max(w, axis=0, keepdims=True)
            tops.append(mx)
            if dst_ref is not None:
                dst_ref[r:r + 1, :] = mx
            hit = w == mx
            if by_index:
                first = jnp.min(jnp.where(hit, ids, float(w.shape[0])), axis=0, keepdims=True)
                hit = ids == first
            if want_rank:
                rank = jnp.where(hit, float(r), rank)
            w = jnp.where(hit, neg, w)
        return rank, tops, w

    def candidates(ga_ref, gb_ref):
        bvals = gb_ref[...]
        for x in range(k):
            cand_ref[x * k:(x + 1) * k, :] = ga_ref[x:x + 1, :] + bvals
        return cand_ref[...]

    def staircase_candidates(ga_ref, gb_ref):
        a = lambda lo, hi: ga_ref[lo:hi, :]
        b = lambda lo, hi: gb_ref[lo:hi, :]
        return jnp.concatenate([
            a(0, 1) + b(0, 8), a(0, 1) + b(8, 16), a(1, 2) + b(0, 8), a(2, 3) + b(0, 8), a(3, 4) + b(0, 8),
            a(0, 8) + b(0, 1), a(8, 16) + b(0, 1), a(0, 8) + b(1, 2), a(0, 8) + b(2, 3)], axis=0)

    def twin_bf16(v):
        bits = lax.bitcast_convert_type(v.astype(BF16).astype(F32), jnp.uint32)
        return bits | (bits >> 16)

    def emit(dst, s1, s2, rank2, n1, a0, b0, c_tops):
        tile, cols = dst
        zsum = jnp.zeros((1, LANES), F32)
        for c in c_tops:
            zsum = zsum + jnp.exp(c - c_tops[0])
        r2_ref[0, tile, :, cols] = pltpu.bitcast(rank2.astype(BF16), jnp.uint32)
        e2_ref[0, tile, :, cols] = pltpu.bitcast(jnp.exp(s2 - b0).astype(BF16), jnp.uint32)
        n1_ref[0, tile, :, cols] = twin_bf16(n1)
        f_ref[0, tile, :, cols] = twin_bf16(jnp.exp(s1 - a0) / zsum)

    count = lambda m: jnp.sum(jnp.where(m, 1.0, 0.0), axis=0, keepdims=True)

    def scores(rows):
        return (_nt_dot(sk_ref[0], q_ref[rows, 0:PEER_HALF], HIGHEST),
                _nt_dot(sk_ref[1], q_ref[rows, PEER_HALF:2 * PEER_HALF], HIGHEST))

    per_tile = tt // LANES
    groups = [(slice(g * LANES, (g + 1) * LANES),
               (g // per_tile, slice((g % per_tile) * LANES, (g % per_tile + 1) * LANES)))
              for g in range(tiles * per_tile)]
    any_tied = []
    for gidx, (rows, dst) in enumerate(groups):
        ga_ref, gb_ref = a_ref.at[gidx], b_ref.at[gidx]
        s1, s2 = scores(rows)
        _, a_tops, w1 = extract(s1, key_id, ga_ref, False, False)
        rank2, b_tops, _ = extract(s2, key_id, gb_ref, False, True)
        _, c_tops, _ = extract(staircase_candidates(ga_ref, gb_ref), None, None, False, False)
        tau = c_tops[k - 1]
        n1 = jnp.zeros_like(s1)
        for y in range(k):
            n1 = n1 + jnp.where((s1 + b_tops[y]) >= tau, 1.0, 0.0)
        emit(dst, s1, s2, rank2, n1, a_tops[0], b_tops[0], c_tops)
        tied = ((count(w1 == neg) != kf) | (count(rank2 < kf) != kf)
                | (jnp.sum(n1, axis=0, keepdims=True) != kf))
        any_tied.append(jnp.max(jnp.where(tied, 1.0, 0.0)) > 0.0)

    for gidx, (rows, dst) in enumerate(groups):
        @pl.when(any_tied[gidx])
        def _():
            ga_ref, gb_ref = a_ref.at[gidx], b_ref.at[gidx]
            s1, s2 = scores(rows)
            rank1, a_x, _ = extract(s1, key_id, ga_ref, True, True)
            rank2x, b_x, _ = extract(s2, key_id, gb_ref, True, True)
            rank3, c_x, _ = extract(candidates(ga_ref, gb_ref), cand_id, None, True, True)
            win = rank3 < kf
            n1x = jnp.zeros_like(s1)
            for x in range(k):
                n_x = count(win[x * k:(x + 1) * k])
                n1x = n1x + jnp.where(rank1 == float(x), n_x, 0.0)
            emit(dst, s1, s2, rank2x, n1x, a_x[0], b_x[0], c_x)


def _peer_route_call(q, subkeys, *, tt, tiles=PEER_ROUTE_TILES):
    t = q.shape[0]
    nt = t // tt
    spec = pl.BlockSpec((1, tiles, PEER_KEYS, tt), lambda i, h: (h, i, 0, 0))
    pair_spec = pl.BlockSpec((1, tiles, PEER_KEYS // 2, tt), lambda i, h: (h, i, 0, 0))
    tab = lambda rows: jax.ShapeDtypeStruct((PEER_HEADS, nt, rows, tt), jnp.uint32)
    return pl.pallas_call(
        functools.partial(_peer_route_kernel, tt=tt, tiles=tiles),
        grid=(nt // tiles, PEER_HEADS),
        in_specs=[
            pl.BlockSpec((tt * tiles, 2 * PEER_HALF), lambda i, h: (i, h)),
            pl.BlockSpec(subkeys.shape, lambda i, h: (0, 0, 0)),
        ],
        out_specs=[pair_spec, pair_spec, spec, spec],
        out_shape=[tab(PEER_KEYS // 2), tab(PEER_KEYS // 2), tab(PEER_KEYS), tab(PEER_KEYS)],
        scratch_shapes=[
            pltpu.VMEM((tiles * tt // LANES, PEER_TOPK, LANES), F32),
            pltpu.VMEM((tiles * tt // LANES, PEER_TOPK, LANES), F32),
            pltpu.VMEM((PEER_TOPK * PEER_TOPK, LANES), F32),
        ],
        compiler_params=_cparams(("arbitrary", "arbitrary")),
        name="peer_route",
    )(q, subkeys)


def _gelu(x):
    return 0.5 * x * (1.0 + lax.erf(x * (1.0 / math.sqrt(2.0))))


def _peer_expert_kernel(x_ref, gate_ref, ht_ref, u0_ref, u1_ref, vt0_ref, vt1_ref, n1_ref, f_ref, r2_ref,
                        e2_ref, o_ref, pre_a, pre_b, wa_ref, acc_ref, *, rows_per_tile):
    u_halves, vt_halves = (u0_ref, u1_ref), (vt0_ref, vt1_ref)
    s = pl.program_id(1)
    te, tt = pre_a.shape
    d = acc_ref.shape[0]
    blk = 16
    rows_per_piece = 2

    def fill_pre(pre_new):
        pre_rows = 512
        for m, u_ref in enumerate(u_halves):
            for r0 in range(0, te // 2, pre_rows):
                pre_new[m * te // 2 + r0:m * te // 2 + r0 + pre_rows, :] = _dot(u_ref[r0:r0 + pre_rows, :],
                                                                               ht_ref[...])

    def step(pre_new, pre_old):
        halves = [slice(m * d // 2, (m + 1) * d // 2) for m in range(2)]
        if pre_old is None:
            acc_ref[...] = jnp.zeros_like(acc_ref)
            fill_pre(pre_new)
            return
        zero = jnp.zeros((blk, LANES), BF16)
        as_bf16 = lambda words: pltpu.bitcast(words, BF16)
        for il in range(rows_per_tile):
            for lt in range(tt // LANES):
                lanes = slice(lt * LANES, (lt + 1) * LANES)
                row = lambda ref, h: as_bf16(jnp.broadcast_to(ref[h, 0, il:il + 1, lanes], (blk // 2, LANES)))
                n1b = [row(n1_ref, h) for h in range(PEER_HEADS)]
                fb = [row(f_ref, h) for h in range(PEER_HEADS)]
                for jb in range(PEER_KEYS // blk):
                    words = slice(jb * blk // 2, (jb + 1) * blk // 2)
                    w = zero
                    for h in range(PEER_HEADS):
                        hit = as_bf16(r2_ref[h, 0, words, lanes]) < n1b[h]
                        w = w + jnp.where(hit, as_bf16(e2_ref[h, 0, words, lanes]), zero) * fb[h]
                    rows = slice(il * PEER_KEYS + jb * blk, il * PEER_KEYS + (jb + 1) * blk)
                    wa_ref[rows, lanes] = w
        if pre_new is not None:
            fill_pre(pre_new)
        piece_rows = rows_per_piece * PEER_KEYS
        for p in range(te // piece_rows):
            piece = slice(p * piece_rows, (p + 1) * piece_rows)
            for r0 in range(p * piece_rows, (p + 1) * piece_rows, blk):
                rows = slice(r0, r0 + blk)
                wa_ref[rows, :] = wa_ref[rows, :] * _gelu(pre_old[rows, :]).astype(BF16)
            for rows, vt_ref in zip(halves, vt_halves):
                acc_ref[rows, :] += _dot(vt_ref[0, :, piece], wa_ref[piece, :])

    last = pl.num_programs(1) - 1
    inner = (s > 0) & (s < last)
    pl.when(s == 0)(lambda: step(pre_a, None))
    pl.when(inner & (s % 2 == 0))(lambda: step(pre_a, pre_b))
    pl.when(inner & (s % 2 == 1))(lambda: step(pre_b, pre_a))

    @pl.when(s == last)
    def _():
        step(None, pre_b)
        o_ref[...] = x_ref[...] + gate_ref[0] * acc_ref[...].T


def _peer_expert_call(x, gate, hb, u_b, vt_b, n1, f, r2, e2, seq, *, tt, te=1024):
    t, d = x.shape
    ne = u_b.shape[0] // te
    assert ne % 2 == 0, "the drain step reads the second pre-activation buffer"
    rows_per_tile = te // PEER_KEYS
    lag = lambda s, n: jnp.clip(s - n, 0, ne - 1)
    sel = pl.BlockSpec((PEER_HEADS, 1, rows_per_tile, tt), lambda i, s: (0, i, lag(s, 1), 0))
    full = pl.BlockSpec((PEER_HEADS, 1, PEER_KEYS // 2, tt), lambda i, s: (0, i, 0, 0))
    return pl.pallas_call(
        functools.partial(_peer_expert_kernel, rows_per_tile=rows_per_tile),
        grid=(t // tt, ne + 1),
        in_specs=[
            pl.BlockSpec((tt, d), lambda i, s: (i, 0)),
            pl.BlockSpec((1, 1, d), lambda i, s: ((i * tt) // seq, 0, 0)),
            pl.BlockSpec((d, tt), lambda i, s: (0, i)),
            pl.BlockSpec((te // 2, d), lambda i, s: (2 * lag(s, 0), 0)),
            pl.BlockSpec((te // 2, d), lambda i, s: (2 * lag(s, 0) + 1, 0)),
            pl.BlockSpec((1, d // 2, te), lambda i, s: (lag(s, 1), 0, 0)),
            pl.BlockSpec((1, d // 2, te), lambda i, s: (lag(s, 1), 1, 0)),
            sel, sel, full, full,
        ],
        out_specs=pl.BlockSpec((tt, d), lambda i, s: (i, 0)),
        out_shape=jax.ShapeDtypeStruct((t, d), F32),
        scratch_shapes=[
            pltpu.VMEM((te, tt), F32),
            pltpu.VMEM((te, tt), F32),
            pltpu.VMEM((te, tt), BF16),
            pltpu.VMEM((d, tt), F32),
        ],
        compiler_params=_cparams(("arbitrary", "arbitrary")),
        name="peer_experts",
    )(x, gate, hb, u_b, u_b, vt_b, vt_b, n1, f, r2, e2)


def _final_norm_kernel(x_ref, g_ref, o_ref):
    x = x_ref[...]
    ms = jnp.mean(x * x, axis=-1, keepdims=True)
    o_ref[...] = x * lax.rsqrt(ms + NORM_EPS) * g_ref[...]


def _final_norm_call(x, g, *, tm=512):
    t, d = x.shape
    return pl.pallas_call(
        _final_norm_kernel,
        grid=(t // tm,),
        in_specs=[pl.BlockSpec((tm, d), lambda i: (i, 0)), pl.BlockSpec((1, d), lambda i: (0, 0))],
        out_specs=pl.BlockSpec((tm, d), lambda i: (i, 0)),
        out_shape=jax.ShapeDtypeStruct((t, d), F32),
        compiler_params=_cparams(("arbitrary",)),
        name="final_norm",
    )(x, g.reshape(1, d))


def _pad_rows(a, rows):
    return jnp.pad(a, ((0, rows - a.shape[0]), (0, 0)))


def _pad_cols(a, cols):
    return jnp.pad(a, ((0, 0), (0, cols - a.shape[1])))


def _hybrid_mixer(x, g, shift, scale, batch, seq, w_in, conv_w, conv_b, conv_ln_g, conv_ln_b, mu, w0, w2,
                  a0, a2, g2, k_k, k_a, r_k, ln_g, ln_b):
    rw = RWKV_WIDTH
    c0 = 2 * CONV_WIDTH
    lora0 = c0 + 3 * rw
    l1, l2 = lora0 + DECAY_LORA, lora0 + DECAY_LORA + AAA_LORA
    w_conv = w_in[:, :c0].astype(BF16)
    w_rkv = w_in[:, c0:lora0].astype(BF16)
    w_lora = jnp.concatenate([
        _pad_cols(w_in[:, lora0:l1], LANES), _pad_cols(w_in[:, l1:l2], LANES),
        _pad_cols(w_in[:, l2:], 2 * LANES)], axis=1).astype(BF16)
    zc, zr, zl = _norm_matmul_call(x, g, shift, scale, [w_conv, w_rkv, w_lora], [F32, F32, F32], seq,
                                   name="hybrid_in_proj")
    ya = _conv_call(zc, conv_w, conv_b, conv_ln_g, conv_ln_b, batch, seq)
    mu_l = jnp.concatenate([
        _pad_cols(mu[None, lora0 - c0:l1 - c0], LANES), _pad_cols(mu[None, l1 - c0:l2 - c0], LANES),
        _pad_cols(mu[None, l2 - c0:], 2 * LANES)], axis=1)
    vec = lambda a: a.reshape(1, rw)
    head_id = jnp.arange(RWKV_GROUP * RWKV_HEAD) // RWKV_HEAD
    params = dict(
        mu_r=mu[None, :3 * rw], mu_l=mu_l, w0=vec(w0), a0=vec(a0), k_k=vec(k_k), k_a=vec(k_a),
        r_k=vec(r_k), ln_g=vec(ln_g), ln_b=vec(ln_b),
        w2=_pad_rows(w2, LANES).astype(BF16), a2=_pad_rows(a2, LANES).astype(BF16),
        g2=_pad_rows(g2, 2 * LANES).astype(BF16),
        seg=(head_id[:, None] == head_id[None, :]).astype(BF16))
    yb = _rwkv_call(zr, zl, params, batch, seq)
    return ya, yb


def kernel(x, c, positions, ada_w, ada_b, norm_mix_g, norm_ffn_g, hyb_w_in, conv_w, conv_b, conv_ln_g, conv_ln_b, rwkv_mu, rwkv_w0, rwkv_w2, rwkv_a0, rwkv_a2, rwkv_g2, rwkv_k_k, rwkv_k_a, rwkv_r_k, rwkv_ln_g, rwkv_ln_b, hyb_w_out, diff_w_qkv, diff_lq1, diff_lk1, diff_lq2, diff_lk2, diff_subln_g, diff_w_out, peer_w_q, peer_subkeys, peer_u, peer_v, final_g):
    batch, seq, d = x.shape
    depth = ada_w.shape[0]
    t = batch * seq
    xt = x.reshape(t, d)
    pos = positions.reshape(t, 1)
    mod = _ada_call(c, ada_w, ada_b).reshape(depth, batch, 6, 1, d)

    for l in range(depth):
        sh1, sc1, g1, sh2, sc2, g2 = (mod[l, :, j] for j in range(6))
        gmix = norm_mix_g[l].reshape(1, d)
        if l % 2 == 0:
            e = l // 2
            ya, yb = _hybrid_mixer(
                xt, gmix, sh1, sc1, batch, seq, hyb_w_in[e], conv_w[e], conv_b[e], conv_ln_g[e],
                conv_ln_b[e], rwkv_mu[e], rwkv_w0[e], rwkv_w2[e], rwkv_a0[e], rwkv_a2[e], rwkv_g2[e],
                rwkv_k_k[e], rwkv_k_a[e], rwkv_r_k[e], rwkv_ln_g[e], rwkv_ln_b[e])
            w_out = hyb_w_out[e].astype(BF16)
            xt = _out_proj_call(xt, g1, [ya.astype(BF16), yb.astype(BF16)],
                                [w_out[:CONV_WIDTH], w_out[CONV_WIDTH:]], seq)
        else:
            o = l // 2
            lambda_init = 0.8 - 0.6 * math.exp(-0.3 * l)
            wqkv = diff_w_qkv[o].astype(BF16)
            n = wqkv.shape[1] // 3
            q, k, v = _qkv_rope_call(xt, gmix, sh1, sc1, pos, wqkv[:, :n], wqkv[:, n:2 * n], wqkv[:, 2 * n:],
                                     seq)
            y = _attn_call(q, k, v, diff_lq1[o], diff_lk1[o], diff_lq2[o], diff_lk2[o], diff_subln_g[o],
                           lambda_init, batch, seq)
            xt = _out_proj_call(xt, g1, [y], [diff_w_out[o].astype(BF16)], seq)

        q, hb = _norm_matmul_call(xt, norm_ffn_g[l].reshape(1, d), sh2, sc2, [peer_w_q[l]], [F32], seq,
                                  split=True, emit_h=True, name="peer_query")
        r2, e2, n1, f = _peer_route_call(q, peer_subkeys[l], tt=PEER_TOKEN_TILE)
        v_tiles = peer_v[l].astype(BF16).reshape(-1, PEER_EXPERT_TILE, d).transpose(0, 2, 1)
        xt = _peer_expert_call(xt, g2, hb, peer_u[l].astype(BF16), v_tiles,
                               n1, f, r2, e2, seq, tt=PEER_TOKEN_TILE, te=PEER_EXPERT_TILE)

    return _final_norm_call(xt, final_g).reshape(batch, seq, d)
```

```python
import functools
import math

import jax
import jax.numpy as jnp
from jax import lax
from jax.experimental import pallas as pl
from jax.experimental.pallas import tpu as pltpu

F32 = jnp.float32
BF16 = jnp.bfloat16
HIGHEST = lax.Precision.HIGHEST

CONV_WIDTH = 512
CONV_KERNEL = 31
RWKV_WIDTH = 512
RWKV_HEAD = 64
RWKV_HEADS = 8
DECAY_LORA = 64
AAA_LORA = 64
GATE_LORA = 160
DIFF_HEADS = 8
DIFF_HEAD_DIM = 64
DIFF_V_DIM = 128
ROPE_DIM = 16
ROPE_THETA = 500000.0
ATTN_CHUNK = 64
PEER_HEADS = 8
PEER_KEYS = 128
PEER_HALF = 128
PEER_TOPK = 16
NORM_EPS = 1e-6
LN_EPS = 1e-5
RWKV_GN_EPS = 64e-5

LANES = 128
SCAN_CHUNK = 64
CONV_HALO = 32
PEER_EXPERT_TILE = 2048
PEER_ROUTE_TILES = 4
PEER_TOKEN_TILE = 256
VMEM_LIMIT = 48 * 1024 * 1024


def _cparams(semantics, flags=None):
    return pltpu.CompilerParams(dimension_semantics=semantics, vmem_limit_bytes=VMEM_LIMIT, flags=flags)


def _nt_dot(a, b, precision=None):
    return lax.dot_general(a, b, (((1,), (1,)), ((), ())), precision=precision,
                           preferred_element_type=F32)


def _dot(a, b, precision=None):
    return jnp.dot(a, b, precision=precision, preferred_element_type=F32)


def _sigmoid(x):
    return 1.0 / (1.0 + jnp.exp(-x))


def _ada_kernel(c_ref, w_ref, b_ref, o_ref):
    c = c_ref[...]
    cond = c * _sigmoid(c)
    o_ref[0] = _dot(cond, w_ref[0], HIGHEST) + b_ref[0]


def _ada_call(c, ada_w, ada_b):
    depth, d, n = ada_w.shape
    b = c.shape[0]
    tn = 1536
    return pl.pallas_call(
        _ada_kernel,
        grid=(depth, n // tn),
        in_specs=[
            pl.BlockSpec((b, d), lambda l, j: (0, 0)),
            pl.BlockSpec((1, d, tn), lambda l, j: (l, 0, j)),
            pl.BlockSpec((1, 1, tn), lambda l, j: (l, 0, j)),
        ],
        out_specs=pl.BlockSpec((1, b, tn), lambda l, j: (l, 0, j)),
        out_shape=jax.ShapeDtypeStruct((depth, b, n), F32),
        compiler_params=_cparams(("arbitrary", "arbitrary")),
        name="ada_mod",
    )(c, ada_w, ada_b.reshape(depth, 1, n))


def _modulate(x, g, shift, scale):
    ms = jnp.mean(x * x, axis=-1, keepdims=True)
    y = x * lax.rsqrt(ms + NORM_EPS)
    return (y * g) * (1.0 + scale) + shift


def _split_bf16(a):
    hi = a.astype(BF16)
    return hi, (a - hi.astype(F32)).astype(BF16)


def _norm_matmul_kernel(*refs, n_w, split, emit_h):
    x_ref, g_ref, sh_ref, sc_ref = refs[:4]
    w_refs = refs[4:4 + n_w]
    o_refs = refs[4 + n_w:]
    h = _modulate(x_ref[...], g_ref[...], sh_ref[0], sc_ref[0])
    hb = h.astype(BF16)
    if split:
        h_lo = (h - hb.astype(F32)).astype(BF16)
        for k in range(n_w // 2):
            w_hi, w_lo = w_refs[2 * k][...], w_refs[2 * k + 1][...]
            o_refs[k][...] = (_dot(hb, w_hi) + _dot(h_lo, w_hi) + _dot(hb, w_lo)).astype(o_refs[k].dtype)
    else:
        for w_ref, o_ref in zip(w_refs, o_refs[:n_w]):
            o_ref[...] = _dot(hb, w_ref[...]).astype(o_ref.dtype)
    if emit_h:
        o_refs[-1][...] = h.T.astype(BF16)


def _norm_matmul_call(x, g, shift, scale, weights, out_dtypes, seq, *, split=False, emit_h=False,
                      tm=512, name="norm_matmul"):
    t, d = x.shape
    if split:
        weights = [part for w in weights for part in _split_bf16(w)]
    in_specs = [
        pl.BlockSpec((tm, d), lambda i: (i, 0)),
        pl.BlockSpec((1, d), lambda i: (0, 0)),
        pl.BlockSpec((1, 1, d), lambda i: ((i * tm) // seq, 0, 0)),
        pl.BlockSpec((1, 1, d), lambda i: ((i * tm) // seq, 0, 0)),
    ]
    out_specs, out_shape = [], []
    for w in weights:
        in_specs.append(pl.BlockSpec(w.shape, lambda i: (0, 0)))
    for w, dt in zip(weights[::2] if split else weights, out_dtypes):
        n = w.shape[1]
        out_specs.append(pl.BlockSpec((tm, n), lambda i: (i, 0)))
        out_shape.append(jax.ShapeDtypeStruct((t, n), dt))
    if emit_h:
        out_specs.append(pl.BlockSpec((d, tm), lambda i: (0, i)))
        out_shape.append(jax.ShapeDtypeStruct((d, t), BF16))
    return pl.pallas_call(
        functools.partial(_norm_matmul_kernel, n_w=len(weights), split=split, emit_h=emit_h),
        grid=(t // tm,),
        in_specs=in_specs,
        out_specs=out_specs,
        out_shape=out_shape,
        compiler_params=_cparams(("arbitrary",)),
        name=name,
    )(x, g, shift, scale, *weights)


def _out_proj_kernel(*refs, n_y):
    x_ref, gate_ref = refs[:2]
    y_refs = refs[2:2 + n_y]
    w_refs = refs[2 + n_y:2 + 2 * n_y]
    o_ref = refs[2 + 2 * n_y]
    acc = _dot(y_refs[0][...], w_refs[0][...])
    for y_ref, w_ref in zip(y_refs[1:], w_refs[1:]):
        acc = acc + _dot(y_ref[...], w_ref[...])
    o_ref[...] = x_ref[...] + gate_ref[0] * acc


def _out_proj_call(x, gate, ys, ws, seq, *, tm=512):
    t, d = x.shape
    in_specs = [
        pl.BlockSpec((tm, d), lambda i: (i, 0)),
        pl.BlockSpec((1, 1, d), lambda i: ((i * tm) // seq, 0, 0)),
    ]
    for y in ys:
        in_specs.append(pl.BlockSpec((tm, y.shape[1]), lambda i: (i, 0)))
    for w in ws:
        in_specs.append(pl.BlockSpec(w.shape, lambda i: (0, 0)))
    return pl.pallas_call(
        functools.partial(_out_proj_kernel, n_y=len(ys)),
        grid=(t // tm,),
        in_specs=in_specs,
        out_specs=pl.BlockSpec((tm, d), lambda i: (i, 0)),
        out_shape=jax.ShapeDtypeStruct((t, d), F32),
        compiler_params=_cparams(("arbitrary",)),
        name="out_proj",
    )(x, gate, *ys, *ws)


def _conv_kernel(z_ref, w_ref, b_ref, g_ref, beta_ref, o_ref, ext_ref, *, ts):
    width = CONV_WIDTH

    @pl.when(pl.program_id(1) == 0)
    def _():
        ext_ref[0:CONV_HALO, :] = jnp.zeros((CONV_HALO, width), F32)

    z = z_ref[...]
    u = z[:, :width] * _sigmoid(z[:, width:])
    ext_ref[CONV_HALO:CONV_HALO + ts, :] = u
    base = CONV_HALO - (CONV_KERNEL - 1)
    rows = 64
    for cb in range(width // LANES):
        cs = slice(cb * LANES, (cb + 1) * LANES)
        for rb in range(ts // rows):
            acc = jnp.zeros((rows, LANES), F32)
            for j in range(CONV_KERNEL):
                start = rb * rows + base + j
                acc = acc + w_ref[j:j + 1, cs] * ext_ref[start:start + rows, cs]
            o_ref[rb * rows:(rb + 1) * rows, cs] = acc
    conv = o_ref[...] + b_ref[...]
    mu = jnp.mean(conv, axis=-1, keepdims=True)
    dlt = conv - mu
    var = jnp.mean(dlt * dlt, axis=-1, keepdims=True)
    y = dlt * lax.rsqrt(var + LN_EPS) * g_ref[...] + beta_ref[...]
    o_ref[...] = y * _sigmoid(y)
    ext_ref[0:CONV_HALO, :] = ext_ref[ts:ts + CONV_HALO, :]


def _conv_call(z, conv_w, conv_b, ln_g, ln_b, batch, seq, *, ts=256):
    t = z.shape[0]
    width = CONV_WIDTH
    nts = seq // ts
    vec = lambda a: a.reshape(1, width)
    return pl.pallas_call(
        functools.partial(_conv_kernel, ts=ts),
        grid=(batch, nts),
        in_specs=[
            pl.BlockSpec((ts, 2 * width), lambda b, i: (b * nts + i, 0)),
            pl.BlockSpec((CONV_KERNEL, width), lambda b, i: (0, 0)),
            pl.BlockSpec((1, width), lambda b, i: (0, 0)),
            pl.BlockSpec((1, width), lambda b, i: (0, 0)),
            pl.BlockSpec((1, width), lambda b, i: (0, 0)),
        ],
        out_specs=pl.BlockSpec((ts, width), lambda b, i: (b * nts + i, 0)),
        out_shape=jax.ShapeDtypeStruct((t, width), F32),
        scratch_shapes=[pltpu.VMEM((ts + CONV_HALO, width), F32)],
        compiler_params=_cparams(("arbitrary", "arbitrary")),
        name="conformer_conv",
    )(z, conv_w, vec(conv_b), vec(ln_g), vec(ln_b))


def _softplus(x):
    return jnp.maximum(x, 0.0) + jnp.log(1.0 + jnp.exp(-jnp.abs(x)))


def _unit_lower_inverses(mats, row, col, size):
    eye = (row == col).astype(F32)
    blk = lambda m: (row // m) == (col // m)
    bdot = lambda p, q: _dot(p.astype(BF16), q.astype(BF16))
    n1 = [jnp.where(blk(8), a, 0.0) for a in mats]
    t = [eye + x for x in n1]
    n2 = [bdot(x, x) for x in n1]
    t = [x + bdot(x, y) for x, y in zip(t, n2)]
    n4 = [bdot(x, x) for x in n2]
    t = [x + bdot(x, y) for x, y in zip(t, n4)]
    m = 8
    while m < size:
        new = blk(2 * m) & jnp.logical_not(blk(m))
        et = [bdot(jnp.where(new, a, 0.0), x) for a, x in zip(mats, t)]
        t = [x + bdot(x, y) for x, y in zip(t, et)]
        m *= 2
    return t


def _split_dot(x, w_bf16, parts):
    acc = None
    for _ in range(parts):
        piece = x.astype(BF16)
        term = _dot(piece, w_bf16)
        acc = term if acc is None else acc + term
        x = x - piece.astype(F32)
    return acc


RWKV_GROUP = 4
RWKV_BATCH_TILE = 4


def _rwkv_kernel(zr_ref, zl_ref, mur_ref, mul_ref, w0_ref, a0_ref, kk_ref, ka_ref, rk_ref,
                 lng_ref, lnb_ref, w2_ref, a2_ref, g2_ref, seg_ref, o_ref,
                 extr_ref, extl_ref, state_ref):
    c = SCAN_CHUNK
    rw = RWKV_WIDTH
    n = RWKV_HEAD
    nb = RWKV_BATCH_TILE
    gw = RWKV_GROUP * n
    groups = rw // gw

    @pl.when(pl.program_id(1) == 0)
    def _():
        extr_ref[:, 0:8, :] = jnp.zeros((nb, 8, 3 * rw), F32)
        extl_ref[:, 0:8, :] = jnp.zeros((nb, 8, rw), F32)
        state_ref[...] = jnp.zeros_like(state_ref)

    zr_rows, zl_rows = [], []
    for b in range(nb):
        zr = zr_ref[b]
        zl = zl_ref[b]
        extr_ref[b, 8:8 + c, :] = zr
        extl_ref[b, 8:8 + c, :] = zl
        zr_rows.append(zr + (extr_ref[b, 7:7 + c, :] - zr) * mur_ref[...])
        zl_rows.append(zl + (extl_ref[b, 7:7 + c, :] - zl) * mul_ref[...])
        extr_ref[b, 0:8, :] = extr_ref[b, c:c + 8, :]
        extl_ref[b, 0:8, :] = extl_ref[b, c:c + 8, :]
    zr = jnp.concatenate(zr_rows, axis=0)
    zl = jnp.concatenate(zl_rows, axis=0)

    r = zr[:, 0:rw]
    k = zr[:, rw:2 * rw]
    v = zr[:, 2 * rw:3 * rw]
    wd = zl[:, 0:LANES]
    ad = zl[:, LANES:2 * LANES]
    gd = zl[:, 2 * LANES:4 * LANES]
    seg = seg_ref[...]

    def segsum(x, parts):
        return jnp.concatenate([_split_dot(x[:, g * gw:(g + 1) * gw], seg, parts) for g in range(groups)],
                               axis=1)

    w_raw = -_softplus(-(w0_ref[...] + _dot(jnp.tanh(wd).astype(BF16), w2_ref[...]))) - 0.5
    lw = -jnp.exp(w_raw)
    alpha = _sigmoid(a0_ref[...] + _dot(ad.astype(BF16), a2_ref[...]))
    gate = _dot(_sigmoid(gd).astype(BF16), g2_ref[...])
    kkf = k * kk_ref[...]
    kk = kkf / jnp.maximum(jnp.sqrt(segsum(kkf * kkf, 3)), 1e-12)
    kp = k * (1.0 + (alpha - 1.0) * ka_ref[...])
    bonus = segsum(r * kp * rk_ref[...], 3) * v
    av = -kk
    bv = kk * alpha

    trow = lax.broadcasted_iota(jnp.int32, (nb * c, nb * c), 0)
    tcol = lax.broadcasted_iota(jnp.int32, (nb * c, nb * c), 1)
    same_seq_incl = ((trow // c) == (tcol // c)) & (tcol <= trow)
    cum = _split_dot_lhs_exact(jnp.where(same_seq_incl, 1.0, 0.0).astype(BF16), lw)
    tot = jnp.concatenate([jnp.broadcast_to(cum[(b + 1) * c - 1:(b + 1) * c, :], (c, rw)) for b in range(nb)],
                          axis=0)
    g_in = jnp.exp(cum)
    g_inp = jnp.exp(cum - lw)
    g_out = jnp.exp(-cum)
    g_end = jnp.exp(tot - cum)
    g_tot = jnp.exp(tot)
    a_in = av * g_inp
    r_in = r * g_in
    b_out = bv * g_out
    k_out = kp * g_out
    b_end = bv * g_end
    k_end = kp * g_end

    row = lax.broadcasted_iota(jnp.int32, (gw, gw), 0)
    col = lax.broadcasted_iota(jnp.int32, (gw, gw), 1)
    same_head = (row // n) == (col // n)
    strict = same_head & ((col % n) < (row % n))
    incl = same_head & ((col % n) <= (row % n))

    def expand(x):
        return jnp.where(same_head, jnp.concatenate([x] * RWKV_GROUP, axis=0), 0.0)

    bf = lambda x: x.astype(BF16)
    chains = [(b, g) for b in range(nb) for g in range(groups)]
    pick = lambda x: [expand(x[b * c:(b + 1) * c, g * gw:(g + 1) * gw]) for b, g in chains]
    each = lambda fn, *lists: [fn(*args) for args in zip(*lists)]
    a_bd, r_bd = each(bf, pick(a_in)), each(bf, pick(r_in))
    bo_bd, ko_bd = each(bf, pick(b_out)), each(bf, pick(k_out))
    v_f = pick(v)
    v_bd = each(bf, v_f)
    st = [state_ref[b, g] for b, g in chains]
    st_b = each(bf, st)
    a_ab = each(lambda p, q: jnp.where(strict, _nt_dot(p, q), 0.0), a_bd, bo_bd)
    a_ak = each(lambda p, q: bf(jnp.where(strict, _nt_dot(p, q), 0.0)), a_bd, ko_bd)
    a_rb = each(lambda p, q: bf(jnp.where(incl, _nt_dot(p, q), 0.0)), r_bd, bo_bd)
    a_rk = each(lambda p, q: bf(jnp.where(incl, _nt_dot(p, q), 0.0)), r_bd, ko_bd)
    ph_a = each(_nt_dot, a_bd, st_b)
    ph_r = each(_nt_dot, r_bd, st_b)
    tinv = _unit_lower_inverses(a_ab, row, col, n)
    rhs = each(lambda p, m, w: bf(p + _dot(m, w)), ph_a, a_ak, v_bd)
    u = each(lambda t, x: _dot(bf(t), x), tinv, rhs)
    y_bd = each(lambda p, m1, uu, m2, w: p + _dot(m1, bf(uu)) + _dot(m2, w), ph_r, a_rb, u, a_rk, v_bd)
    be_bd, ke_bd = each(bf, pick(b_end)), each(bf, pick(k_end))
    for (b, g), s0, uu, vf, be, ke in zip(chains, st, u, v_f, be_bd, ke_bd):
        state_ref[b, g] = (s0 * g_tot[b * c:b * c + 1, g * gw:(g + 1) * gw]
                           + _dot(bf(uu.T), be) + _dot(bf(vf.T), ke))
    y_g = [sum(m[i * c:(i + 1) * c] for i in range(RWKV_GROUP)) for m in y_bd]
    y = jnp.concatenate([jnp.concatenate(y_g[b * groups:(b + 1) * groups], axis=1) for b in range(nb)],
                        axis=0)

    inv_n = 1.0 / n
    mu_y = segsum(y, 3) * inv_n
    dy = y - mu_y
    var_y = segsum(dy * dy, 3) * inv_n
    yn = dy * lax.rsqrt(var_y + RWKV_GN_EPS)
    out = (yn * lng_ref[...] + lnb_ref[...] + bonus) * gate
    for b in range(nb):
        o_ref[b] = out[b * c:(b + 1) * c]


def _split_dot_lhs_exact(w_bf16, x):
    acc = None
    for _ in range(3):
        piece = x.astype(BF16)
        term = _dot(w_bf16, piece)
        acc = term if acc is None else acc + term
        x = x - piece.astype(F32)
    return acc


def _rwkv_call(zr, zl, p, batch, seq):
    c = SCAN_CHUNK
    rw = RWKV_WIDTH
    nb = RWKV_BATCH_TILE
    nc = seq // c
    gw = RWKV_GROUP * RWKV_HEAD
    const = lambda shape: pl.BlockSpec(shape, lambda b, i: (0,) * len(shape))
    vec = const((1, rw))
    out = pl.pallas_call(
        _rwkv_kernel,
        grid=(batch // nb, nc),
        in_specs=[
            pl.BlockSpec((nb, c, 3 * rw), lambda b, i: (b, i, 0)),
            pl.BlockSpec((nb, c, rw), lambda b, i: (b, i, 0)),
            const((1, 3 * rw)), vec,
            vec, vec, vec, vec, vec, vec, vec,
            const((LANES, rw)), const((LANES, rw)), const((2 * LANES, rw)),
            const((gw, gw)),
        ],
        out_specs=pl.BlockSpec((nb, c, rw), lambda b, i: (b, i, 0)),
        out_shape=jax.ShapeDtypeStruct((batch, seq, rw), F32),
        scratch_shapes=[
            pltpu.VMEM((nb, c + 8, 3 * rw), F32),
            pltpu.VMEM((nb, c + 8, rw), F32),
            pltpu.VMEM((nb, rw // gw, gw, gw), F32),
        ],
        compiler_params=_cparams(("arbitrary", "arbitrary")),
        name="rwkv7_mix",
    )(zr.reshape(batch, seq, 3 * rw), zl.reshape(batch, seq, rw), p["mu_r"], p["mu_l"], p["w0"], p["a0"],
      p["k_k"], p["k_a"], p["r_k"], p["ln_g"], p["ln_b"], p["w2"], p["a2"], p["g2"], p["seg"])
    return out.reshape(batch * seq, rw)


def _qkv_rope_kernel(x_ref, g_ref, sh_ref, sc_ref, pos_ref, inv_ref, sgn_ref, wq_ref, wk_ref, wv_ref,
                     q_ref, k_ref, v_ref):
    h = _modulate(x_ref[...], g_ref[...], sh_ref[0], sc_ref[0]).astype(BF16)
    ang = pos_ref[...].astype(F32) * inv_ref[...]
    cosf = jnp.cos(ang)
    sinf = jnp.sin(ang) * sgn_ref[...]
    lane = lax.broadcasted_iota(jnp.int32, ang.shape, 1)
    low = (lane % DIFF_HEAD_DIM) < (ROPE_DIM // 2)
    half = ROPE_DIM // 2

    def rope(w_ref, o_ref, scale):
        z = _dot(h, w_ref[...])
        for cb in range(z.shape[1] // LANES):
            zc = z[:, cb * LANES:(cb + 1) * LANES]
            partner = jnp.where(low, pltpu.roll(zc, LANES - half, axis=1), pltpu.roll(zc, half, axis=1))
            o_ref[:, cb * LANES:(cb + 1) * LANES] = ((zc * cosf + partner * sinf) * scale).astype(o_ref.dtype)

    rope(wq_ref, q_ref, DIFF_HEAD_DIM ** -0.5)
    rope(wk_ref, k_ref, 1.0)
    v_ref[...] = _dot(h, wv_ref[...]).astype(v_ref.dtype)


def _qkv_rope_call(x, g, shift, scale, pos, wq, wk, wv, seq, *, tm=512):
    t, d = x.shape
    half = ROPE_DIM // 2
    inv = 1.0 / (ROPE_THETA ** (jnp.arange(0, ROPE_DIM, 2, dtype=F32) / ROPE_DIM))
    dl = jnp.arange(LANES) % DIFF_HEAD_DIM
    inv_pat = jnp.where(dl < ROPE_DIM, inv[dl % half], 0.0).reshape(1, LANES).astype(F32)
    sgn_pat = jnp.where(dl < half, -1.0, 1.0).reshape(1, LANES).astype(F32)
    row = lambda n: pl.BlockSpec((tm, n), lambda i: (i, 0))
    full = lambda a: pl.BlockSpec(a.shape, lambda i: (0, 0))
    mod = pl.BlockSpec((1, 1, d), lambda i: ((i * tm) // seq, 0, 0))
    n = wq.shape[1]
    return pl.pallas_call(
        _qkv_rope_kernel,
        grid=(t // tm,),
        in_specs=[row(d), full(g), mod, mod, row(1), full(inv_pat), full(sgn_pat),
                  full(wq), full(wk), full(wv)],
        out_specs=[row(n), row(n), row(n)],
        out_shape=[jax.ShapeDtypeStruct((t, n), BF16)] * 3,
        compiler_params=_cparams(("arbitrary",)),
        name="qkv_rope",
    )(x, g, shift, scale, pos, inv_pat, sgn_pat, wq, wk, wv)


def _attn_kernel(q_ref, k_ref, v_ref, lq1_ref, lk1_ref, lq2_ref, lk2_ref, sg_ref, o_ref,
                 *, tq, seq, lambda_init):
    i = pl.program_id(2)
    kb = 2 * tq
    q = q_ref[...]
    lane = lax.broadcasted_iota(jnp.int32, q.shape, 1)
    zero = jnp.zeros_like(q)
    qm = (jnp.where(lane < DIFF_HEAD_DIM, q, zero), jnp.where(lane >= DIFF_HEAD_DIM, q, zero))
    f32sum = lambda a, b: jnp.sum(a[...] * b[...], axis=-1, keepdims=True)
    lam = jnp.exp(f32sum(lq1_ref, lk1_ref)) - jnp.exp(f32sum(lq2_ref, lk2_ref)) + lambda_init
    row_chunk = (i * tq + lax.broadcasted_iota(jnp.int32, (tq, kb), 0)) // ATTN_CHUNK
    col_in_block = lax.broadcasted_iota(jnp.int32, (tq, kb), 1)

    def attend(nblk):
        outs = []
        for mp in range(2):
            s = [_nt_dot(qm[mp], k_ref[j * kb:(j + 1) * kb, :]) for j in range(nblk)]
            col_chunk = ((nblk - 1) * kb + col_in_block) // ATTN_CHUNK
            s[-1] = jnp.where(col_chunk <= row_chunk, s[-1], -jnp.inf)
            m = jnp.max(s[0], axis=-1, keepdims=True)
            for sj in s[1:]:
                m = jnp.maximum(m, jnp.max(sj, axis=-1, keepdims=True))
            l = jnp.zeros_like(m)
            acc = jnp.zeros((tq, DIFF_V_DIM), F32)
            for j, sj in enumerate(s):
                p = jnp.exp(sj - m)
                l = l + jnp.sum(p, axis=-1, keepdims=True)
                acc = acc + _dot(p.astype(BF16), v_ref[j * kb:(j + 1) * kb, :])
            outs.append(acc / l)
        o = outs[0] - lam * outs[1]
        ms = jnp.mean(o * o, axis=-1, keepdims=True)
        o = o * lax.rsqrt(ms + NORM_EPS) * sg_ref[...] * (1.0 - lambda_init)
        o_ref[...] = o.astype(o_ref.dtype)

    for nblk in range(1, seq // kb + 1):
        pl.when(i // 2 == nblk - 1)(functools.partial(attend, nblk))


def _attn_call(q, k, v, lq1, lk1, lq2, lk2, subln_g, lambda_init, batch, seq, *, tq=256):
    t = q.shape[0]
    nq = seq // tq
    assert seq % (2 * tq) == 0
    vec = lambda a: a.reshape(1, -1)
    small = lambda a: pl.BlockSpec(a.shape, lambda b, h, i: (0, 0))
    kv = pl.BlockSpec((seq, DIFF_V_DIM), lambda b, h, i: (b, h))
    args = [vec(lq1), vec(lk1), vec(lq2), vec(lk2), vec(subln_g)]
    return pl.pallas_call(
        functools.partial(_attn_kernel, tq=tq, seq=seq, lambda_init=lambda_init),
        grid=(batch, DIFF_HEADS, nq),
        in_specs=[pl.BlockSpec((tq, DIFF_V_DIM), lambda b, h, i: (b * nq + i, h)), kv, kv]
                 + [small(a) for a in args],
        out_specs=pl.BlockSpec((tq, DIFF_V_DIM), lambda b, h, i: (b * nq + i, h)),
        out_shape=jax.ShapeDtypeStruct((t, DIFF_HEADS * DIFF_V_DIM), BF16),
        compiler_params=_cparams(("arbitrary", "arbitrary", "arbitrary")),
        name="diff_attn",
    )(q, k, v, *args)


def _peer_route_kernel(q_ref, sk_ref, r2_ref, e2_ref, n1_ref, f_ref, a_ref, b_ref, cand_ref, *, tt, tiles):
    neg = -jnp.inf
    k = PEER_TOPK
    kf = float(k)
    key_id = lax.broadcasted_iota(jnp.int32, (PEER_KEYS, LANES), 0).astype(F32)
    cand_id = lax.broadcasted_iota(jnp.int32, (k * k, LANES), 0).astype(F32)

    def extract(w, ids, dst_ref, by_index, want_rank):
        rank = jnp.full(w.shape, kf, F32) if want_rank else None
        tops = []
        for r in range(k):
            mx = jnp.max(w, axis=0, keepdims=True)
            tops.append(mx)
            if dst_ref is not None:
                dst_ref[r:r + 1, :] = mx
            hit = w == mx
            if by_index:
                first = jnp.min(jnp.where(hit, ids, float(w.shape[0])), axis=0, keepdims=True)
                hit = ids == first
            if want_rank:
                rank = jnp.where(hit, float(r), rank)
            w = jnp.where(hit, neg, w)
        return rank, tops, w

    def candidates(ga_ref, gb_ref):
        bvals = gb_ref[...]
        for x in range(k):
            cand_ref[x * k:(x + 1) * k, :] = ga_ref[x:x + 1, :] + bvals
        return cand_ref[...]

    def staircase_candidates(ga_ref, gb_ref):
        a = lambda lo, hi: ga_ref[lo:hi, :]
        b = lambda lo, hi: gb_ref[lo:hi, :]
        return jnp.concatenate([
            a(0, 1) + b(0, 8), a(0, 1) + b(8, 16), a(1, 2) + b(0, 8), a(2, 3) + b(0, 8), a(3, 4) + b(0, 8),
            a(0, 8) + b(0, 1), a(8, 16) + b(0, 1), a(0, 8) + b(1, 2), a(0, 8) + b(2, 3)], axis=0)

    def twin_bf16(v):
        bits = lax.bitcast_convert_type(v.astype(BF16).astype(F32), jnp.uint32)
        return bits | (bits >> 16)

    def emit(dst, s1, s2, rank2, n1, a0, b0, c_tops):
        tile, cols = dst
        zsum = jnp.zeros((1, LANES), F32)
        for c in c_tops:
            zsum = zsum + jnp.exp(c - c_tops[0])
        r2_ref[0, tile, :, cols] = pltpu.bitcast(rank2.astype(BF16), jnp.uint32)
        e2_ref[0, tile, :, cols] = pltpu.bitcast(jnp.exp(s2 - b0).astype(BF16), jnp.uint32)
        n1_ref[0, tile, :, cols] = twin_bf16(n1)
        f_ref[0, tile, :, cols] = twin_bf16(jnp.exp(s1 - a0) / zsum)

    count = lambda m: jnp.sum(jnp.where(m, 1.0, 0.0), axis=0, keepdims=True)

    def scores(rows):
        return (_nt_dot(sk_ref[0], q_ref[rows, 0:PEER_HALF], HIGHEST),
                _nt_dot(sk_ref[1], q_ref[rows, PEER_HALF:2 * PEER_HALF], HIGHEST))

    per_tile = tt // LANES
    groups = [(slice(g * LANES, (g + 1) * LANES),
               (g // per_tile, slice((g % per_tile) * LANES, (g % per_tile + 1) * LANES)))
              for g in range(tiles * per_tile)]
    any_tied = []
    for gidx, (rows, dst) in enumerate(groups):
        ga_ref, gb_ref = a_ref.at[gidx], b_ref.at[gidx]
        s1, s2 = scores(rows)
        _, a_tops, w1 = extract(s1, key_id, ga_ref, False, False)
        rank2, b_tops, _ = extract(s2, key_id, gb_ref, False, True)
        _, c_tops, _ = extract(staircase_candidates(ga_ref, gb_ref), None, None, False, False)
        tau = c_tops[k - 1]
        n1 = jnp.zeros_like(s1)
        for y in range(k):
            n1 = n1 + jnp.where((s1 + b_tops[y]) >= tau, 1.0, 0.0)
        emit(dst, s1, s2, rank2, n1, a_tops[0], b_tops[0], c_tops)
        tied = ((count(w1 == neg) != kf) | (count(rank2 < kf) != kf)
                | (jnp.sum(n1, axis=0, keepdims=True) != kf))
        any_tied.append(jnp.max(jnp.where(tied, 1.0, 0.0)) > 0.0)

    for gidx, (rows, dst) in enumerate(groups):
        @pl.when(any_tied[gidx])
        def _():
            ga_ref, gb_ref = a_ref.at[gidx], b_ref.at[gidx]
            s1, s2 = scores(rows)
            rank1, a_x, _ = extract(s1, key_id, ga_ref, True, True)
            rank2x, b_x, _ = extract(s2, key_id, gb_ref, True, True)
            rank3, c_x, _ = extract(candidates(ga_ref, gb_ref), cand_id, None, True, True)
            win = rank3 < kf
            n1x = jnp.zeros_like(s1)
            for x in range(k):
                n_x = count(win[x * k:(x + 1) * k])
                n1x = n1x + jnp.where(rank1 == float(x), n_x, 0.0)
            emit(dst, s1, s2, rank2x, n1x, a_x[0], b_x[0], c_x)


def _peer_route_call(q, subkeys, *, tt, tiles=PEER_ROUTE_TILES):
    t = q.shape[0]
    nt = t // tt
    spec = pl.BlockSpec((1, tiles, PEER_KEYS, tt), lambda i, h: (h, i, 0, 0))
    pair_spec = pl.BlockSpec((1, tiles, PEER_KEYS // 2, tt), lambda i, h: (h, i, 0, 0))
    tab = lambda rows: jax.ShapeDtypeStruct((PEER_HEADS, nt, rows, tt), jnp.uint32)
    return pl.pallas_call(
        functools.partial(_peer_route_kernel, tt=tt, tiles=tiles),
        grid=(nt // tiles, PEER_HEADS),
        in_specs=[
            pl.BlockSpec((tt * tiles, 2 * PEER_HALF), lambda i, h: (i, h)),
            pl.BlockSpec(subkeys.shape, lambda i, h: (0, 0, 0)),
        ],
        out_specs=[pair_spec, pair_spec, spec, spec],
        out_shape=[tab(PEER_KEYS // 2), tab(PEER_KEYS // 2), tab(PEER_KEYS), tab(PEER_KEYS)],
        scratch_shapes=[
            pltpu.VMEM((tiles * tt // LANES, PEER_TOPK, LANES), F32),
            pltpu.VMEM((tiles * tt // LANES, PEER_TOPK, LANES), F32),
            pltpu.VMEM((PEER_TOPK * PEER_TOPK, LANES), F32),
        ],
        compiler_params=_cparams(("arbitrary", "arbitrary")),
        name="peer_route",
    )(q, subkeys)


def _gelu(x):
    return 0.5 * x * (1.0 + lax.erf(x * (1.0 / math.sqrt(2.0))))


def _peer_expert_kernel(x_ref, gate_ref, ht_ref, u0_ref, u1_ref, vt0_ref, vt1_ref, n1_ref, f_ref, r2_ref,
                        e2_ref, o_ref, pre_a, pre_b, wa_ref, acc_ref, *, rows_per_tile):
    u_halves, vt_halves = (u0_ref, u1_ref), (vt0_ref, vt1_ref)
    s = pl.program_id(1)
    te, tt = pre_a.shape
    d = acc_ref.shape[0]
    blk = 16
    rows_per_piece = 2

    def fill_pre(pre_new):
        pre_rows = 512
        for m, u_ref in enumerate(u_halves):
            for r0 in range(0, te // 2, pre_rows):
                pre_new[m * te // 2 + r0:m * te // 2 + r0 + pre_rows, :] = _dot(u_ref[r0:r0 + pre_rows, :],
                                                                               ht_ref[...])

    def step(pre_new, pre_old):
        halves = [slice(m * d // 2, (m + 1) * d // 2) for m in range(2)]
        if pre_old is None:
            acc_ref[...] = jnp.zeros_like(acc_ref)
            fill_pre(pre_new)
            return
        zero = jnp.zeros((blk, LANES), BF16)
        as_bf16 = lambda words: pltpu.bitcast(words, BF16)
        for il in range(rows_per_tile):
            for lt in range(tt // LANES):
                lanes = slice(lt * LANES, (lt + 1) * LANES)
                row = lambda ref, h: as_bf16(jnp.broadcast_to(ref[h, 0, il:il + 1, lanes], (blk // 2, LANES)))
                n1b = [row(n1_ref, h) for h in range(PEER_HEADS)]
                fb = [row(f_ref, h) for h in range(PEER_HEADS)]
                for jb in range(PEER_KEYS // blk):
                    words = slice(jb * blk // 2, (jb + 1) * blk // 2)
                    w = zero
                    for h in range(PEER_HEADS):
                        hit = as_bf16(r2_ref[h, 0, words, lanes]) < n1b[h]
                        w = w + jnp.where(hit, as_bf16(e2_ref[h, 0, words, lanes]), zero) * fb[h]
                    rows = slice(il * PEER_KEYS + jb * blk, il * PEER_KEYS + (jb + 1) * blk)
                    wa_ref[rows, lanes] = w
        if pre_new is not None:
            fill_pre(pre_new)
        piece_rows = rows_per_piece * PEER_KEYS
        for p in range(te // piece_rows):
            piece = slice(p * piece_rows, (p + 1) * piece_rows)
            for r0 in range(p * piece_rows, (p + 1) * piece_rows, blk):
                rows = slice(r0, r0 + blk)
                wa_ref[rows, :] = wa_ref[rows, :] * _gelu(pre_old[rows, :]).astype(BF16)
            for rows, vt_ref in zip(halves, vt_halves):
                acc_ref[rows, :] += _dot(vt_ref[0, :, piece], wa_ref[piece, :])

    last = pl.num_programs(1) - 1
    inner = (s > 0) & (s < last)
    pl.when(s == 0)(lambda: step(pre_a, None))
    pl.when(inner & (s % 2 == 0))(lambda: step(pre_a, pre_b))
    pl.when(inner & (s % 2 == 1))(lambda: step(pre_b, pre_a))

    @pl.when(s == last)
    def _():
        step(None, pre_b)
        o_ref[...] = x_ref[...] + gate_ref[0] * acc_ref[...].T


def _peer_expert_call(x, gate, hb, u_b, vt_b, n1, f, r2, e2, seq, *, tt, te=1024):
    t, d = x.shape
    ne = u_b.shape[0] // te
    assert ne % 2 == 0, "the drain step reads the second pre-activation buffer"
    rows_per_tile = te // PEER_KEYS
    lag = lambda s, n: jnp.clip(s - n, 0, ne - 1)
    sel = pl.BlockSpec((PEER_HEADS, 1, rows_per_tile, tt), lambda i, s: (0, i, lag(s, 1), 0))
    full = pl.BlockSpec((PEER_HEADS, 1, PEER_KEYS // 2, tt), lambda i, s: (0, i, 0, 0))
    return pl.pallas_call(
        functools.partial(_peer_expert_kernel, rows_per_tile=rows_per_tile),
        grid=(t // tt, ne + 1),
        in_specs=[
            pl.BlockSpec((tt, d), lambda i, s: (i, 0)),
            pl.BlockSpec((1, 1, d), lambda i, s: ((i * tt) // seq, 0, 0)),
            pl.BlockSpec((d, tt), lambda i, s: (0, i)),
            pl.BlockSpec((te // 2, d), lambda i, s: (2 * lag(s, 0), 0)),
            pl.BlockSpec((te // 2, d), lambda i, s: (2 * lag(s, 0) + 1, 0)),
            pl.BlockSpec((1, d // 2, te), lambda i, s: (lag(s, 1), 0, 0)),
            pl.BlockSpec((1, d // 2, te), lambda i, s: (lag(s, 1), 1, 0)),
            sel, sel, full, full,
        ],
        out_specs=pl.BlockSpec((tt, d), lambda i, s: (i, 0)),
        out_shape=jax.ShapeDtypeStruct((t, d), F32),
        scratch_shapes=[
            pltpu.VMEM((te, tt), F32),
            pltpu.VMEM((te, tt), F32),
            pltpu.VMEM((te, tt), BF16),
            pltpu.VMEM((d, tt), F32),
        ],
        compiler_params=_cparams(("arbitrary", "arbitrary")),
        name="peer_experts",
    )(x, gate, hb, u_b, u_b, vt_b, vt_b, n1, f, r2, e2)


def _final_norm_kernel(x_ref, g_ref, o_ref):
    x = x_ref[...]
    ms = jnp.mean(x * x, axis=-1, keepdims=True)
    o_ref[...] = x * lax.rsqrt(ms + NORM_EPS) * g_ref[...]


def _final_norm_call(x, g, *, tm=512):
    t, d = x.shape
    return pl.pallas_call(
        _final_norm_kernel,
        grid=(t // tm,),
        in_specs=[pl.BlockSpec((tm, d), lambda i: (i, 0)), pl.BlockSpec((1, d), lambda i: (0, 0))],
        out_specs=pl.BlockSpec((tm, d), lambda i: (i, 0)),
        out_shape=jax.ShapeDtypeStruct((t, d), F32),
        compiler_params=_cparams(("arbitrary",)),
        name="final_norm",
    )(x, g.reshape(1, d))


def _pad_rows(a, rows):
    return jnp.pad(a, ((0, rows - a.shape[0]), (0, 0)))


def _pad_cols(a, cols):
    return jnp.pad(a, ((0, 0), (0, cols - a.shape[1])))


def _hybrid_mixer(x, g, shift, scale, batch, seq, w_in, conv_w, conv_b, conv_ln_g, conv_ln_b, mu, w0, w2,
                  a0, a2, g2, k_k, k_a, r_k, ln_g, ln_b):
    rw = RWKV_WIDTH
    c0 = 2 * CONV_WIDTH
    lora0 = c0 + 3 * rw
    l1, l2 = lora0 + DECAY_LORA, lora0 + DECAY_LORA + AAA_LORA
    w_conv = w_in[:, :c0].astype(BF16)
    w_rkv = w_in[:, c0:lora0].astype(BF16)
    w_lora = jnp.concatenate([
        _pad_cols(w_in[:, lora0:l1], LANES), _pad_cols(w_in[:, l1:l2], LANES),
        _pad_cols(w_in[:, l2:], 2 * LANES)], axis=1).astype(BF16)
    zc, zr, zl = _norm_matmul_call(x, g, shift, scale, [w_conv, w_rkv, w_lora], [F32, F32, F32], seq,
                                   name="hybrid_in_proj")
    ya = _conv_call(zc, conv_w, conv_b, conv_ln_g, conv_ln_b, batch, seq)
    mu_l = jnp.concatenate([
        _pad_cols(mu[None, lora0 - c0:l1 - c0], LANES), _pad_cols(mu[None, l1 - c0:l2 - c0], LANES),
        _pad_cols(mu[None, l2 - c0:], 2 * LANES)], axis=1)
    vec = lambda a: a.reshape(1, rw)
    head_id = jnp.arange(RWKV_GROUP * RWKV_HEAD) // RWKV_HEAD
    params = dict(
        mu_r=mu[None, :3 * rw], mu_l=mu_l, w0=vec(w0), a0=vec(a0), k_k=vec(k_k), k_a=vec(k_a),
        r_k=vec(r_k), ln_g=vec(ln_g), ln_b=vec(ln_b),
        w2=_pad_rows(w2, LANES).astype(BF16), a2=_pad_rows(a2, LANES).astype(BF16),
        g2=_pad_rows(g2, 2 * LANES).astype(BF16),
        seg=(head_id[:, None] == head_id[None, :]).astype(BF16))
    yb = _rwkv_call(zr, zl, params, batch, seq)
    return ya, yb


def kernel(x, c, positions, ada_w, ada_b, norm_mix_g, norm_ffn_g, hyb_w_in, conv_w, conv_b, conv_ln_g, conv_ln_b, rwkv_mu, rwkv_w0, rwkv_w2, rwkv_a0, rwkv_a2, rwkv_g2, rwkv_k_k, rwkv_k_a, rwkv_r_k, rwkv_ln_g, rwkv_ln_b, hyb_w_out, diff_w_qkv, diff_lq1, diff_lk1, diff_lq2, diff_lk2, diff_subln_g, diff_w_out, peer_w_q, peer_subkeys, peer_u, peer_v, final_g):
    batch, seq, d = x.shape
    depth = ada_w.shape[0]
    t = batch * seq
    xt = x.reshape(t, d)
    pos = positions.reshape(t, 1)
    mod = _ada_call(c, ada_w, ada_b).reshape(depth, batch, 6, 1, d)

    for l in range(depth):
        sh1, sc1, g1, sh2, sc2, g2 = (mod[l, :, j] for j in range(6))
        gmix = norm_mix_g[l].reshape(1, d)
        if l % 2 == 0:
            e = l // 2
            ya, yb = _hybrid_mixer(
                xt, gmix, sh1, sc1, batch, seq, hyb_w_in[e], conv_w[e], conv_b[e], conv_ln_g[e],
                conv_ln_b[e], rwkv_mu[e], rwkv_w0[e], rwkv_w2[e], rwkv_a0[e], rwkv_a2[e], rwkv_g2[e],
                rwkv_k_k[e], rwkv_k_a[e], rwkv_r_k[e], rwkv_ln_g[e], rwkv_ln_b[e])
            w_out = hyb_w_out[e].astype(BF16)
            xt = _out_proj_call(xt, g1, [ya.astype(BF16), yb.astype(BF16)],
                                [w_out[:CONV_WIDTH], w_out[CONV_WIDTH:]], seq)
        else:
            o = l // 2
            lambda_init = 0.8 - 0.6 * math.exp(-0.3 * l)
            wqkv = diff_w_qkv[o].astype(BF16)
            n = wqkv.shape[1] // 3
            q, k, v = _qkv_rope_call(xt, gmix, sh1, sc1, pos, wqkv[:, :n], wqkv[:, n:2 * n], wqkv[:, 2 * n:],
                                     seq)
            y = _attn_call(q, k, v, diff_lq1[o], diff_lk1[o], diff_lq2[o], diff_lk2[o], diff_subln_g[o],
                           lambda_init, batch, seq)
            xt = _out_proj_call(xt, g1, [y], [diff_w_out[o].astype(BF16)], seq)

        q, hb = _norm_matmul_call(xt, norm_ffn_g[l].reshape(1, d), sh2, sc2, [peer_w_q[l]], [F32], seq,
                                  split=True, emit_h=True, name="peer_query")
        r2, e2, n1, f = _peer_route_call(q, peer_subkeys[l], tt=PEER_TOKEN_TILE)
        v_tiles = peer_v[l].astype(BF16).reshape(-1, PEER_EXPERT_TILE, d).transpose(0, 2, 1)
        xt = _peer_expert_call(xt, g2, hb, peer_u[l].astype(BF16), v_tiles,
                               n1, f, r2, e2, seq, tt=PEER_TOKEN_TILE, te=PEER_EXPERT_TILE)

    return _final_norm_call(xt, final_g).reshape(batch, seq, d)
```
